```python
import math
import jax, jax.numpy as jnp
from jax import lax
import numpy as np

D_MODEL = 2048
BATCH = 4
SEQ = 2048
DEPTH = 2
DEC_BATCH = 128
DEC_SEQ = 4
PAST_LEN = 2048
PAGE_SIZE = 128

D_MIX = D_MODEL
D_ATTN = D_MIX // 2
HEAD_DIM = 128
N_HEADS = D_ATTN // HEAD_DIM
D_SSM = D_MIX - D_ATTN
SSM_GROUP = 16
N_SSM_GROUPS = D_SSM // SSM_GROUP
SSM_STATE = 64
D_IN = 3 * D_ATTN + N_HEADS + D_SSM
POOL_WINDOWS = (2, 4, 8, 16)
POOL_GROUP = D_MODEL // len(POOL_WINDOWS)
POOL_BUF = max(POOL_WINDOWS) - 1
D_FF = ((-(-8 * D_MODEL // 3) + 255) // 256) * 256
Q_BLOCK = 128
RMS_EPS = 1e-6
DT_MIN = 1e-3
DT_MAX = 1e-1

kernel_name = 'fox_s5_pool_hybrid_step'

F32 = jnp.float32


def rmsnorm(x, w):
    xf = x.astype(F32)
    y = xf * lax.rsqrt(jnp.mean(xf * xf, axis=-1, keepdims=True) + RMS_EPS)
    return (y * w.astype(F32)).astype(x.dtype)


def swiglu(h, w1, w3, w2):
    return (jax.nn.silu(h @ w1) * (h @ w3)) @ w2


def mixer_ab_projection(h, w_in, b_f, q_gain, k_gain):
    B, L, _ = h.shape
    z = h @ w_in
    q = z[..., :D_ATTN].reshape(B, L, N_HEADS, HEAD_DIM)
    k = z[..., D_ATTN:2 * D_ATTN].reshape(B, L, N_HEADS, HEAD_DIM)
    v = z[..., 2 * D_ATTN:3 * D_ATTN].reshape(B, L, N_HEADS, HEAD_DIM)
    f = z[..., 3 * D_ATTN:3 * D_ATTN + N_HEADS]
    u = z[..., 3 * D_ATTN + N_HEADS:]
    q = rmsnorm(q, q_gain)
    k = rmsnorm(k, k_gain)
    logf = jax.nn.log_sigmoid(f.astype(F32) + b_f.astype(F32))
    return q, k, v, logf, u


def fox_attend_prompt(q, k, v, logf):
    B, L, H, Dh = q.shape
    scale = Dh ** -0.5
    c = jnp.cumsum(logf, axis=1)
    c_t = c.transpose(0, 2, 1)
    k_pos = jnp.arange(L)

    def block(i):
        s0 = i * Q_BLOCK
        qb = lax.dynamic_slice_in_dim(q, s0, Q_BLOCK, axis=1)
        cb = lax.dynamic_slice_in_dim(c_t, s0, Q_BLOCK, axis=2)
        q_pos = s0 + jnp.arange(Q_BLOCK)
        logits = jnp.einsum('bqhd,bkhd->bhqk', qb, k, preferred_element_type=F32) * scale
        logits = logits + (cb[..., :, None] - c_t[..., None, :])
        logits = jnp.where(k_pos[None, :] <= q_pos[:, None], logits, -jnp.inf)
        p = jax.nn.softmax(logits, axis=-1).astype(v.dtype)
        return jnp.einsum('bhqk,bkhd->bqhd', p, v)

    out = lax.map(block, jnp.arange(L // Q_BLOCK))
    return out.transpose(1, 0, 2, 3, 4).reshape(B, L, H * Dh)


def fox_attend_sample(q, k_new, v_new, logf_new, k_past, v_past, logf_past):
    B, Q, H, Dh = q.shape
    P = k_past.shape[1]
    scale = Dh ** -0.5
    c_past = jnp.cumsum(logf_past.astype(F32), axis=1)
    c_new = c_past[:, -1:] + jnp.cumsum(logf_new.astype(F32), axis=1)
    cp = c_past.transpose(0, 2, 1)
    cn = c_new.transpose(0, 2, 1)
    s_past = jnp.einsum('bqhd,bkhd->bhqk', q, k_past, preferred_element_type=F32) * scale
    s_past = s_past + (cn[..., :, None] - cp[..., None, :])
    s_new = jnp.einsum('bqhd,bkhd->bhqk', q, k_new, preferred_element_type=F32) * scale
    s_new = s_new + (cn[..., :, None] - cn[..., None, :])
    causal = jnp.tril(jnp.ones((Q, Q), dtype=bool))
    s_new = jnp.where(causal, s_new, -jnp.inf)
    p = jax.nn.softmax(jnp.concatenate([s_past, s_new], axis=-1), axis=-1).astype(v_new.dtype)
    out = (jnp.einsum('bhqk,bkhd->bqhd', p[..., :P], v_past)
           + jnp.einsum('bhqk,bkhd->bqhd', p[..., P:], v_new))
    return out.reshape(B, Q, H * Dh)


def _complex_affine_combine(left, right):
    a1r, a1i, b1r, b1i = left
    a2r, a2i, b2r, b2i = right
    return (a1r * a2r - a1i * a2i,
            a1r * a2i + a1i * a2r,
            a2r * b1r - a2i * b1i + b2r,
            a2r * b1i + a2i * b1r + b2i)


def s5_mixer(u, h0_re, h0_im, lam_re, lam_im, log_dt, b_re, b_im, c_re, c_im, d, w_glu, b_glu):
    B, L, _ = u.shape
    uf = u.astype(F32).reshape(B, L, N_SSM_GROUPS, SSM_GROUP)
    dt = jnp.exp(log_dt.astype(F32))[:, None]
    lr, li = lam_re.astype(F32), lam_im.astype(F32)
    mag = jnp.exp(lr * dt)
    ar, ai = mag * jnp.cos(li * dt), mag * jnp.sin(li * dt)
    den = lr * lr + li * li
    zr = ((ar - 1.0) * lr + ai * li) / den
    zi = (ai * lr - (ar - 1.0) * li) / den
    br, bi = b_re.astype(F32), b_im.astype(F32)
    bbr = zr[..., None] * br - zi[..., None] * bi
    bbi = zr[..., None] * bi + zi[..., None] * br
    xr = jnp.einsum('blgc,gpc->blgp', uf, bbr)
    xi = jnp.einsum('blgc,gpc->blgp', uf, bbi)
    h0r, h0i = h0_re.astype(F32), h0_im.astype(F32)
    xr = xr.at[:, 0].add(ar * h0r - ai * h0i)
    xi = xi.at[:, 0].add(ar * h0i + ai * h0r)
    a_r = jnp.broadcast_to(ar, xr.shape)
    a_i = jnp.broadcast_to(ai, xr.shape)
    _, _, hr, hi = lax.associative_scan(_complex_affine_combine, (a_r, a_i, xr, xi), axis=1)
    y = (jnp.einsum('blgp,gcp->blgc', hr, c_re.astype(F32))
         - jnp.einsum('blgp,gcp->blgc', hi, c_im.astype(F32))
         + d.astype(F32) * uf)
    y = y.reshape(B, L, D_SSM)
    z = jax.nn.gelu(y)
    out = z * jax.nn.sigmoid(z @ w_glu.astype(F32) + b_glu.astype(F32))
    return out.astype(u.dtype), hr[:, -1], hi[:, -1]


def pool_mixer(hp, pos, pool_w, pool_scale):
    L = hp.shape[1] - POOL_BUF
    s = jnp.cumsum(hp.astype(F32), axis=1)
    s = jnp.pad(s, ((0, 0), (1, 0), (0, 0)))
    cur = hp[:, POOL_BUF:].astype(F32)
    outs = []
    for g, w in enumerate(POOL_WINDOWS):
        lo_c, hi_c = g * POOL_GROUP, (g + 1) * POOL_GROUP
        upper = s[:, POOL_BUF + 1:POOL_BUF + 1 + L, lo_c:hi_c]
        lower = s[:, POOL_BUF + 1 - w:POOL_BUF + 1 - w + L, lo_c:hi_c]
        cnt = jnp.minimum(pos + 1, w).astype(F32)[None, :, None]
        pooled = (upper - lower) / cnt - cur[..., lo_c:hi_c]
        outs.append(pooled @ pool_w[g].astype(F32))
    out = jnp.concatenate(outs, axis=-1) * pool_scale.astype(F32)
    return out.astype(hp.dtype)


def setup_inputs(seed: int = 0) -> dict:
    key = jax.random.key(seed)
    ks = jax.random.split(key, 32)
    n_attn = (DEPTH + 1) // 2
    n_pool = DEPTH // 2
    n_pages = PAST_LEN // PAGE_SIZE
    n_phys = (5 * DEC_BATCH * n_pages + 3) // 4
    nrm = jax.random.normal
    x_prompt = nrm(ks[0], (BATCH, SEQ, D_MODEL), F32)
    x_sample = nrm(ks[1], (DEC_BATCH, DEC_SEQ, D_MODEL), F32)
    cache_k = nrm(ks[2], (n_attn, n_phys, PAGE_SIZE, N_HEADS, HEAD_DIM), F32)
    cache_v = nrm(ks[3], (n_attn, n_phys, PAGE_SIZE, N_HEADS, HEAD_DIM), F32)
    cache_logf = jax.nn.log_sigmoid(3.0 + nrm(ks[4], (n_attn, n_phys, PAGE_SIZE, N_HEADS), F32))
    state_s5_re = 0.1 * nrm(ks[5], (n_attn, DEC_BATCH, N_SSM_GROUPS, SSM_STATE), F32)
    state_s5_im = 0.1 * nrm(ks[6], (n_attn, DEC_BATCH, N_SSM_GROUPS, SSM_STATE), F32)
    state_pool = nrm(ks[7], (n_pool, DEC_BATCH, POOL_BUF, D_MODEL), F32)
    page_table = jax.random.permutation(ks[8], n_phys)[:DEC_BATCH * n_pages].reshape(DEC_BATCH, n_pages).astype(jnp.int32)
    norm_mix_w = 1.0 + 0.02 * nrm(ks[9], (DEPTH, D_MODEL), F32)
    norm_ffn_w = 1.0 + 0.02 * nrm(ks[10], (DEPTH, D_MODEL), F32)
    w_in = nrm(ks[11], (n_attn, D_MODEL, D_IN), F32) * D_MODEL ** -0.5
    b_f = 3.0 + 0.5 * nrm(ks[12], (n_attn, N_HEADS), F32)
    q_norm_w = 1.0 + 0.02 * nrm(ks[13], (n_attn, HEAD_DIM), F32)
    k_norm_w = 1.0 + 0.02 * nrm(ks[14], (n_attn, HEAD_DIM), F32)
    s5_lam_re = -0.5 + 0.01 * nrm(ks[15], (n_attn, N_SSM_GROUPS, SSM_STATE), F32)
    s5_lam_im = jnp.pi * jnp.arange(SSM_STATE, dtype=F32) + 0.01 * nrm(ks[16], (n_attn, N_SSM_GROUPS, SSM_STATE), F32)
    s5_log_dt = jax.random.uniform(ks[17], (n_attn, N_SSM_GROUPS), F32, math.log(DT_MIN), math.log(DT_MAX))
    s5_b_re = nrm(ks[18], (n_attn, N_SSM_GROUPS, SSM_STATE, SSM_GROUP), F32) * (2 * SSM_GROUP) ** -0.5
    s5_b_im = nrm(ks[19], (n_attn, N_SSM_GROUPS, SSM_STATE, SSM_GROUP), F32) * (2 * SSM_GROUP) ** -0.5
    s5_c_re = nrm(ks[20], (n_attn, N_SSM_GROUPS, SSM_GROUP, SSM_STATE), F32) * SSM_STATE ** -0.5
    s5_c_im = nrm(ks[21], (n_attn, N_SSM_GROUPS, SSM_GROUP, SSM_STATE), F32) * SSM_STATE ** -0.5
    s5_d = nrm(ks[22], (n_attn, N_SSM_GROUPS, SSM_GROUP), F32)
    w_glu = nrm(ks[23], (n_attn, D_SSM, D_SSM), F32) * D_SSM ** -0.5
    b_glu = 0.02 * nrm(ks[24], (n_attn, D_SSM), F32)
    w_out = nrm(ks[25], (n_attn, D_MIX, D_MODEL), F32) * D_MIX ** -0.5
    pool_w = nrm(ks[26], (n_pool, len(POOL_WINDOWS), POOL_GROUP, POOL_GROUP), F32) * POOL_GROUP ** -0.5
    pool_scale = 1.0 + 0.02 * nrm(ks[27], (n_pool, D_MODEL), F32)
    ffn_w1 = nrm(ks[28], (DEPTH, D_MODEL, D_FF), F32) * D_MODEL ** -0.5
    ffn_w3 = nrm(ks[29], (DEPTH, D_MODEL, D_FF), F32) * D_MODEL ** -0.5
    ffn_w2 = nrm(ks[30], (DEPTH, D_FF, D_MODEL), F32) * D_FF ** -0.5
    return {'x_prompt': x_prompt, 'x_sample': x_sample, 'cache_k': cache_k, 'cache_v': cache_v,
            'cache_logf': cache_logf, 'state_s5_re': state_s5_re, 'state_s5_im': state_s5_im,
            'state_pool': state_pool, 'page_table': page_table,
            'norm_mix_w': norm_mix_w, 'norm_ffn_w': norm_ffn_w, 'w_in': w_in, 'b_f': b_f,
            'q_norm_w': q_norm_w, 'k_norm_w': k_norm_w, 's5_lam_re': s5_lam_re, 's5_lam_im': s5_lam_im,
            's5_log_dt': s5_log_dt, 's5_b_re': s5_b_re, 's5_b_im': s5_b_im, 's5_c_re': s5_c_re,
            's5_c_im': s5_c_im, 's5_d': s5_d, 'w_glu': w_glu, 'b_glu': b_glu, 'w_out': w_out,
            'pool_w': pool_w, 'pool_scale': pool_scale, 'ffn_w1': ffn_w1, 'ffn_w3': ffn_w3, 'ffn_w2': ffn_w2}


def reference(x_prompt, x_sample, cache_k, cache_v, cache_logf, state_s5_re, state_s5_im, state_pool,
              page_table, norm_mix_w, norm_ffn_w, w_in, b_f, q_norm_w, k_norm_w, s5_lam_re, s5_lam_im,
              s5_log_dt, s5_b_re, s5_b_im, s5_c_re, s5_c_im, s5_d, w_glu, b_glu, w_out, pool_w,
              pool_scale, ffn_w1, ffn_w3, ffn_w2):
    bp, lp, _ = x_prompt.shape
    bs, ls, _ = x_sample.shape
    n_pages = page_table.shape[1]
    past_len = n_pages * cache_k.shape[2]
    pos_p = jnp.arange(lp)
    pos_s = past_len + jnp.arange(ls)
    xp, xs = x_prompt, x_sample
    kp, vp, fp, srp, sip, plp = [], [], [], [], [], []
    ksm, vsm, fsm, srs, sis, pls = [], [], [], [], [], []
    for layer in range(DEPTH):
        i = layer // 2
        hp = rmsnorm(xp, norm_mix_w[layer])
        hs = rmsnorm(xs, norm_mix_w[layer])
        if layer % 2 == 0:
            ssm_params = (s5_lam_re[i], s5_lam_im[i], s5_log_dt[i], s5_b_re[i], s5_b_im[i],
                          s5_c_re[i], s5_c_im[i], s5_d[i], w_glu[i], b_glu[i])
            q, k, v, logf, u = mixer_ab_projection(hp, w_in[i], b_f[i], q_norm_w[i], k_norm_w[i])
            att = fox_attend_prompt(q, k, v, logf)
            h0 = jnp.zeros((bp, N_SSM_GROUPS, SSM_STATE), F32)
            ssm, h_re, h_im = s5_mixer(u, h0, h0, *ssm_params)
            xp = xp + jnp.concatenate([att, ssm], axis=-1) @ w_out[i]
            kp.append(k); vp.append(v); fp.append(logf); srp.append(h_re); sip.append(h_im)
            q, k, v, logf, u = mixer_ab_projection(hs, w_in[i], b_f[i], q_norm_w[i], k_norm_w[i])
            k_past = cache_k[i, page_table].reshape(bs, past_len, N_HEADS, HEAD_DIM)
            v_past = cache_v[i, page_table].reshape(bs, past_len, N_HEADS, HEAD_DIM)
            f_past = cache_logf[i, page_table].reshape(bs, past_len, N_HEADS)
            att = fox_attend_sample(q, k, v, logf, k_past.astype(k.dtype), v_past.astype(v.dtype), f_past)
            ssm, h_re, h_im = s5_mixer(u, state_s5_re[i], state_s5_im[i], *ssm_params)
            xs = xs + jnp.concatenate([att, ssm], axis=-1) @ w_out[i]
            ksm.append(k); vsm.append(v); fsm.append(logf); srs.append(h_re); sis.append(h_im)
        else:
            ext_p = jnp.concatenate([jnp.zeros((bp, POOL_BUF, D_MODEL), hp.dtype), hp], axis=1)
            xp = xp + pool_mixer(ext_p, pos_p, pool_w[i], pool_scale[i])
            plp.append(ext_p[:, -POOL_BUF:])
            ext_s = jnp.concatenate([state_pool[i].astype(hs.dtype), hs], axis=1)
            xs = xs + pool_mixer(ext_s, pos_s, pool_w[i], pool_scale[i])
            pls.append(ext_s[:, -POOL_BUF:])
        xp = xp + swiglu(rmsnorm(xp, norm_ffn_w[layer]), ffn_w1[layer], ffn_w3[layer], ffn_w2[layer])
        xs = xs + swiglu(rmsnorm(xs, norm_ffn_w[layer]), ffn_w1[layer], ffn_w3[layer], ffn_w2[layer])
    return (xp, xs,
            jnp.stack(kp), jnp.stack(vp), jnp.stack(fp), jnp.stack(srp), jnp.stack(sip), jnp.stack(plp),
            jnp.stack(ksm), jnp.stack(vsm), jnp.stack(fsm), jnp.stack(srs), jnp.stack(sis), jnp.stack(pls))
```

```python
import functools
import math

import jax
import jax.numpy as jnp
from jax import lax
from jax.experimental import pallas as pl
from jax.experimental.pallas import tpu as pltpu

F32 = jnp.float32
BF16 = jnp.bfloat16

D_MODEL = 2048
D_ATTN = D_MODEL // 2
HEAD_DIM = 128
N_HEADS = D_ATTN // HEAD_DIM
D_SSM = D_MODEL - D_ATTN
SSM_GROUP = 16
N_SSM_GROUPS = D_SSM // SSM_GROUP
SSM_STATE = 64
POOL_WINDOWS = (2, 4, 8, 16)
POOL_GROUP = D_MODEL // len(POOL_WINDOWS)
POOL_BUF = max(POOL_WINDOWS) - 1
RMS_EPS = 1e-6

LANES = 128
MXU_DIM = 256
SLAB = MXU_DIM
SLAB_STATES = (SLAB // SSM_GROUP) * SSM_STATE
N_SLABS = D_SSM // SLAB
VMEM_BYTES = 64 * 1024 * 1024


def _params(vmem_mb, n_axes):
    return pltpu.CompilerParams(dimension_semantics=("arbitrary",) * n_axes,
                                vmem_limit_bytes=min(vmem_mb * 1024 * 1024, VMEM_BYTES - 4 * 1024 * 1024))


def _rms(x, w):
    return x * lax.rsqrt(jnp.mean(x * x, axis=-1, keepdims=True) + RMS_EPS) * w


def _log_sigmoid(x):
    return jnp.minimum(x, 0.0) - jnp.log1p(jnp.exp(-jnp.abs(x)))


def _sigmoid(x):
    return 1.0 / (1.0 + jnp.exp(-x))


def _split3(x):
    hi = x.astype(BF16)
    r = x - hi.astype(F32)
    mid = r.astype(BF16)
    lo = (r - mid.astype(F32)).astype(BF16)
    return hi, mid, lo


def _cumsum_rows(x):
    n = x.shape[0]
    tri = (lax.broadcasted_iota(jnp.int32, (n, n), 1) <= lax.broadcasted_iota(jnp.int32, (n, n), 0)).astype(BF16)
    hi, mid, lo = _split3(x)
    z = jnp.dot(tri, jnp.concatenate([hi, mid, lo], axis=1), preferred_element_type=F32)
    return z[:, :LANES] + z[:, LANES:2 * LANES] + z[:, 2 * LANES:]


def _cumsum_lanes(x):
    m, n = x.shape
    tri = (lax.broadcasted_iota(jnp.int32, (n, n), 0) <= lax.broadcasted_iota(jnp.int32, (n, n), 1)).astype(BF16)
    hi, mid, lo = _split3(x)
    z = jnp.dot(jnp.concatenate([hi, mid, lo], axis=0), tri, preferred_element_type=F32)
    return z[:m] + z[m:2 * m] + z[2 * m:]


def _inproj_kernel(x_ref, nw_ref, wq_ref, wk_ref, wv_ref, wu_ref, wf_ref, bf_ref, qg_ref, kg_ref,
                   q_ref, k_ref, kb_ref, v_ref, vb_ref, u_ref, lf_ref, *rest, tiles_per_seq):
    h = _rms(x_ref[...], nw_ref[...]).astype(BF16)
    scale = HEAD_DIM ** -0.5
    for j in range(D_ATTN // MXU_DIM):
        cols = slice(j * MXU_DIM, (j + 1) * MXU_DIM)
        zq = jnp.dot(h, wq_ref[:, cols], preferred_element_type=F32)
        zk = jnp.dot(h, wk_ref[:, cols], preferred_element_type=F32)
        zv = jnp.dot(h, wv_ref[:, cols], preferred_element_type=F32)
        for t in range(MXU_DIM // HEAD_DIM):
            sub = slice(t * HEAD_DIM, (t + 1) * HEAD_DIM)
            dst = slice(j * MXU_DIM + t * HEAD_DIM, j * MXU_DIM + (t + 1) * HEAD_DIM)
            q_ref[:, dst] = (_rms(zq[:, sub], qg_ref[...]) * scale).astype(BF16)
            kn = _rms(zk[:, sub], kg_ref[...])
            k_ref[:, dst] = kn
            kb_ref[:, dst] = kn.astype(BF16)
        v_ref[:, cols] = zv
        vb_ref[:, cols] = zv.astype(BF16)
        u_ref[:, cols] = jnp.dot(h, wu_ref[:, cols], preferred_element_type=F32)
    lf = _log_sigmoid(jnp.dot(h, wf_ref[...], preferred_element_type=F32) + bf_ref[...])
    lf_ref[...] = lf
    if tiles_per_seq is not None:
        c_ref, carry_ref = rest

        @pl.when(pl.program_id(0) % tiles_per_seq == 0)
        def _():
            carry_ref[...] = jnp.zeros_like(carry_ref)

        c = _cumsum_rows(lf) + carry_ref[...]
        c_ref[...] = c
        carry_ref[...] = c[c.shape[0] - 1:, :]


def _inproj(x, nw, wq, wk, wv, wu, wf, bf, qg, kg, seq_len):
    R = x.shape[0]
    tm = min(R, 512)
    row = lambda n: pl.BlockSpec((tm, n), lambda i: (i, 0))
    full = lambda a: pl.BlockSpec(a.shape, lambda i: (0,) * a.ndim, pipeline_mode=pl.Buffered(1))
    out_shape = [jax.ShapeDtypeStruct((R, D_ATTN), BF16), jax.ShapeDtypeStruct((R, D_ATTN), F32),
                 jax.ShapeDtypeStruct((R, D_ATTN), BF16), jax.ShapeDtypeStruct((R, D_ATTN), F32),
                 jax.ShapeDtypeStruct((R, D_ATTN), BF16), jax.ShapeDtypeStruct((R, D_SSM), F32),
                 jax.ShapeDtypeStruct((R, LANES), F32)]
    out_specs = [row(D_ATTN)] * 5 + [row(D_SSM), row(LANES)]
    scratch = []
    tiles_per_seq = None
    if seq_len is not None:
        assert seq_len % tm == 0
        tiles_per_seq = seq_len // tm
        out_shape.append(jax.ShapeDtypeStruct((R, LANES), F32))
        out_specs.append(row(LANES))
        scratch.append(pltpu.VMEM((1, LANES), F32))
    return pl.pallas_call(
        functools.partial(_inproj_kernel, tiles_per_seq=tiles_per_seq),
        grid=(R // tm,),
        in_specs=[row(D_MODEL), full(nw), full(wq), full(wk), full(wv), full(wu), full(wf), full(bf), full(qg), full(kg)],
        out_specs=out_specs, out_shape=out_shape, scratch_shapes=scratch,
        compiler_params=_params(60, 1), name="inproj",
    )(x, nw, wq, wk, wv, wu, wf, bf, qg, kg)


def _attn_prompt_kernel(q_ref, k_ref, v_ref, cq_ref, ck_ref, o_ref, *, L, tq):
    nt = (((1,), (1,)), ((), ()))
    for qi in range(L // tq):
        rows = slice(qi * tq, (qi + 1) * tq)
        q = q_ref[rows, :]
        cq = cq_ref[rows, :]
        m = l = acc = None
        for ki in range(qi + 1):
            cols = slice(ki * tq, (ki + 1) * tq)
            s = lax.dot_general(q, k_ref[cols, :], nt, preferred_element_type=F32)
            s = s + (cq - ck_ref[:, cols])
            if ki == qi:
                keep = lax.broadcasted_iota(jnp.int32, (tq, tq), 1) <= lax.broadcasted_iota(jnp.int32, (tq, tq), 0)
                s = jnp.where(keep, s, -jnp.inf)
            mx = jnp.max(s, axis=-1, keepdims=True)
            if ki == 0:
                m = mx
                p = jnp.exp(s - m)
                l = jnp.sum(p, axis=-1, keepdims=True)
                acc = jnp.dot(p.astype(BF16), v_ref[cols, :], preferred_element_type=F32)
            else:
                m_new = jnp.maximum(m, mx)
                alpha = jnp.exp(m - m_new)
                p = jnp.exp(s - m_new)
                l = alpha * l + jnp.sum(p, axis=-1, keepdims=True)
                acc = alpha * acc + jnp.dot(p.astype(BF16), v_ref[cols, :], preferred_element_type=F32)
                m = m_new
        o_ref[rows, :] = (acc / l).astype(BF16)


def _attn_prompt(qb, kb, vb, cq, ck, B, L):
    tq = min(L, 512)
    blk = pl.BlockSpec((L, HEAD_DIM), lambda b, h: (b, h))
    return pl.pallas_call(
        functools.partial(_attn_prompt_kernel, L=L, tq=tq),
        grid=(B, N_HEADS),
        in_specs=[blk, blk, blk,
                  pl.BlockSpec((None, None, L, 1), lambda b, h: (b, h, 0, 0)),
                  pl.BlockSpec((None, None, 1, L), lambda b, h: (b, h, 0, 0))],
        out_specs=blk, out_shape=jax.ShapeDtypeStruct((B * L, D_ATTN), BF16),
        compiler_params=_params(32, 2), name="attn_prompt",
    )(qb, kb, vb, cq, ck)


def _attn_sample_kernel(pt_ref, q_ref, kn_ref, vn_ref, lfn_ref, *rest, n_pages, Q):
    k_refs = rest[:n_pages]
    v_refs = rest[n_pages:2 * n_pages]
    lf_refs = rest[2 * n_pages:3 * n_pages]
    o_ref = rest[3 * n_pages]
    nt = (((1,), (1,)), ((), ()))
    H = N_HEADS
    M = Q * H
    page = k_refs[0].shape[0]

    qrep = jnp.concatenate([jnp.broadcast_to(q_ref[i].astype(F32), (H, D_ATTN)) for i in range(Q)], axis=0)
    head_of_lane = lax.broadcasted_iota(jnp.int32, (M, D_ATTN), 1) // HEAD_DIM
    head_of_row = lax.broadcasted_iota(jnp.int32, (M, D_ATTN), 0) % H
    same_head = head_of_lane == head_of_row
    qbd = jnp.where(same_head, qrep, 0.0).astype(BF16)

    lf_all = jnp.concatenate([r[...] for r in lf_refs], axis=0)
    c_loc = _cumsum_lanes(lf_all)
    carry = jnp.zeros((H, 1), F32)
    c_pages = []
    for j in range(n_pages):
        cj = c_loc[j * H:(j + 1) * H, :] + carry
        c_pages.append(cj)
        carry = cj[:, page - 1:page]
    c_tot = carry

    lfn = lfn_ref[...]
    cum = [lfn[:, 0:1]]
    for i in range(1, Q):
        cum.append(cum[-1] + lfn[:, i:i + 1])
    cn_cols = [c_tot + c for c in cum]
    cn_q = jnp.concatenate(cn_cols, axis=0)
    cn_k = jnp.concatenate([jnp.concatenate(cn_cols, axis=1)] * Q, axis=0)

    s_pages = []
    for j in range(n_pages):
        kb = k_refs[j][...].astype(BF16)
        s = lax.dot_general(qbd, kb, nt, preferred_element_type=F32)
        s_pages.append(s + (cn_q - jnp.concatenate([c_pages[j]] * Q, axis=0)))
    qbd_f = qbd.astype(F32)
    s_new = jnp.concatenate(
        [jnp.sum(qbd_f * kn_ref[i].astype(BF16).astype(F32), axis=-1, keepdims=True) for i in range(Q)], axis=1)
    s_new = s_new + (cn_q - cn_k)
    q_of_row = lax.broadcasted_iota(jnp.int32, (M, Q), 0) // H
    s_new = jnp.where(lax.broadcasted_iota(jnp.int32, (M, Q), 1) <= q_of_row, s_new, -jnp.inf)

    m = jnp.max(s_new, axis=-1, keepdims=True)
    for s in s_pages:
        m = jnp.maximum(m, jnp.max(s, axis=-1, keepdims=True))
    p_new = jnp.exp(s_new - m)
    l = jnp.sum(p_new, axis=-1, keepdims=True)
    p_new_r = p_new.astype(BF16).astype(F32)
    acc = p_new_r[:, 0:1] * vn_ref[0].astype(BF16).astype(F32)
    for i in range(1, Q):
        acc = acc + p_new_r[:, i:i + 1] * vn_ref[i].astype(BF16).astype(F32)
    for j in range(n_pages):
        p = jnp.exp(s_pages[j] - m)
        l = l + jnp.sum(p, axis=-1, keepdims=True)
        acc = acc + jnp.dot(p.astype(BF16), v_refs[j][...].astype(BF16), preferred_element_type=F32)
    out = jnp.where(same_head, acc / l, 0.0)
    for i in range(Q):
        o_ref[i] = jnp.sum(out[i * H:(i + 1) * H, :], axis=0, keepdims=True).astype(BF16)


def _attn_sample(page_table_flat, qb, k_new, v_new, lfn_t, cache_k, cache_v, cache_lft, layer, n_phys, n_pages, Bd, Q):
    page = cache_k.shape[1]
    base = layer * n_phys

    def paged(width_shape):
        return [pl.BlockSpec((None,) + width_shape, functools.partial(
            lambda b, pt, j: (pt[b * n_pages + j] + base, 0, 0), j=j)) for j in range(n_pages)]

    tok = lambda n: pl.BlockSpec((Q, None, 1, n), lambda b, pt: (0, b, 0, 0))
    qb, k_new, v_new = (a.reshape(Q, Bd, 1, D_ATTN) for a in (qb, k_new, v_new))
    grid_spec = pltpu.PrefetchScalarGridSpec(
        num_scalar_prefetch=1, grid=(Bd,),
        in_specs=[tok(D_ATTN), tok(D_ATTN), tok(D_ATTN), pl.BlockSpec((None, N_HEADS, Q), lambda b, pt: (b, 0, 0))]
        + paged((page, D_ATTN)) + paged((page, D_ATTN)) + paged((N_HEADS, page)),
        out_specs=tok(D_ATTN))
    return pl.pallas_call(
        functools.partial(_attn_sample_kernel, n_pages=n_pages, Q=Q),
        grid_spec=grid_spec, out_shape=jax.ShapeDtypeStruct((Q, Bd, 1, D_ATTN), BF16),
        compiler_params=_params(56, 1), name="attn_sample",
    )(page_table_flat, qb, k_new, v_new, lfn_t, *([cache_k] * n_pages), *([cache_v] * n_pages), *([cache_lft] * n_pages))


def _s5_kernel(u_ref, h0r_ref, h0i_ref, lr_ref, li_ref, ldt_ref, bdr_ref, bdi_ref, cdr_ref, cdi_ref, d_ref,
               y_ref, fr_ref, fi_ref, xr_ref, xi_ref, *scratch, T, NC, chained):
    lr, li = lr_ref[...], li_ref[...]
    dt = jnp.exp(ldt_ref[...])
    mag = jnp.exp(lr * dt)
    ar, ai = mag * jnp.cos(li * dt), mag * jnp.sin(li * dt)
    den = lr * lr + li * li
    zr = ((ar - 1.0) * lr + ai * li) / den
    zi = (ai * lr - (ar - 1.0) * li) / den
    bdr, bdi = bdr_ref[...], bdi_ref[...]
    pr = (zr * bdr - zi * bdi).astype(BF16)
    pi = (zr * bdi + zi * bdr).astype(BF16)
    u = u_ref[...]
    ub = u.astype(BF16)
    n_cb = SLAB_STATES // LANES
    per = MXU_DIM // LANES
    for c2 in range(n_cb // per):
        cols = slice(c2 * MXU_DIM, (c2 + 1) * MXU_DIM)
        x_r = jnp.dot(ub, pr[:, cols], preferred_element_type=F32)
        x_i = jnp.dot(ub, pi[:, cols], preferred_element_type=F32)
        for t in range(per):
            xr_ref[c2 * per + t] = x_r[:, t * LANES:(t + 1) * LANES]
            xi_ref[c2 * per + t] = x_i[:, t * LANES:(t + 1) * LANES]

    def step_rows(s):
        return pl.ds(s, NC, stride=T) if chained else pl.ds(s * NC, NC)

    def sweep(cb, hr, hi, store):
        cols = slice(cb * LANES, (cb + 1) * LANES)
        a_r, a_i = ar[:, cols], ai[:, cols]
        for s in range(T):
            rows = step_rows(s)
            xr, xi = xr_ref[cb, rows, :], xi_ref[cb, rows, :]
            hr, hi = a_r * hr - a_i * hi + xr, a_r * hi + a_i * hr + xi
            if store:
                xr_ref[cb, rows, :] = hr
                xi_ref[cb, rows, :] = hi
        return hr, hi

    if chained:
        er_ref, ei_ref, gr_ref, gi_ref = scratch
        zero = jnp.zeros((NC, LANES), F32)
        for cb in range(n_cb):
            cols = slice(cb * LANES, (cb + 1) * LANES)
            er_ref[:, cols], ei_ref[:, cols] = sweep(cb, zero, zero, False)
        tr, ti = ar, ai
        for _ in range(T - 1):
            tr, ti = tr * ar - ti * ai, tr * ai + ti * ar

        def chunk_step(k, carry):
            sr, si = carry
            gr_ref[pl.ds(k, 1), :] = sr
            gi_ref[pl.ds(k, 1), :] = si
            er, ei = er_ref[pl.ds(k, 1), :], ei_ref[pl.ds(k, 1), :]
            return tr * sr - ti * si + er, tr * si + ti * sr + ei

        sr, si = lax.fori_loop(0, NC, chunk_step, (h0r_ref[...], h0i_ref[...]))
        fr_ref[...] = sr
        fi_ref[...] = si
        for cb in range(n_cb):
            cols = slice(cb * LANES, (cb + 1) * LANES)
            sweep(cb, gr_ref[:, cols], gi_ref[:, cols], True)
    else:
        for cb in range(n_cb):
            cols = slice(cb * LANES, (cb + 1) * LANES)
            hr, hi = sweep(cb, h0r_ref[:, cols], h0i_ref[:, cols], True)
            fr_ref[:, cols] = hr
            fi_ref[:, cols] = hi

    y = d_ref[...] * u
    for c2 in range(n_cb // per):
        rows = slice(c2 * MXU_DIM, (c2 + 1) * MXU_DIM)
        h_r = jnp.concatenate([xr_ref[c2 * per + t] for t in range(per)], axis=1).astype(BF16)
        h_i = jnp.concatenate([xi_ref[c2 * per + t] for t in range(per)], axis=1).astype(BF16)
        y = y + (jnp.dot(h_r, cdr_ref[rows, :], preferred_element_type=F32)
                 - jnp.dot(h_i, cdi_ref[rows, :], preferred_element_type=F32))
    y_ref[...] = y


def _s5(u, h0r, h0i, lam_re, lam_im, log_dt, bd_re, bd_im, cd_re, cd_im, d_row, n_seq, T, NC, chained):
    rows = NC * T
    slab_row = lambda: pl.BlockSpec((1, SLAB_STATES), lambda b, s: (0, s))
    if chained:
        st_spec = pl.BlockSpec((None, 1, SLAB_STATES), lambda b, s: (b, 0, s))
        st_shape = jax.ShapeDtypeStruct((n_seq, 1, N_SLABS * SLAB_STATES), F32)
        scratch = [pltpu.VMEM((NC, SLAB_STATES), F32)] * 4
    else:
        st_spec = pl.BlockSpec((NC, SLAB_STATES), lambda b, s: (0, s))
        st_shape = jax.ShapeDtypeStruct((NC, N_SLABS * SLAB_STATES), F32)
        scratch = []
    return pl.pallas_call(
        functools.partial(_s5_kernel, T=T, NC=NC, chained=chained),
        grid=(n_seq, N_SLABS),
        in_specs=[pl.BlockSpec((rows, SLAB), lambda b, s: (b, s)), st_spec, st_spec,
                  slab_row(), slab_row(), slab_row(),
                  pl.BlockSpec((None, SLAB, SLAB_STATES), lambda b, s: (s, 0, 0)),
                  pl.BlockSpec((None, SLAB, SLAB_STATES), lambda b, s: (s, 0, 0)),
                  pl.BlockSpec((None, SLAB_STATES, SLAB), lambda b, s: (s, 0, 0)),
                  pl.BlockSpec((None, SLAB_STATES, SLAB), lambda b, s: (s, 0, 0)),
                  pl.BlockSpec((1, SLAB), lambda b, s: (0, s))],
        out_specs=[pl.BlockSpec((rows, SLAB), lambda b, s: (b, s)), st_spec, st_spec],
        out_shape=[jax.ShapeDtypeStruct((n_seq * rows, D_SSM), F32), st_shape, st_shape],
        scratch_shapes=[pltpu.VMEM((SLAB_STATES // LANES, rows, LANES), F32)] * 2 + scratch,
        compiler_params=_params(48, 2), name="s5",
    )(u, h0r, h0i, lam_re, lam_im, log_dt, bd_re, bd_im, cd_re, cd_im, d_row)


def _mixout_kernel(x_ref, att_ref, y_ref, wg_ref, bg_ref, wa_ref, ws_ref, o_ref):
    y = y_ref[...]
    z = 0.5 * y * (1.0 + jnp.tanh(math.sqrt(2.0 / math.pi) * (y + 0.044715 * (y * y * y))))
    gate = _sigmoid(jnp.dot(z.astype(BF16), wg_ref[...], preferred_element_type=F32) + bg_ref[...])
    ssm = (z * gate).astype(BF16)
    o_ref[...] = (x_ref[...] + jnp.dot(att_ref[...], wa_ref[...], preferred_element_type=F32)
                  + jnp.dot(ssm, ws_ref[...], preferred_element_type=F32))


def _mixout(x, att, y, wg, bg, wa, ws):
    R = x.shape[0]
    tm = min(R, 512)
    row = lambda n: pl.BlockSpec((tm, n), lambda i: (i, 0))
    full = lambda a: pl.BlockSpec(a.shape, lambda i: (0,) * a.ndim, pipeline_mode=pl.Buffered(1))
    return pl.pallas_call(
        _mixout_kernel, grid=(R // tm,),
        in_specs=[row(D_MODEL), row(D_ATTN), row(D_SSM), full(wg), full(bg), full(wa), full(ws)],
        out_specs=row(D_MODEL), out_shape=jax.ShapeDtypeStruct((R, D_MODEL), F32),
        compiler_params=_params(56, 1), name="mixout",
    )(x, att, y, wg, bg, wa, ws)


def _ffn_kernel(x_ref, nw_ref, w1_ref, w3_ref, w2_ref, o_ref, h_ref):
    @pl.when(pl.program_id(1) == 0)
    def _():
        x = x_ref[...]
        h_ref[...] = _rms(x, nw_ref[...]).astype(BF16)
        o_ref[...] = x

    h = h_ref[...]
    a = jnp.dot(h, w1_ref[...], preferred_element_type=F32)
    b = jnp.dot(h, w3_ref[...], preferred_element_type=F32)
    g = (a * _sigmoid(a) * b).astype(BF16)
    o_ref[...] += jnp.dot(g, w2_ref[...], preferred_element_type=F32)


def _ffn(x, nw, w1, w3, w2):
    R = x.shape[0]
    d_ff = w1.shape[1]
    tm = min(R, 512)
    tf = 512
    assert d_ff % tf == 0 and R % tm == 0
    return pl.pallas_call(
        _ffn_kernel, grid=(R // tm, d_ff // tf),
        in_specs=[pl.BlockSpec((tm, D_MODEL), lambda i, f: (i, 0)),
                  pl.BlockSpec((1, D_MODEL), lambda i, f: (0, 0)),
                  pl.BlockSpec((D_MODEL, tf), lambda i, f: (0, f)),
                  pl.BlockSpec((D_MODEL, tf), lambda i, f: (0, f)),
                  pl.BlockSpec((tf, D_MODEL), lambda i, f: (f, 0))],
        out_specs=pl.BlockSpec((tm, D_MODEL), lambda i, f: (i, 0)),
        out_shape=jax.ShapeDtypeStruct((R, D_MODEL), F32),
        scratch_shapes=[pltpu.VMEM((tm, D_MODEL), BF16)],
        compiler_params=_params(48, 2), name="ffn",
    )(x, nw, w1, w3, w2)


HIST = 16


def _pool_prompt_kernel(x_ref, nw_ref, pw_ref, ps_ref, o_ref, st_ref, ext_ref, *, tm, tiles_per_seq):
    i = pl.program_id(0)
    t_in_seq = i % tiles_per_seq

    @pl.when(t_in_seq == 0)
    def _():
        ext_ref[0:HIST, :] = jnp.zeros((HIST, D_MODEL), F32)

    x = x_ref[...]
    hp = _rms(x, nw_ref[...])
    ext_ref[HIST:HIST + tm, :] = hp
    pos = t_in_seq * tm + lax.broadcasted_iota(jnp.int32, (tm, 1), 0)
    for g, w in enumerate(POOL_WINDOWS):
        cols = slice(g * POOL_GROUP, (g + 1) * POOL_GROUP)
        acc = hp[:, cols]
        for k in range(1, w):
            acc = acc + ext_ref[HIST - k:HIST - k + tm, cols]
        inv = 1.0 / jnp.minimum(pos + 1, w).astype(F32)
        pooled = (acc * inv - hp[:, cols]).astype(BF16)
        o_ref[:, cols] = x[:, cols] + jnp.dot(pooled, pw_ref[g], preferred_element_type=F32) * ps_ref[:, cols]
    st_ref[...] = ext_ref[tm + HIST - POOL_BUF:tm + HIST, :]
    ext_ref[0:HIST, :] = ext_ref[tm:tm + HIST, :]


def _pool_prompt(x, nw, pw, ps, B, L):
    tm = min(L, 512)
    assert L % tm == 0 and tm >= HIST
    tiles_per_seq = L // tm
    return pl.pallas_call(
        functools.partial(_pool_prompt_kernel, tm=tm, tiles_per_seq=tiles_per_seq),
        grid=(B * tiles_per_seq,),
        in_specs=[pl.BlockSpec((tm, D_MODEL), lambda i: (i, 0)),
                  pl.BlockSpec((1, D_MODEL), lambda i: (0, 0)),
                  pl.BlockSpec(pw.shape, lambda i: (0, 0, 0)),
                  pl.BlockSpec((1, D_MODEL), lambda i: (0, 0))],
        out_specs=[pl.BlockSpec((tm, D_MODEL), lambda i: (i, 0)),
                   pl.BlockSpec((None, POOL_BUF, D_MODEL), lambda i: (i // tiles_per_seq, 0, 0))],
        out_shape=[jax.ShapeDtypeStruct((B * L, D_MODEL), F32), jax.ShapeDtypeStruct((B, POOL_BUF, D_MODEL), F32)],
        scratch_shapes=[pltpu.VMEM((HIST + tm, D_MODEL), F32)],
        compiler_params=_params(40, 1), name="pool_prompt",
    )(x, nw, pw, ps)


def _pool_sample(x, nw, sp, pw, ps, Bd, Q, pos0):
    G = len(POOL_WINDOWS)
    hist = [pl.BlockSpec((Bd, POOL_GROUP), functools.partial(lambda g, j: (0, j * G + g), j=j)) for j in range(POOL_BUF)]
    col = pl.BlockSpec((Q * Bd, POOL_GROUP), lambda g: (0, g))
    return pl.pallas_call(
        functools.partial(_pool_sample_kernel, Bd=Bd, Q=Q, pos0=pos0),
        grid=(G,),
        in_specs=[pl.BlockSpec((Q * Bd, D_MODEL), lambda g: (0, 0)),
                  pl.BlockSpec((1, D_MODEL), lambda g: (0, 0)),
                  pl.BlockSpec((None, POOL_GROUP, POOL_GROUP), lambda g: (g, 0, 0)),
                  pl.BlockSpec((1, POOL_GROUP), lambda g: (0, g))] + hist,
        out_specs=[col, col],
        out_shape=[jax.ShapeDtypeStruct((Q * Bd, D_MODEL), F32)] * 2,
        scratch_shapes=[pltpu.VMEM((Q * Bd, 1), F32)],
        compiler_params=_params(40, 1), name="pool_sample",
    )(x, nw, pw, ps, *([sp] * POOL_BUF))


def _pool_sample_kernel(x_ref, nw_ref, pw_ref, ps_ref, *rest, Bd, Q, pos0):
    hist_refs = rest[:POOL_BUF]
    o_ref, hs_ref, inv_ref = rest[POOL_BUF:]
    g = pl.program_id(0)

    @pl.when(g == 0)
    def _():
        x = x_ref[...]
        inv_ref[...] = lax.rsqrt(jnp.mean(x * x, axis=-1, keepdims=True) + RMS_EPS)

    for wi, w in enumerate(POOL_WINDOWS):
        @pl.when(g == wi)
        def _(wi=wi, w=w):
            cols = slice(wi * POOL_GROUP, (wi + 1) * POOL_GROUP)
            xs, hs = [], []
            for q in range(Q):
                rows = slice(q * Bd, (q + 1) * Bd)
                xq = x_ref[rows, cols]
                xs.append(xq)
                hs.append(xq * inv_ref[rows, :] * nw_ref[:, cols])
                hs_ref[rows, :] = hs[q]
            ext = lambda j: hist_refs[j][...] if j < POOL_BUF else hs[j - POOL_BUF]
            for q in range(Q):
                acc = hs[q]
                for k in range(1, w):
                    acc = acc + ext(POOL_BUF + q - k)
                cnt = float(min(pos0 + q + 1, w))
                pooled = (acc / cnt - hs[q]).astype(BF16)
                val = jnp.dot(pooled, pw_ref[...], preferred_element_type=F32) * ps_ref[...]
                o_ref[q * Bd:(q + 1) * Bd, :] = xs[q] + val


def _block_diag(w):
    G, a, b = w.shape
    gps = SLAB // SSM_GROUP
    w4 = w.reshape(N_SLABS, gps, a, b)
    eye = jnp.eye(gps, dtype=w.dtype)
    return jnp.einsum('sgab,gh->sgahb', w4, eye).reshape(N_SLABS, gps * a, gps * b)


def kernel(x_prompt, x_sample, cache_k, cache_v, cache_logf, state_s5_re, state_s5_im, state_pool, page_table, norm_mix_w, norm_ffn_w, w_in, b_f, q_norm_w, k_norm_w, s5_lam_re, s5_lam_im, s5_log_dt, s5_b_re, s5_b_im, s5_c_re, s5_c_im, s5_d, w_glu, b_glu, w_out, pool_w, pool_scale, ffn_w1, ffn_w3, ffn_w2):
    B, L, D = x_prompt.shape
    Bd, Q, _ = x_sample.shape
    depth = norm_mix_w.shape[0]
    n_attn, n_phys, page = cache_k.shape[:3]
    n_pages = page_table.shape[1]
    past_len = n_pages * page
    T = 16
    assert D == D_MODEL and L % T == 0

    xp = x_prompt.reshape(B * L, D)
    xs = jnp.swapaxes(x_sample, 0, 1).reshape(Q * Bd, D)
    from_tm = lambda a, *tail: jnp.swapaxes(a.reshape(Q, Bd, *tail), 0, 1)
    pt_flat = page_table.reshape(-1)
    ck_flat = cache_k.reshape(n_attn * n_phys, page, D_ATTN)
    cv_flat = cache_v.reshape(n_attn * n_phys, page, D_ATTN)
    clf_t = jnp.swapaxes(cache_logf, -1, -2).reshape(n_attn * n_phys, N_HEADS, page)

    kp, vp, fp, srp, sip, plp = [], [], [], [], [], []
    ksm, vsm, fsm, srs, sis, pls = [], [], [], [], [], []
    for layer in range(depth):
        i = layer // 2
        nw = norm_mix_w[layer].reshape(1, D)
        if layer % 2 == 0:
            w = w_in[i]
            wq = w[:, :D_ATTN].astype(BF16)
            wk = w[:, D_ATTN:2 * D_ATTN].astype(BF16)
            wv = w[:, 2 * D_ATTN:3 * D_ATTN].astype(BF16)
            wf = jnp.pad(w[:, 3 * D_ATTN:3 * D_ATTN + N_HEADS], ((0, 0), (0, LANES - N_HEADS))).astype(BF16)
            wu = w[:, 3 * D_ATTN + N_HEADS:].astype(BF16)
            bf = jnp.pad(b_f[i], (0, LANES - N_HEADS)).reshape(1, LANES)
            qg = q_norm_w[i].reshape(1, HEAD_DIM)
            kg = k_norm_w[i].reshape(1, HEAD_DIM)
            wg = w_glu[i].astype(BF16)
            bg = b_glu[i].reshape(1, D_SSM)
            wa = w_out[i][:D_ATTN].astype(BF16)
            ws = w_out[i][D_ATTN:].astype(BF16)
            lam_re = s5_lam_re[i].reshape(1, -1)
            lam_im = s5_lam_im[i].reshape(1, -1)
            log_dt = jnp.broadcast_to(s5_log_dt[i][:, None], (N_SSM_GROUPS, SSM_STATE)).reshape(1, -1)
            bd_re = _block_diag(jnp.swapaxes(s5_b_re[i], 1, 2))
            bd_im = _block_diag(jnp.swapaxes(s5_b_im[i], 1, 2))
            cd_re = _block_diag(jnp.swapaxes(s5_c_re[i], 1, 2)).astype(BF16)
            cd_im = _block_diag(jnp.swapaxes(s5_c_im[i], 1, 2)).astype(BF16)
            d_row = s5_d[i].reshape(1, D_SSM)

            qb, k, kb, v, vb, u, lf, c = _inproj(xp, nw, wq, wk, wv, wu, wf, bf, qg, kg, seq_len=L)
            c_t = jnp.swapaxes(c[:, :N_HEADS].reshape(B, L, N_HEADS), 1, 2)
            att = _attn_prompt(qb, kb, vb, c_t[..., None], c_t[:, :, None, :], B, L)
            zeros = jnp.zeros((B, 1, N_SLABS * SLAB_STATES), F32)
            y, h_re, h_im = _s5(u, zeros, zeros, lam_re, lam_im, log_dt, bd_re, bd_im, cd_re, cd_im, d_row,
                                n_seq=B, T=T, NC=L // T, chained=True)
            xp = _mixout(xp, att, y, wg, bg, wa, ws)
            kp.append(k.reshape(B, L, N_HEADS, HEAD_DIM))
            vp.append(v.reshape(B, L, N_HEADS, HEAD_DIM))
            fp.append(lf[:, :N_HEADS].reshape(B, L, N_HEADS))
            srp.append(h_re.reshape(B, N_SSM_GROUPS, SSM_STATE))
            sip.append(h_im.reshape(B, N_SSM_GROUPS, SSM_STATE))

            qb, k, kb, v, vb, u, lf = _inproj(xs, nw, wq, wk, wv, wu, wf, bf, qg, kg, seq_len=None)
            lfn = lf[:, :N_HEADS].reshape(Q, Bd, N_HEADS)
            att = _attn_sample(pt_flat, qb.reshape(Q, Bd, D_ATTN), k.reshape(Q, Bd, D_ATTN), v.reshape(Q, Bd, D_ATTN),
                               jnp.transpose(lfn, (1, 2, 0)), ck_flat, cv_flat, clf_t, i, n_phys, n_pages, Bd, Q)
            y, h_re, h_im = _s5(u, state_s5_re[i].reshape(Bd, -1), state_s5_im[i].reshape(Bd, -1),
                                lam_re, lam_im, log_dt, bd_re, bd_im, cd_re, cd_im, d_row,
                                n_seq=1, T=Q, NC=Bd, chained=False)
            xs = _mixout(xs, att.reshape(Q * Bd, D_ATTN), y, wg, bg, wa, ws)
            ksm.append(from_tm(k, N_HEADS, HEAD_DIM))
            vsm.append(from_tm(v, N_HEADS, HEAD_DIM))
            fsm.append(jnp.swapaxes(lfn, 0, 1))
            srs.append(h_re.reshape(Bd, N_SSM_GROUPS, SSM_STATE))
            sis.append(h_im.reshape(Bd, N_SSM_GROUPS, SSM_STATE))
        else:
            pw = pool_w[i].astype(BF16)
            ps = pool_scale[i].reshape(1, D)
            xp, st = _pool_prompt(xp, nw, pw, ps, B, L)
            plp.append(st)
            xs, hs = _pool_sample(xs, nw, state_pool[i].reshape(Bd, POOL_BUF * D), pw, ps, Bd, Q, past_len)
            pls.append(jnp.concatenate([state_pool[i], from_tm(hs, D)], axis=1)[:, -POOL_BUF:])
        nf = norm_ffn_w[layer].reshape(1, D)
        w1, w3, w2 = ffn_w1[layer].astype(BF16), ffn_w3[layer].astype(BF16), ffn_w2[layer].astype(BF16)
        xp = _ffn(xp, nf, w1, w3, w2)
        xs = _ffn(xs, nf, w1, w3, w2)
    return (xp.reshape(B, L, D), from_tm(xs, D),
            jnp.stack(kp), jnp.stack(vp), jnp.stack(fp), jnp.stack(srp), jnp.stack(sip), jnp.stack(plp),
            jnp.stack(ksm), jnp.stack(vsm), jnp.stack(fsm), jnp.stack(srs), jnp.stack(sis), jnp.stack(pls))
```

```python
import functools
import math

import jax
import jax.numpy as jnp
from jax import lax
from jax.experimental import pallas as pl
from jax.experimental.pallas import tpu as pltpu

F32 = jnp.float32
BF16 = jnp.bfloat16

D_MODEL = 2048
D_ATTN = D_MODEL // 2
HEAD_DIM = 128
N_HEADS = D_ATTN // HEAD_DIM
D_SSM = D_MODEL - D_ATTN
SSM_GROUP = 16
N_SSM_GROUPS = D_SSM // SSM_GROUP
SSM_STATE = 64
POOL_WINDOWS = (2, 4, 8, 16)
POOL_GROUP = D_MODEL // len(POOL_WINDOWS)
POOL_BUF = max(POOL_WINDOWS) - 1
RMS_EPS = 1e-6

LANES = 128
MXU_DIM = 256
SLAB = MXU_DIM
SLAB_STATES = (SLAB // SSM_GROUP) * SSM_STATE
N_SLABS = D_SSM // SLAB
VMEM_BYTES = 64 * 1024 * 1024


def _params(vmem_mb, n_axes):
    return pltpu.CompilerParams(dimension_semantics=("arbitrary",) * n_axes,
                                vmem_limit_bytes=min(vmem_mb * 1024 * 1024, VMEM_BYTES - 4 * 1024 * 1024))


def _rms(x, w):
    return x * lax.rsqrt(jnp.mean(x * x, axis=-1, keepdims=True) + RMS_EPS) * w


def _log_sigmoid(x):
    return jnp.minimum(x, 0.0) - jnp.log1p(jnp.exp(-jnp.abs(x)))


def _sigmoid(x):
    return 1.0 / (1.0 + jnp.exp(-x))


def _split3(x):
    hi = x.astype(BF16)
    r = x - hi.astype(F32)
    mid = r.astype(BF16)
    lo = (r - mid.astype(F32)).astype(BF16)
    return hi, mid, lo


def _cumsum_rows(x):
    n = x.shape[0]
    tri = (lax.broadcasted_iota(jnp.int32, (n, n), 1) <= lax.broadcasted_iota(jnp.int32, (n, n), 0)).astype(BF16)
    hi, mid, lo = _split3(x)
    z = jnp.dot(tri, jnp.concatenate([hi, mid, lo], axis=1), preferred_element_type=F32)
    return z[:, :LANES] + z[:, LANES:2 * LANES] + z[:, 2 * LANES:]


def _inproj_kernel(x_ref, nw_ref, wq_ref, wk_ref, wv_ref, wu_ref, wf_ref, bf_ref, qg_ref, kg_ref,
                   q_ref, k_ref, kb_ref, v_ref, vb_ref, u_ref, lf_ref, *rest, tiles_per_seq):
    h = _rms(x_ref[...], nw_ref[...]).astype(BF16)
    scale = HEAD_DIM ** -0.5
    for j in range(D_ATTN // MXU_DIM):
        cols = slice(j * MXU_DIM, (j + 1) * MXU_DIM)
        zq = jnp.dot(h, wq_ref[:, cols], preferred_element_type=F32)
        zk = jnp.dot(h, wk_ref[:, cols], preferred_element_type=F32)
        zv = jnp.dot(h, wv_ref[:, cols], preferred_element_type=F32)
        for t in range(MXU_DIM // HEAD_DIM):
            sub = slice(t * HEAD_DIM, (t + 1) * HEAD_DIM)
            dst = slice(j * MXU_DIM + t * HEAD_DIM, j * MXU_DIM + (t + 1) * HEAD_DIM)
            q_ref[:, dst] = (_rms(zq[:, sub], qg_ref[...]) * scale).astype(BF16)
            kn = _rms(zk[:, sub], kg_ref[...])
            k_ref[:, dst] = kn
            kb_ref[:, dst] = kn.astype(BF16)
        v_ref[:, cols] = zv
        vb_ref[:, cols] = zv.astype(BF16)
        u_ref[:, cols] = jnp.dot(h, wu_ref[:, cols], preferred_element_type=F32)
    lf = _log_sigmoid(jnp.dot(h, wf_ref[...], preferred_element_type=F32) + bf_ref[...])
    lf_ref[...] = lf
    if tiles_per_seq is not None:
        c_ref, carry_ref = rest

        @pl.when(pl.program_id(0) % tiles_per_seq == 0)
        def _():
            carry_ref[...] = jnp.zeros_like(carry_ref)

        c = _cumsum_rows(lf) + carry_ref[...]
        c_ref[...] = c
        carry_ref[...] = c[c.shape[0] - 1:, :]


def _inproj(x, nw, wq, wk, wv, wu, wf, bf, qg, kg, seq_len):
    R = x.shape[0]
    tm = min(R, 512)
    row = lambda n: pl.BlockSpec((tm, n), lambda i: (i, 0))
    full = lambda a: pl.BlockSpec(a.shape, lambda i: (0,) * a.ndim, pipeline_mode=pl.Buffered(1))
    out_shape = [jax.ShapeDtypeStruct((R, D_ATTN), BF16), jax.ShapeDtypeStruct((R, D_ATTN), F32),
                 jax.ShapeDtypeStruct((R, D_ATTN), BF16), jax.ShapeDtypeStruct((R, D_ATTN), F32),
                 jax.ShapeDtypeStruct((R, D_ATTN), BF16), jax.ShapeDtypeStruct((R, D_SSM), F32),
                 jax.ShapeDtypeStruct((R, LANES), F32)]
    out_specs = [row(D_ATTN)] * 5 + [row(D_SSM), row(LANES)]
    scratch = []
    tiles_per_seq = None
    if seq_len is not None:
        assert seq_len % tm == 0
        tiles_per_seq = seq_len // tm
        out_shape.append(jax.ShapeDtypeStruct((R, LANES), F32))
        out_specs.append(row(LANES))
        scratch.append(pltpu.VMEM((1, LANES), F32))
    return pl.pallas_call(
        functools.partial(_inproj_kernel, tiles_per_seq=tiles_per_seq),
        grid=(R // tm,),
        in_specs=[row(D_MODEL), full(nw), full(wq), full(wk), full(wv), full(wu), full(wf), full(bf), full(qg), full(kg)],
        out_specs=out_specs, out_shape=out_shape, scratch_shapes=scratch,
        compiler_params=_params(60, 1), name="inproj",
    )(x, nw, wq, wk, wv, wu, wf, bf, qg, kg)


def _attn_prompt_kernel(q_ref, k_ref, v_ref, cq_ref, ck_ref, o_ref, *, L, tq):
    nt = (((1,), (1,)), ((), ()))
    for qi in range(L // tq):
        rows = slice(qi * tq, (qi + 1) * tq)
        q = q_ref[rows, :]
        cq = cq_ref[rows, :]
        m = l = acc = None
        for ki in range(qi + 1):
            cols = slice(ki * tq, (ki + 1) * tq)
            s = lax.dot_general(q, k_ref[cols, :], nt, preferred_element_type=F32)
            s = s + (cq - ck_ref[:, cols])
            if ki == qi:
                keep = lax.broadcasted_iota(jnp.int32, (tq, tq), 1) <= lax.broadcasted_iota(jnp.int32, (tq, tq), 0)
                s = jnp.where(keep, s, -jnp.inf)
            mx = jnp.max(s, axis=-1, keepdims=True)
            if ki == 0:
                m = mx
                p = jnp.exp(s - m)
                l = jnp.sum(p, axis=-1, keepdims=True)
                acc = jnp.dot(p.astype(BF16), v_ref[cols, :], preferred_element_type=F32)
            else:
                m_new = jnp.maximum(m, mx)
                alpha = jnp.exp(m - m_new)
                p = jnp.exp(s - m_new)
                l = alpha * l + jnp.sum(p, axis=-1, keepdims=True)
                acc = alpha * acc + jnp.dot(p.astype(BF16), v_ref[cols, :], preferred_element_type=F32)
                m = m_new
        o_ref[rows, :] = (acc / l).astype(BF16)


def _attn_prompt(qb, kb, vb, cq, ck, B, L):
    tq = min(L, 512)
    blk = pl.BlockSpec((L, HEAD_DIM), lambda b, h: (b, h))
    return pl.pallas_call(
        functools.partial(_attn_prompt_kernel, L=L, tq=tq),
        grid=(B, N_HEADS),
        in_specs=[blk, blk, blk,
                  pl.BlockSpec((None, None, L, 1), lambda b, h: (b, h, 0, 0)),
                  pl.BlockSpec((None, None, 1, L), lambda b, h: (b, h, 0, 0))],
        out_specs=blk, out_shape=jax.ShapeDtypeStruct((B * L, D_ATTN), BF16),
        compiler_params=_params(32, 2), name="attn_prompt",
    )(qb, kb, vb, cq, ck)


def _attn_sample_kernel(pt_ref, q_ref, kn_ref, vn_ref, lfn_ref, *rest, n_pages, Q):
    k_refs = rest[:n_pages]
    v_refs = rest[n_pages:2 * n_pages]
    lf_refs = rest[2 * n_pages:3 * n_pages]
    o_ref, s_ref, d_ref = rest[3 * n_pages:]
    nt = (((1,), (1,)), ((), ()))
    H = N_HEADS
    M = Q * H
    page = k_refs[0].shape[0]
    PL = page * H
    RPP = PL // LANES

    qm = q_ref[...].reshape(M, HEAD_DIM)
    qmb = qm.astype(BF16)
    same_head = (lax.broadcasted_iota(jnp.int32, (M, LANES), 1) % H
                 == lax.broadcasted_iota(jnp.int32, (M, LANES), 0) % H)

    x = jnp.concatenate([r[...] for r in lf_refs], axis=0)
    lane = lax.broadcasted_iota(jnp.int32, x.shape, 1)
    sh = H
    while sh < LANES:
        x = x + jnp.where(lane >= sh, pltpu.roll(x, sh, axis=1), 0.0)
        sh *= 2
    tot = jnp.where(lane >= LANES - H, x, 0.0)
    inc = _cumsum_rows(tot)
    after = inc[inc.shape[0] - 1:, :] - inc
    sh = H
    while sh < LANES:
        after = after + pltpu.roll(after, sh, axis=1)
        sh *= 2
    row_tot = tot
    sh = H
    while sh < LANES:
        row_tot = row_tot + pltpu.roll(row_tot, sh, axis=1)
        sh *= 2
    d_ref[...] = after + (row_tot - x)

    lfn = lfn_ref[...]
    cum = [lfn[0:H, :]]
    for i in range(1, Q):
        cum.append(cum[-1] + lfn[i * H:(i + 1) * H, :])
    cum_q = jnp.concatenate(cum, axis=0)
    cum_k = jnp.concatenate([jnp.concatenate([c] * Q, axis=0) for c in cum], axis=1)

    qm_r = qmb.astype(F32)
    tile_q = lambda a: jnp.concatenate([a.astype(BF16).astype(F32)] * Q, axis=0)
    s_new = jnp.concatenate(
        [jnp.sum(qm_r * tile_q(kn_ref[i]), axis=-1, keepdims=True) for i in range(Q)], axis=1)
    s_new = s_new + (cum_q - cum_k)
    q_of_row = lax.broadcasted_iota(jnp.int32, (M, Q), 0) // H
    s_new = jnp.where(lax.broadcasted_iota(jnp.int32, (M, Q), 1) <= q_of_row, s_new, -jnp.inf)
    m = jnp.max(s_new, axis=-1, keepdims=True)

    for j in range(n_pages):
        kb = k_refs[j][...].reshape(PL, HEAD_DIM).astype(BF16)
        s = lax.dot_general(qmb, kb, nt, preferred_element_type=F32)
        for r in range(RPP):
            blk = s[:, r * LANES:(r + 1) * LANES] + (cum_q + d_ref[j * RPP + r:j * RPP + r + 1, :])
            blk = jnp.where(same_head, blk, -jnp.inf)
            s_ref[:, (j * RPP + r) * LANES:(j * RPP + r + 1) * LANES] = blk
            m = jnp.maximum(m, jnp.max(blk, axis=-1, keepdims=True))

    p_new = jnp.exp(s_new - m)
    l = jnp.sum(p_new, axis=-1, keepdims=True)
    p_new_r = p_new.astype(BF16).astype(F32)
    acc = p_new_r[:, 0:1] * tile_q(vn_ref[0])
    for i in range(1, Q):
        acc = acc + p_new_r[:, i:i + 1] * tile_q(vn_ref[i])
    for j in range(n_pages):
        p = jnp.exp(s_ref[:, j * PL:(j + 1) * PL] - m)
        l = l + jnp.sum(p, axis=-1, keepdims=True)
        vb = v_refs[j][...].reshape(PL, HEAD_DIM).astype(BF16)
        acc = acc + jnp.dot(p.astype(BF16), vb, preferred_element_type=F32)
    o_ref[...] = (acc / l).reshape(Q, H, HEAD_DIM)


def _attn_sample(page_table_flat, q4, k4, v4, lfn_col, cache_k, cache_v, cache_lf, layer, n_phys, n_pages, Bd, Q):
    page = cache_k.shape[1]
    base = layer * n_phys
    rpp = page * N_HEADS // LANES

    def paged(tail):
        return [pl.BlockSpec((None,) + tail, functools.partial(
            lambda b, pt, j: (pt[b * n_pages + j] + base,) + (0,) * len(tail), j=j)) for j in range(n_pages)]

    tok = pl.BlockSpec((Q, None, N_HEADS, HEAD_DIM), lambda b, pt: (0, b, 0, 0))
    grid_spec = pltpu.PrefetchScalarGridSpec(
        num_scalar_prefetch=1, grid=(Bd,),
        in_specs=[tok, tok, tok, pl.BlockSpec((None, Q * N_HEADS, 1), lambda b, pt: (b, 0, 0))]
        + paged((page, N_HEADS, HEAD_DIM)) + paged((page, N_HEADS, HEAD_DIM)) + paged((rpp, LANES)),
        out_specs=tok,
        scratch_shapes=[pltpu.VMEM((Q * N_HEADS, n_pages * page * N_HEADS), F32),
                        pltpu.VMEM((n_pages * rpp, LANES), F32)])
    return pl.pallas_call(
        functools.partial(_attn_sample_kernel, n_pages=n_pages, Q=Q),
        grid_spec=grid_spec, out_shape=jax.ShapeDtypeStruct((Q, Bd, N_HEADS, HEAD_DIM), F32),
        compiler_params=_params(56, 1), name="attn_sample",
    )(page_table_flat, q4, k4, v4, lfn_col, *([cache_k] * n_pages), *([cache_v] * n_pages), *([cache_lf] * n_pages))


def _s5_kernel(u_ref, h0r_ref, h0i_ref, lr_ref, li_ref, ldt_ref, bdr_ref, bdi_ref, cdr_ref, cdi_ref, d_ref,
               y_ref, fr_ref, fi_ref, xr_ref, xi_ref, *scratch, T, NC, chained):
    lr, li = lr_ref[...], li_ref[...]
    dt = jnp.exp(ldt_ref[...])
    mag = jnp.exp(lr * dt)
    ar, ai = mag * jnp.cos(li * dt), mag * jnp.sin(li * dt)
    den = lr * lr + li * li
    zr = ((ar - 1.0) * lr + ai * li) / den
    zi = (ai * lr - (ar - 1.0) * li) / den
    bdr, bdi = bdr_ref[...], bdi_ref[...]
    pr = (zr * bdr - zi * bdi).astype(BF16)
    pi = (zr * bdi + zi * bdr).astype(BF16)
    n_cb = SLAB_STATES // LANES
    per = MXU_DIM // LANES
    if chained:
        stage_ref, perm_ref = scratch[4:]
        for t in range(per):
            stage_ref[t] = u_ref[:, t * LANES:(t + 1) * LANES]
        for s in range(T):
            for t in range(per):
                perm_ref[s * NC:(s + 1) * NC, t * LANES:(t + 1) * LANES] = stage_ref[t, pl.ds(s, NC, stride=T), :]
        u = perm_ref[...]
    else:
        u = u_ref[...]
    ub = u.astype(BF16)
    for c2 in range(n_cb // per):
        cols = slice(c2 * MXU_DIM, (c2 + 1) * MXU_DIM)
        x_r = jnp.dot(ub, pr[:, cols], preferred_element_type=F32)
        x_i = jnp.dot(ub, pi[:, cols], preferred_element_type=F32)
        for t in range(per):
            xr_ref[c2 * per + t] = x_r[:, t * LANES:(t + 1) * LANES]
            xi_ref[c2 * per + t] = x_i[:, t * LANES:(t + 1) * LANES]

    def sweep(cb, hr, hi, store):
        cols = slice(cb * LANES, (cb + 1) * LANES)
        a_r, a_i = ar[:, cols], ai[:, cols]
        for s in range(T):
            rows = slice(s * NC, (s + 1) * NC)
            xr, xi = xr_ref[cb, rows, :], xi_ref[cb, rows, :]
            hr, hi = a_r * hr - a_i * hi + xr, a_r * hi + a_i * hr + xi
            if store:
                xr_ref[cb, rows, :] = hr
                xi_ref[cb, rows, :] = hi
        return hr, hi

    if chained:
        er_ref, ei_ref, gr_ref, gi_ref = scratch[:4]
        zero = jnp.zeros((NC, LANES), F32)
        for cb in range(n_cb):
            cols = slice(cb * LANES, (cb + 1) * LANES)
            er_ref[:, cols], ei_ref[:, cols] = sweep(cb, zero, zero, False)
        tr, ti = ar, ai
        for _ in range(T - 1):
            tr, ti = tr * ar - ti * ai, tr * ai + ti * ar

        def chunk_step(k, carry):
            sr, si = carry
            gr_ref[pl.ds(k, 1), :] = sr
            gi_ref[pl.ds(k, 1), :] = si
            er, ei = er_ref[pl.ds(k, 1), :], ei_ref[pl.ds(k, 1), :]
            return tr * sr - ti * si + er, tr * si + ti * sr + ei

        sr, si = lax.fori_loop(0, NC, chunk_step, (h0r_ref[...], h0i_ref[...]))
        fr_ref[...] = sr
        fi_ref[...] = si
        for cb in range(n_cb):
            cols = slice(cb * LANES, (cb + 1) * LANES)
            sweep(cb, gr_ref[:, cols], gi_ref[:, cols], True)
    else:
        for cb in range(n_cb):
            cols = slice(cb * LANES, (cb + 1) * LANES)
            hr, hi = sweep(cb, h0r_ref[:, cols], h0i_ref[:, cols], True)
            fr_ref[:, cols] = hr
            fi_ref[:, cols] = hi

    y = d_ref[...] * u
    for c2 in range(n_cb // per):
        rows = slice(c2 * MXU_DIM, (c2 + 1) * MXU_DIM)
        h_r = jnp.concatenate([xr_ref[c2 * per + t] for t in range(per)], axis=1).astype(BF16)
        h_i = jnp.concatenate([xi_ref[c2 * per + t] for t in range(per)], axis=1).astype(BF16)
        y = y + (jnp.dot(h_r, cdr_ref[rows, :], preferred_element_type=F32)
                 - jnp.dot(h_i, cdi_ref[rows, :], preferred_element_type=F32))
    if chained:
        for s in range(T):
            for t in range(per):
                stage_ref[t, pl.ds(s, NC, stride=T), :] = y[s * NC:(s + 1) * NC, t * LANES:(t + 1) * LANES]
        for t in range(per):
            y_ref[:, t * LANES:(t + 1) * LANES] = stage_ref[t]
    else:
        y_ref[...] = y


def _s5(u, h0r, h0i, lam_re, lam_im, log_dt, bd_re, bd_im, cd_re, cd_im, d_row, n_seq, T, NC, chained):
    rows = NC * T
    slab_row = lambda: pl.BlockSpec((1, SLAB_STATES), lambda b, s: (0, s))
    if chained:
        st_spec = pl.BlockSpec((None, 1, SLAB_STATES), lambda b, s: (b, 0, s))
        st_shape = jax.ShapeDtypeStruct((n_seq, 1, N_SLABS * SLAB_STATES), F32)
        scratch = [pltpu.VMEM((NC, SLAB_STATES), F32)] * 4 + [
            pltpu.VMEM((SLAB // LANES, rows, LANES), F32), pltpu.VMEM((rows, SLAB), F32)]
    else:
        st_spec = pl.BlockSpec((NC, SLAB_STATES), lambda b, s: (0, s))
        st_shape = jax.ShapeDtypeStruct((NC, N_SLABS * SLAB_STATES), F32)
        scratch = []
    return pl.pallas_call(
        functools.partial(_s5_kernel, T=T, NC=NC, chained=chained),
        grid=(n_seq, N_SLABS),
        in_specs=[pl.BlockSpec((rows, SLAB), lambda b, s: (b, s)), st_spec, st_spec,
                  slab_row(), slab_row(), slab_row(),
                  pl.BlockSpec((None, SLAB, SLAB_STATES), lambda b, s: (s, 0, 0)),
                  pl.BlockSpec((None, SLAB, SLAB_STATES), lambda b, s: (s, 0, 0)),
                  pl.BlockSpec((None, SLAB_STATES, SLAB), lambda b, s: (s, 0, 0)),
                  pl.BlockSpec((None, SLAB_STATES, SLAB), lambda b, s: (s, 0, 0)),
                  pl.BlockSpec((1, SLAB), lambda b, s: (0, s))],
        out_specs=[pl.BlockSpec((rows, SLAB), lambda b, s: (b, s)), st_spec, st_spec],
        out_shape=[jax.ShapeDtypeStruct((n_seq * rows, D_SSM), F32), st_shape, st_shape],
        scratch_shapes=[pltpu.VMEM((SLAB_STATES // LANES, rows, LANES), F32)] * 2 + scratch,
        compiler_params=_params(48, 2), name="s5",
    )(u, h0r, h0i, lam_re, lam_im, log_dt, bd_re, bd_im, cd_re, cd_im, d_row)


def _mixout_kernel(x_ref, att_ref, y_ref, wg_ref, bg_ref, wa_ref, ws_ref, o_ref):
    y = y_ref[...]
    z = 0.5 * y * (1.0 + jnp.tanh(math.sqrt(2.0 / math.pi) * (y + 0.044715 * (y * y * y))))
    gate = _sigmoid(jnp.dot(z.astype(BF16), wg_ref[...], preferred_element_type=F32) + bg_ref[...])
    ssm = (z * gate).astype(BF16)
    o_ref[...] = (x_ref[...] + jnp.dot(att_ref[...], wa_ref[...], preferred_element_type=F32)
                  + jnp.dot(ssm, ws_ref[...], preferred_element_type=F32))


def _mixout(x, att, y, wg, bg, wa, ws):
    R = x.shape[0]
    tm = min(R, 512)
    row = lambda n: pl.BlockSpec((tm, n), lambda i: (i, 0))
    full = lambda a: pl.BlockSpec(a.shape, lambda i: (0,) * a.ndim, pipeline_mode=pl.Buffered(1))
    return pl.pallas_call(
        _mixout_kernel, grid=(R // tm,),
        in_specs=[row(D_MODEL), row(D_ATTN), row(D_SSM), full(wg), full(bg), full(wa), full(ws)],
        out_specs=row(D_MODEL), out_shape=jax.ShapeDtypeStruct((R, D_MODEL), F32),
        compiler_params=_params(56, 1), name="mixout",
    )(x, att, y, wg, bg, wa, ws)


def _ffn_kernel(x_ref, nw_ref, w1_ref, w3_ref, w2_ref, o_ref, h_ref):
    @pl.when(pl.program_id(1) == 0)
    def _():
        x = x_ref[...]
        h_ref[...] = _rms(x, nw_ref[...]).astype(BF16)
        o_ref[...] = x

    h = h_ref[...]
    a = jnp.dot(h, w1_ref[...], preferred_element_type=F32)
    b = jnp.dot(h, w3_ref[...], preferred_element_type=F32)
    g = (a * _sigmoid(a) * b).astype(BF16)
    o_ref[...] += jnp.dot(g, w2_ref[...], preferred_element_type=F32)


def _ffn(x, nw, w1, w3, w2):
    R = x.shape[0]
    d_ff = w1.shape[1]
    tm = min(R, 512)
    tf = 512
    assert d_ff % tf == 0 and R % tm == 0
    return pl.pallas_call(
        _ffn_kernel, grid=(R // tm, d_ff // tf),
        in_specs=[pl.BlockSpec((tm, D_MODEL), lambda i, f: (i, 0)),
                  pl.BlockSpec((1, D_MODEL), lambda i, f: (0, 0)),
                  pl.BlockSpec((D_MODEL, tf), lambda i, f: (0, f)),
                  pl.BlockSpec((D_MODEL, tf), lambda i, f: (0, f)),
                  pl.BlockSpec((tf, D_MODEL), lambda i, f: (f, 0))],
        out_specs=pl.BlockSpec((tm, D_MODEL), lambda i, f: (i, 0)),
        out_shape=jax.ShapeDtypeStruct((R, D_MODEL), F32),
        scratch_shapes=[pltpu.VMEM((tm, D_MODEL), BF16)],
        compiler_params=_params(48, 2), name="ffn",
    )(x, nw, w1, w3, w2)


HIST = 16


def _pool_prompt_kernel(x_ref, nw_ref, pw_ref, ps_ref, o_ref, st_ref, ext_ref, *, tm, tiles_per_seq):
    i = pl.program_id(0)
    t_in_seq = i % tiles_per_seq

    @pl.when(t_in_seq == 0)
    def _():
        ext_ref[0:HIST, :] = jnp.zeros((HIST, D_MODEL), F32)

    x = x_ref[...]
    hp = _rms(x, nw_ref[...])
    ext_ref[HIST:HIST + tm, :] = hp
    pos = t_in_seq * tm + lax.broadcasted_iota(jnp.int32, (tm, 1), 0)
    for g, w in enumerate(POOL_WINDOWS):
        cols = slice(g * POOL_GROUP, (g + 1) * POOL_GROUP)
        acc = hp[:, cols]
        for k in range(1, w):
            acc = acc + ext_ref[HIST - k:HIST - k + tm, cols]
        inv = 1.0 / jnp.minimum(pos + 1, w).astype(F32)
        pooled = (acc * inv - hp[:, cols]).astype(BF16)
        o_ref[:, cols] = x[:, cols] + jnp.dot(pooled, pw_ref[g], preferred_element_type=F32) * ps_ref[:, cols]
    st_ref[...] = ext_ref[tm + HIST - POOL_BUF:tm + HIST, :]
    ext_ref[0:HIST, :] = ext_ref[tm:tm + HIST, :]


def _pool_prompt(x, nw, pw, ps, B, L):
    tm = min(L, 512)
    assert L % tm == 0 and tm >= HIST
    tiles_per_seq = L // tm
    return pl.pallas_call(
        functools.partial(_pool_prompt_kernel, tm=tm, tiles_per_seq=tiles_per_seq),
        grid=(B * tiles_per_seq,),
        in_specs=[pl.BlockSpec((tm, D_MODEL), lambda i: (i, 0)),
                  pl.BlockSpec((1, D_MODEL), lambda i: (0, 0)),
                  pl.BlockSpec(pw.shape, lambda i: (0, 0, 0)),
                  pl.BlockSpec((1, D_MODEL), lambda i: (0, 0))],
        out_specs=[pl.BlockSpec((tm, D_MODEL), lambda i: (i, 0)),
                   pl.BlockSpec((None, POOL_BUF, D_MODEL), lambda i: (i // tiles_per_seq, 0, 0))],
        out_shape=[jax.ShapeDtypeStruct((B * L, D_MODEL), F32), jax.ShapeDtypeStruct((B, POOL_BUF, D_MODEL), F32)],
        scratch_shapes=[pltpu.VMEM((HIST + tm, D_MODEL), F32)],
        compiler_params=_params(40, 1), name="pool_prompt",
    )(x, nw, pw, ps)


def _pool_sample(x, nw, sp, pw, ps, Bd, Q, pos0):
    G = len(POOL_WINDOWS)
    hist = [pl.BlockSpec((Bd, POOL_GROUP), functools.partial(lambda g, j: (0, j * G + g), j=j)) for j in range(POOL_BUF)]
    col = pl.BlockSpec((Q * Bd, POOL_GROUP), lambda g: (0, g))
    return pl.pallas_call(
        functools.partial(_pool_sample_kernel, Bd=Bd, Q=Q, pos0=pos0),
        grid=(G,),
        in_specs=[pl.BlockSpec((Q * Bd, D_MODEL), lambda g: (0, 0)),
                  pl.BlockSpec((1, D_MODEL), lambda g: (0, 0)),
                  pl.BlockSpec((None, POOL_GROUP, POOL_GROUP), lambda g: (g, 0, 0)),
                  pl.BlockSpec((1, POOL_GROUP), lambda g: (0, g))] + hist,
        out_specs=[col, col],
        out_shape=[jax.ShapeDtypeStruct((Q * Bd, D_MODEL), F32)] * 2,
        scratch_shapes=[pltpu.VMEM((Q * Bd, 1), F32)],
        compiler_params=_params(40, 1), name="pool_sample",
    )(x, nw, pw, ps, *([sp] * POOL_BUF))


def _pool_sample_kernel(x_ref, nw_ref, pw_ref, ps_ref, *rest, Bd, Q, pos0):
    hist_refs = rest[:POOL_BUF]
    o_ref, hs_ref, inv_ref = rest[POOL_BUF:]
    g = pl.program_id(0)

    @pl.when(g == 0)
    def _():
        x = x_ref[...]
        inv_ref[...] = lax.rsqrt(jnp.mean(x * x, axis=-1, keepdims=True) + RMS_EPS)

    for wi, w in enumerate(POOL_WINDOWS):
        @pl.when(g == wi)
        def _(wi=wi, w=w):
            cols = slice(wi * POOL_GROUP, (wi + 1) * POOL_GROUP)
            xs, hs = [], []
            for q in range(Q):
                rows = slice(q * Bd, (q + 1) * Bd)
                xq = x_ref[rows, cols]
                xs.append(xq)
                hs.append(xq * inv_ref[rows, :] * nw_ref[:, cols])
                hs_ref[rows, :] = hs[q]
            ext = lambda j: hist_refs[j][...] if j < POOL_BUF else hs[j - POOL_BUF]
            for q in range(Q):
                acc = hs[q]
                for k in range(1, w):
                    acc = acc + ext(POOL_BUF + q - k)
                cnt = float(min(pos0 + q + 1, w))
                pooled = (acc / cnt - hs[q]).astype(BF16)
                val = jnp.dot(pooled, pw_ref[...], preferred_element_type=F32) * ps_ref[...]
                o_ref[q * Bd:(q + 1) * Bd, :] = xs[q] + val


def _block_diag(w):
    G, a, b = w.shape
    gps = SLAB // SSM_GROUP
    w4 = w.reshape(N_SLABS, gps, a, b)
    eye = jnp.eye(gps, dtype=w.dtype)
    return jnp.einsum('sgab,gh->sgahb', w4, eye).reshape(N_SLABS, gps * a, gps * b)


def kernel(x_prompt, x_sample, cache_k, cache_v, cache_logf, state_s5_re, state_s5_im, state_pool, page_table, norm_mix_w, norm_ffn_w, w_in, b_f, q_norm_w, k_norm_w, s5_lam_re, s5_lam_im, s5_log_dt, s5_b_re, s5_b_im, s5_c_re, s5_c_im, s5_d, w_glu, b_glu, w_out, pool_w, pool_scale, ffn_w1, ffn_w3, ffn_w2):
    B, L, D = x_prompt.shape
    Bd, Q, _ = x_sample.shape
    depth = norm_mix_w.shape[0]
    n_attn, n_phys, page = cache_k.shape[:3]
    n_pages = page_table.shape[1]
    past_len = n_pages * page
    T = 16
    assert D == D_MODEL and L % T == 0

    xp = x_prompt.reshape(B * L, D)
    xs = jnp.swapaxes(x_sample, 0, 1).reshape(Q * Bd, D)
    from_tm = lambda a, *tail: jnp.swapaxes(a.reshape(Q, Bd, *tail), 0, 1)
    pt_flat = page_table.reshape(-1)
    assert (page * N_HEADS) % LANES == 0
    ck_flat = cache_k.reshape(n_attn * n_phys, page, N_HEADS, HEAD_DIM)
    cv_flat = cache_v.reshape(n_attn * n_phys, page, N_HEADS, HEAD_DIM)
    clf_flat = cache_logf.reshape(n_attn * n_phys, page * N_HEADS // LANES, LANES)

    kp, vp, fp, srp, sip, plp = [], [], [], [], [], []
    ksm, vsm, fsm, srs, sis, pls = [], [], [], [], [], []
    for layer in range(depth):
        i = layer // 2
        nw = norm_mix_w[layer].reshape(1, D)
        if layer % 2 == 0:
            w = w_in[i]
            wq = w[:, :D_ATTN].astype(BF16)
            wk = w[:, D_ATTN:2 * D_ATTN].astype(BF16)
            wv = w[:, 2 * D_ATTN:3 * D_ATTN].astype(BF16)
            wf = jnp.pad(w[:, 3 * D_ATTN:3 * D_ATTN + N_HEADS], ((0, 0), (0, LANES - N_HEADS))).astype(BF16)
            wu = w[:, 3 * D_ATTN + N_HEADS:].astype(BF16)
            bf = jnp.pad(b_f[i], (0, LANES - N_HEADS)).reshape(1, LANES)
            qg = q_norm_w[i].reshape(1, HEAD_DIM)
            kg = k_norm_w[i].reshape(1, HEAD_DIM)
            wg = w_glu[i].astype(BF16)
            bg = b_glu[i].reshape(1, D_SSM)
            wa = w_out[i][:D_ATTN].astype(BF16)
            ws = w_out[i][D_ATTN:].astype(BF16)
            lam_re = s5_lam_re[i].reshape(1, -1)
            lam_im = s5_lam_im[i].reshape(1, -1)
            log_dt = jnp.broadcast_to(s5_log_dt[i][:, None], (N_SSM_GROUPS, SSM_STATE)).reshape(1, -1)
            bd_re = _block_diag(jnp.swapaxes(s5_b_re[i], 1, 2))
            bd_im = _block_diag(jnp.swapaxes(s5_b_im[i], 1, 2))
            cd_re = _block_diag(jnp.swapaxes(s5_c_re[i], 1, 2)).astype(BF16)
            cd_im = _block_diag(jnp.swapaxes(s5_c_im[i], 1, 2)).astype(BF16)
            d_row = s5_d[i].reshape(1, D_SSM)

            qb, k, kb, v, vb, u, lf, c = _inproj(xp, nw, wq, wk, wv, wu, wf, bf, qg, kg, seq_len=L)
            c_t = jnp.swapaxes(c[:, :N_HEADS].reshape(B, L, N_HEADS), 1, 2)
            att = _attn_prompt(qb, kb, vb, c_t[..., None], c_t[:, :, None, :], B, L)
            zeros = jnp.zeros((B, 1, N_SLABS * SLAB_STATES), F32)
            y, h_re, h_im = _s5(u, zeros, zeros, lam_re, lam_im, log_dt, bd_re, bd_im, cd_re, cd_im, d_row,
                                n_seq=B, T=T, NC=L // T, chained=True)
            xp = _mixout(xp, att, y, wg, bg, wa, ws)
            kp.append(k.reshape(B, L, N_HEADS, HEAD_DIM))
            vp.append(v.reshape(B, L, N_HEADS, HEAD_DIM))
            fp.append(lf[:, :N_HEADS].reshape(B, L, N_HEADS))
            srp.append(h_re.reshape(B, N_SSM_GROUPS, SSM_STATE))
            sip.append(h_im.reshape(B, N_SSM_GROUPS, SSM_STATE))

            qb, k, kb, v, vb, u, lf = _inproj(xs, nw, wq, wk, wv, wu, wf, bf, qg, kg, seq_len=None)
            lfn = lf[:, :N_HEADS].reshape(Q, Bd, N_HEADS)
            heads = lambda a: a.astype(F32).reshape(Q, Bd, N_HEADS, HEAD_DIM)
            att = _attn_sample(pt_flat, heads(qb), heads(k), heads(v),
                               jnp.swapaxes(lfn, 0, 1).reshape(Bd, Q * N_HEADS, 1),
                               ck_flat, cv_flat, clf_flat, i, n_phys, n_pages, Bd, Q).astype(BF16)
            y, h_re, h_im = _s5(u, state_s5_re[i].reshape(Bd, -1), state_s5_im[i].reshape(Bd, -1),
                                lam_re, lam_im, log_dt, bd_re, bd_im, cd_re, cd_im, d_row,
                                n_seq=1, T=Q, NC=Bd, chained=False)
            xs = _mixout(xs, att.reshape(Q * Bd, D_ATTN), y, wg, bg, wa, ws)
            ksm.append(from_tm(k, N_HEADS, HEAD_DIM))
            vsm.append(from_tm(v, N_HEADS, HEAD_DIM))
            fsm.append(jnp.swapaxes(lfn, 0, 1))
            srs.append(h_re.reshape(Bd, N_SSM_GROUPS, SSM_STATE))
            sis.append(h_im.reshape(Bd, N_SSM_GROUPS, SSM_STATE))
        else:
            pw = pool_w[i].astype(BF16)
            ps = pool_scale[i].reshape(1, D)
            xp, st = _pool_prompt(xp, nw, pw, ps, B, L)
            plp.append(st)
            xs, hs = _pool_sample(xs, nw, state_pool[i].reshape(Bd, POOL_BUF * D), pw, ps, Bd, Q, past_len)
            pls.append(jnp.concatenate([state_pool[i], from_tm(hs, D)], axis=1)[:, -POOL_BUF:])
        nf = norm_ffn_w[layer].reshape(1, D)
        w1, w3, w2 = ffn_w1[layer].astype(BF16), ffn_w3[layer].astype(BF16), ffn_w2[layer].astype(BF16)
        xp = _ffn(xp, nf, w1, w3, w2)
        xs = _ffn(xs, nf, w1, w3, w2)
    return (xp.reshape(B, L, D), from_tm(xs, D),
            jnp.stack(kp), jnp.stack(vp), jnp.stack(fp), jnp.stack(srp), jnp.stack(sip), jnp.stack(plp),
            jnp.stack(ksm), jnp.stack(vsm), jnp.stack(fsm), jnp.stack(srs), jnp.stack(sis), jnp.stack(pls))
```

```python
import functools
import math

import jax
import jax.numpy as jnp
from jax import lax
from jax.experimental import pallas as pl
from jax.experimental.pallas import tpu as pltpu

F32 = jnp.float32
BF16 = jnp.bfloat16

D_MODEL = 2048
D_ATTN = D_MODEL // 2
HEAD_DIM = 128
N_HEADS = D_ATTN // HEAD_DIM
D_SSM = D_MODEL - D_ATTN
SSM_GROUP = 16
N_SSM_GROUPS = D_SSM // SSM_GROUP
SSM_STATE = 64
POOL_WINDOWS = (2, 4, 8, 16)
POOL_GROUP = D_MODEL // len(POOL_WINDOWS)
POOL_BUF = max(POOL_WINDOWS) - 1
RMS_EPS = 1e-6

LANES = 128
MXU_DIM = 256
SLAB = MXU_DIM
SLAB_STATES = (SLAB // SSM_GROUP) * SSM_STATE
N_SLABS = D_SSM // SLAB
VMEM_BYTES = 64 * 1024 * 1024


def _params(vmem_mb, n_axes):
    return pltpu.CompilerParams(dimension_semantics=("arbitrary",) * n_axes,
                                vmem_limit_bytes=min(vmem_mb * 1024 * 1024, VMEM_BYTES - 4 * 1024 * 1024))


def _rms(x, w):
    return x * lax.rsqrt(jnp.mean(x * x, axis=-1, keepdims=True) + RMS_EPS) * w


def _log_sigmoid(x):
    return jnp.minimum(x, 0.0) - jnp.log1p(jnp.exp(-jnp.abs(x)))


def _sigmoid(x):
    return 1.0 / (1.0 + jnp.exp(-x))


def _split3(x):
    hi = x.astype(BF16)
    r = x - hi.astype(F32)
    mid = r.astype(BF16)
    lo = (r - mid.astype(F32)).astype(BF16)
    return hi, mid, lo


def _cumsum_rows(x):
    n = x.shape[0]
    tri = (lax.broadcasted_iota(jnp.int32, (n, n), 1) <= lax.broadcasted_iota(jnp.int32, (n, n), 0)).astype(BF16)
    hi, mid, lo = _split3(x)
    z = jnp.dot(tri, jnp.concatenate([hi, mid, lo], axis=1), preferred_element_type=F32)
    return z[:, :LANES] + z[:, LANES:2 * LANES] + z[:, 2 * LANES:]


def _inproj_kernel(x_ref, nw_ref, wq_ref, wk_ref, wv_ref, wu_ref, wf_ref, bf_ref, qg_ref, kg_ref,
                   q_ref, k_ref, kb_ref, v_ref, vb_ref, u_ref, lf_ref, *rest, tiles_per_seq):
    h = _rms(x_ref[...], nw_ref[...]).astype(BF16)
    scale = HEAD_DIM ** -0.5
    for j in range(D_ATTN // MXU_DIM):
        cols = slice(j * MXU_DIM, (j + 1) * MXU_DIM)
        zq = jnp.dot(h, wq_ref[:, cols], preferred_element_type=F32)
        zk = jnp.dot(h, wk_ref[:, cols], preferred_element_type=F32)
        zv = jnp.dot(h, wv_ref[:, cols], preferred_element_type=F32)
        for t in range(MXU_DIM // HEAD_DIM):
            sub = slice(t * HEAD_DIM, (t + 1) * HEAD_DIM)
            dst = slice(j * MXU_DIM + t * HEAD_DIM, j * MXU_DIM + (t + 1) * HEAD_DIM)
            q_ref[:, dst] = (_rms(zq[:, sub], qg_ref[...]) * scale).astype(BF16)
            kn = _rms(zk[:, sub], kg_ref[...])
            k_ref[:, dst] = kn
            kb_ref[:, dst] = kn.astype(BF16)
        v_ref[:, cols] = zv
        vb_ref[:, cols] = zv.astype(BF16)
        u_ref[:, cols] = jnp.dot(h, wu_ref[:, cols], preferred_element_type=F32)
    lf = _log_sigmoid(jnp.dot(h, wf_ref[...], preferred_element_type=F32) + bf_ref[...])
    lf_ref[...] = lf
    if tiles_per_seq is not None:
        c_ref, carry_ref = rest

        @pl.when(pl.program_id(0) % tiles_per_seq == 0)
        def _():
            carry_ref[...] = jnp.zeros_like(carry_ref)

        c = _cumsum_rows(lf) + carry_ref[...]
        c_ref[...] = c
        carry_ref[...] = c[c.shape[0] - 1:, :]


def _inproj(x, nw, wq, wk, wv, wu, wf, bf, qg, kg, seq_len):
    R = x.shape[0]
    tm = min(R, 512)
    row = lambda n: pl.BlockSpec((tm, n), lambda i: (i, 0))
    full = lambda a: pl.BlockSpec(a.shape, lambda i: (0,) * a.ndim, pipeline_mode=pl.Buffered(1))
    out_shape = [jax.ShapeDtypeStruct((R, D_ATTN), BF16), jax.ShapeDtypeStruct((R, D_ATTN), F32),
                 jax.ShapeDtypeStruct((R, D_ATTN), BF16), jax.ShapeDtypeStruct((R, D_ATTN), F32),
                 jax.ShapeDtypeStruct((R, D_ATTN), BF16), jax.ShapeDtypeStruct((R, D_SSM), F32),
                 jax.ShapeDtypeStruct((R, LANES), F32)]
    out_specs = [row(D_ATTN)] * 5 + [row(D_SSM), row(LANES)]
    scratch = []
    tiles_per_seq = None
    if seq_len is not None:
        assert seq_len % tm == 0
        tiles_per_seq = seq_len // tm
        out_shape.append(jax.ShapeDtypeStruct((R, LANES), F32))
        out_specs.append(row(LANES))
        scratch.append(pltpu.VMEM((1, LANES), F32))
    return pl.pallas_call(
        functools.partial(_inproj_kernel, tiles_per_seq=tiles_per_seq),
        grid=(R // tm,),
        in_specs=[row(D_MODEL), full(nw), full(wq), full(wk), full(wv), full(wu), full(wf), full(bf), full(qg), full(kg)],
        out_specs=out_specs, out_shape=out_shape, scratch_shapes=scratch,
        compiler_params=_params(60, 1), name="inproj",
    )(x, nw, wq, wk, wv, wu, wf, bf, qg, kg)


def _attn_prompt_kernel(q_ref, k_ref, v_ref, cq_ref, ck_ref, o_ref, *, L, tq):
    nt = (((1,), (1,)), ((), ()))
    for qi in range(L // tq):
        rows = slice(qi * tq, (qi + 1) * tq)
        q = q_ref[rows, :]
        cq = cq_ref[rows, :]
        m = l = acc = None
        for ki in range(qi + 1):
            cols = slice(ki * tq, (ki + 1) * tq)
            s = lax.dot_general(q, k_ref[cols, :], nt, preferred_element_type=F32)
            s = s + (cq - ck_ref[:, cols])
            if ki == qi:
                keep = lax.broadcasted_iota(jnp.int32, (tq, tq), 1) <= lax.broadcasted_iota(jnp.int32, (tq, tq), 0)
                s = jnp.where(keep, s, -jnp.inf)
            mx = jnp.max(s, axis=-1, keepdims=True)
            if ki == 0:
                m = mx
                p = jnp.exp(s - m)
                l = jnp.sum(p, axis=-1, keepdims=True)
                acc = jnp.dot(p.astype(BF16), v_ref[cols, :], preferred_element_type=F32)
            else:
                m_new = jnp.maximum(m, mx)
                alpha = jnp.exp(m - m_new)
                p = jnp.exp(s - m_new)
                l = alpha * l + jnp.sum(p, axis=-1, keepdims=True)
                acc = alpha * acc + jnp.dot(p.astype(BF16), v_ref[cols, :], preferred_element_type=F32)
                m = m_new
        o_ref[rows, :] = (acc / l).astype(BF16)


def _attn_prompt(qb, kb, vb, cq, ck, B, L):
    tq = min(L, 512)
    blk = pl.BlockSpec((L, HEAD_DIM), lambda b, h: (b, h))
    return pl.pallas_call(
        functools.partial(_attn_prompt_kernel, L=L, tq=tq),
        grid=(B, N_HEADS),
        in_specs=[blk, blk, blk,
                  pl.BlockSpec((None, None, L, 1), lambda b, h: (b, h, 0, 0)),
                  pl.BlockSpec((None, None, 1, L), lambda b, h: (b, h, 0, 0))],
        out_specs=blk, out_shape=jax.ShapeDtypeStruct((B * L, D_ATTN), BF16),
        compiler_params=_params(32, 2), name="attn_prompt",
    )(qb, kb, vb, cq, ck)


def _attn_sample_kernel(pt_ref, q_ref, kn_ref, vn_ref, lfn_ref, *rest, n_pages, Q):
    k_refs = rest[:n_pages]
    v_refs = rest[n_pages:2 * n_pages]
    lf_refs = rest[2 * n_pages:3 * n_pages]
    o_ref, s_ref, d_ref = rest[3 * n_pages:]
    nt = (((1,), (1,)), ((), ()))
    H = N_HEADS
    M = Q * H
    page = k_refs[0].shape[0]
    PL = page * H
    RPP = PL // LANES

    qm = q_ref[...].reshape(M, HEAD_DIM)
    qmb = qm.astype(BF16)
    same_head = (lax.broadcasted_iota(jnp.int32, (M, LANES), 1) % H
                 == lax.broadcasted_iota(jnp.int32, (M, LANES), 0) % H)

    x = jnp.concatenate([r[...] for r in lf_refs], axis=0)
    lane = lax.broadcasted_iota(jnp.int32, x.shape, 1)
    sh = H
    while sh < LANES:
        x = x + jnp.where(lane >= sh, pltpu.roll(x, sh, axis=1), 0.0)
        sh *= 2
    tot = jnp.where(lane >= LANES - H, x, 0.0)
    inc = _cumsum_rows(tot)
    after = inc[inc.shape[0] - 1:, :] - inc
    sh = H
    while sh < LANES:
        after = after + pltpu.roll(after, sh, axis=1)
        sh *= 2
    row_tot = tot
    sh = H
    while sh < LANES:
        row_tot = row_tot + pltpu.roll(row_tot, sh, axis=1)
        sh *= 2
    d_ref[...] = after + (row_tot - x)

    lfn = lfn_ref[...]
    cum = [lfn[0:H, :]]
    for i in range(1, Q):
        cum.append(cum[-1] + lfn[i * H:(i + 1) * H, :])
    cum_q = jnp.concatenate(cum, axis=0)
    cum_k = jnp.concatenate([jnp.concatenate([c] * Q, axis=0) for c in cum], axis=1)

    qm_r = qmb.astype(F32)
    tile_q = lambda a: jnp.concatenate([a.astype(BF16).astype(F32)] * Q, axis=0)
    s_new = jnp.concatenate(
        [jnp.sum(qm_r * tile_q(kn_ref[i]), axis=-1, keepdims=True) for i in range(Q)], axis=1)
    s_new = s_new + (cum_q - cum_k)
    q_of_row = lax.broadcasted_iota(jnp.int32, (M, Q), 0) // H
    s_new = jnp.where(lax.broadcasted_iota(jnp.int32, (M, Q), 1) <= q_of_row, s_new, -jnp.inf)
    m = jnp.max(s_new, axis=-1, keepdims=True)

    for j in range(n_pages):
        kb = k_refs[j][...].reshape(PL, HEAD_DIM).astype(BF16)
        s = lax.dot_general(qmb, kb, nt, preferred_element_type=F32)
        for r in range(RPP):
            blk = s[:, r * LANES:(r + 1) * LANES] + (cum_q + d_ref[j * RPP + r:j * RPP + r + 1, :])
            blk = jnp.where(same_head, blk, -jnp.inf)
            s_ref[:, (j * RPP + r) * LANES:(j * RPP + r + 1) * LANES] = blk
            m = jnp.maximum(m, jnp.max(blk, axis=-1, keepdims=True))

    p_new = jnp.exp(s_new - m)
    l = jnp.sum(p_new, axis=-1, keepdims=True)
    p_new_r = p_new.astype(BF16).astype(F32)
    acc = p_new_r[:, 0:1] * tile_q(vn_ref[0])
    for i in range(1, Q):
        acc = acc + p_new_r[:, i:i + 1] * tile_q(vn_ref[i])
    for j in range(n_pages):
        p = jnp.exp(s_ref[:, j * PL:(j + 1) * PL] - m)
        l = l + jnp.sum(p, axis=-1, keepdims=True)
        vb = v_refs[j][...].reshape(PL, HEAD_DIM).astype(BF16)
        acc = acc + jnp.dot(p.astype(BF16), vb, preferred_element_type=F32)
    o_ref[...] = (acc / l).reshape(Q, H, HEAD_DIM)


def _attn_sample(page_table_flat, q4, k4, v4, lfn_col, cache_k, cache_v, cache_lf, layer, n_phys, n_pages, Bd, Q):
    page = cache_k.shape[1]
    base = layer * n_phys
    rpp = page * N_HEADS // LANES

    def paged(tail):
        return [pl.BlockSpec((None,) + tail, functools.partial(
            lambda b, pt, j: (pt[b * n_pages + j] + base,) + (0,) * len(tail), j=j)) for j in range(n_pages)]

    tok = pl.BlockSpec((Q, None, N_HEADS, HEAD_DIM), lambda b, pt: (0, b, 0, 0))
    grid_spec = pltpu.PrefetchScalarGridSpec(
        num_scalar_prefetch=1, grid=(Bd,),
        in_specs=[tok, tok, tok, pl.BlockSpec((None, Q * N_HEADS, 1), lambda b, pt: (b, 0, 0))]
        + paged((page, N_HEADS, HEAD_DIM)) + paged((page, N_HEADS, HEAD_DIM)) + paged((rpp, LANES)),
        out_specs=tok,
        scratch_shapes=[pltpu.VMEM((Q * N_HEADS, n_pages * page * N_HEADS), F32),
                        pltpu.VMEM((n_pages * rpp, LANES), F32)])
    return pl.pallas_call(
        functools.partial(_attn_sample_kernel, n_pages=n_pages, Q=Q),
        grid_spec=grid_spec, out_shape=jax.ShapeDtypeStruct((Q, Bd, N_HEADS, HEAD_DIM), F32),
        compiler_params=_params(56, 1), name="attn_sample",
    )(page_table_flat, q4, k4, v4, lfn_col, *([cache_k] * n_pages), *([cache_v] * n_pages), *([cache_lf] * n_pages))


def _s5_kernel(u_ref, h0r_ref, h0i_ref, lr_ref, li_ref, ldt_ref, bdr_ref, bdi_ref, cdr_ref, cdi_ref, d_ref,
               y_ref, fr_ref, fi_ref, xr_ref, xi_ref, *scratch, T, NC, chained):
    lr, li = lr_ref[...], li_ref[...]
    dt = jnp.exp(ldt_ref[...])
    mag = jnp.exp(lr * dt)
    ar, ai = mag * jnp.cos(li * dt), mag * jnp.sin(li * dt)
    den = lr * lr + li * li
    zr = ((ar - 1.0) * lr + ai * li) / den
    zi = (ai * lr - (ar - 1.0) * li) / den
    bdr, bdi = bdr_ref[...], bdi_ref[...]
    pr = (zr * bdr - zi * bdi).astype(BF16)
    pi = (zr * bdi + zi * bdr).astype(BF16)
    n_cb = SLAB_STATES // LANES
    per = MXU_DIM // LANES
    if chained:
        stage_ref, perm_ref = scratch[4:]
        for t in range(per):
            stage_ref[t] = u_ref[:, t * LANES:(t + 1) * LANES]
        for s in range(T):
            for t in range(per):
                perm_ref[s * NC:(s + 1) * NC, t * LANES:(t + 1) * LANES] = stage_ref[t, pl.ds(s, NC, stride=T), :]
        u = perm_ref[...]
    else:
        u = u_ref[...]
    ub = u.astype(BF16)
    for c2 in range(n_cb // per):
        cols = slice(c2 * MXU_DIM, (c2 + 1) * MXU_DIM)
        x_r = jnp.dot(ub, pr[:, cols], preferred_element_type=F32)
        x_i = jnp.dot(ub, pi[:, cols], preferred_element_type=F32)
        for t in range(per):
            xr_ref[c2 * per + t] = x_r[:, t * LANES:(t + 1) * LANES]
            xi_ref[c2 * per + t] = x_i[:, t * LANES:(t + 1) * LANES]

    def sweep(cb, hr, hi, store):
        cols = slice(cb * LANES, (cb + 1) * LANES)
        a_r, a_i = ar[:, cols], ai[:, cols]
        for s in range(T):
            rows = slice(s * NC, (s + 1) * NC)
            xr, xi = xr_ref[cb, rows, :], xi_ref[cb, rows, :]
            hr, hi = a_r * hr - a_i * hi + xr, a_r * hi + a_i * hr + xi
            if store:
                xr_ref[cb, rows, :] = hr
                xi_ref[cb, rows, :] = hi
        return hr, hi

    if chained:
        er_ref, ei_ref, gr_ref, gi_ref = scratch[:4]
        zero = jnp.zeros((NC, LANES), F32)
        for cb in range(n_cb):
            cols = slice(cb * LANES, (cb + 1) * LANES)
            er_ref[:, cols], ei_ref[:, cols] = sweep(cb, zero, zero, False)
        tr, ti = ar, ai
        for _ in range(T - 1):
            tr, ti = tr * ar - ti * ai, tr * ai + ti * ar

        def chunk_step(k, carry):
            sr, si = carry
            gr_ref[pl.ds(k, 1), :] = sr
            gi_ref[pl.ds(k, 1), :] = si
            er, ei = er_ref[pl.ds(k, 1), :], ei_ref[pl.ds(k, 1), :]
            return tr * sr - ti * si + er, tr * si + ti * sr + ei

        sr, si = lax.fori_loop(0, NC, chunk_step, (h0r_ref[...], h0i_ref[...]))
        fr_ref[...] = sr
        fi_ref[...] = si
        for cb in range(n_cb):
            cols = slice(cb * LANES, (cb + 1) * LANES)
            sweep(cb, gr_ref[:, cols], gi_ref[:, cols], True)
    else:
        for cb in range(n_cb):
            cols = slice(cb * LANES, (cb + 1) * LANES)
            hr, hi = sweep(cb, h0r_ref[:, cols], h0i_ref[:, cols], True)
            fr_ref[:, cols] = hr
            fi_ref[:, cols] = hi

    y = d_ref[...] * u
    for c2 in range(n_cb // per):
        rows = slice(c2 * MXU_DIM, (c2 + 1) * MXU_DIM)
        h_r = jnp.concatenate([xr_ref[c2 * per + t] for t in range(per)], axis=1).astype(BF16)
        h_i = jnp.concatenate([xi_ref[c2 * per + t] for t in range(per)], axis=1).astype(BF16)
        y = y + (jnp.dot(h_r, cdr_ref[rows, :], preferred_element_type=F32)
                 - jnp.dot(h_i, cdi_ref[rows, :], preferred_element_type=F32))
    if chained:
        for s in range(T):
            for t in range(per):
                stage_ref[t, pl.ds(s, NC, stride=T), :] = y[s * NC:(s + 1) * NC, t * LANES:(t + 1) * LANES]
        for t in range(per):
            y_ref[:, t * LANES:(t + 1) * LANES] = stage_ref[t]
    else:
        y_ref[...] = y


def _s5(u, h0r, h0i, lam_re, lam_im, log_dt, bd_re, bd_im, cd_re, cd_im, d_row, n_seq, T, NC, chained):
    rows = NC * T
    slab_row = lambda: pl.BlockSpec((1, SLAB_STATES), lambda b, s: (0, s))
    if chained:
        st_spec = pl.BlockSpec((None, 1, SLAB_STATES), lambda b, s: (b, 0, s))
        st_shape = jax.ShapeDtypeStruct((n_seq, 1, N_SLABS * SLAB_STATES), F32)
        scratch = [pltpu.VMEM((NC, SLAB_STATES), F32)] * 4 + [
            pltpu.VMEM((SLAB // LANES, rows, LANES), F32), pltpu.VMEM((rows, SLAB), F32)]
    else:
        st_spec = pl.BlockSpec((NC, SLAB_STATES), lambda b, s: (0, s))
        st_shape = jax.ShapeDtypeStruct((NC, N_SLABS * SLAB_STATES), F32)
        scratch = []
    return pl.pallas_call(
        functools.partial(_s5_kernel, T=T, NC=NC, chained=chained),
        grid=(n_seq, N_SLABS),
        in_specs=[pl.BlockSpec((rows, SLAB), lambda b, s: (b, s)), st_spec, st_spec,
                  slab_row(), slab_row(), slab_row(),
                  pl.BlockSpec((None, SLAB, SLAB_STATES), lambda b, s: (s, 0, 0)),
                  pl.BlockSpec((None, SLAB, SLAB_STATES), lambda b, s: (s, 0, 0)),
                  pl.BlockSpec((None, SLAB_STATES, SLAB), lambda b, s: (s, 0, 0)),
                  pl.BlockSpec((None, SLAB_STATES, SLAB), lambda b, s: (s, 0, 0)),
                  pl.BlockSpec((1, SLAB), lambda b, s: (0, s))],
        out_specs=[pl.BlockSpec((rows, SLAB), lambda b, s: (b, s)), st_spec, st_spec],
        out_shape=[jax.ShapeDtypeStruct((n_seq * rows, D_SSM), F32), st_shape, st_shape],
        scratch_shapes=[pltpu.VMEM((SLAB_STATES // LANES, rows, LANES), F32)] * 2 + scratch,
        compiler_params=_params(48, 2), name="s5",
    )(u, h0r, h0i, lam_re, lam_im, log_dt, bd_re, bd_im, cd_re, cd_im, d_row)


def _mixout_kernel(x_ref, att_ref, y_ref, wg_ref, bg_ref, wa_ref, ws_ref, o_ref):
    y = y_ref[...]
    z = 0.5 * y * (1.0 + jnp.tanh(math.sqrt(2.0 / math.pi) * (y + 0.044715 * (y * y * y))))
    gate = _sigmoid(jnp.dot(z.astype(BF16), wg_ref[...], preferred_element_type=F32) + bg_ref[...])
    ssm = (z * gate).astype(BF16)
    o_ref[...] = (x_ref[...] + jnp.dot(att_ref[...], wa_ref[...], preferred_element_type=F32)
                  + jnp.dot(ssm, ws_ref[...], preferred_element_type=F32))


def _mixout(x, att, y, wg, bg, wa, ws):
    R = x.shape[0]
    tm = min(R, 512)
    row = lambda n: pl.BlockSpec((tm, n), lambda i: (i, 0))
    full = lambda a: pl.BlockSpec(a.shape, lambda i: (0,) * a.ndim, pipeline_mode=pl.Buffered(1))
    return pl.pallas_call(
        _mixout_kernel, grid=(R // tm,),
        in_specs=[row(D_MODEL), row(D_ATTN), row(D_SSM), full(wg), full(bg), full(wa), full(ws)],
        out_specs=row(D_MODEL), out_shape=jax.ShapeDtypeStruct((R, D_MODEL), F32),
        compiler_params=_params(56, 1), name="mixout",
    )(x, att, y, wg, bg, wa, ws)


def _ffn_kernel(x_ref, nw_ref, w1_ref, w3_ref, w2_ref, o_ref, h_ref):
    @pl.when(pl.program_id(1) == 0)
    def _():
        x = x_ref[...]
        h_ref[...] = _rms(x, nw_ref[...]).astype(BF16)
        o_ref[...] = x

    h = h_ref[...]
    a = jnp.dot(h, w1_ref[...].astype(BF16), preferred_element_type=F32)
    b = jnp.dot(h, w3_ref[...].astype(BF16), preferred_element_type=F32)
    g = (a * _sigmoid(a) * b).astype(BF16)
    o_ref[...] += jnp.dot(g, w2_ref[...].astype(BF16), preferred_element_type=F32)


def _ffn(x, nw, w1, w3, w2, layer):
    R = x.shape[0]
    d_ff = w1.shape[2]
    tm = min(R, 1024)
    tf = MXU_DIM
    assert d_ff % tf == 0 and R % tm == 0
    return pl.pallas_call(
        _ffn_kernel, grid=(R // tm, d_ff // tf),
        in_specs=[pl.BlockSpec((tm, D_MODEL), lambda i, f: (i, 0), pipeline_mode=pl.Buffered(1)),
                  pl.BlockSpec((1, D_MODEL), lambda i, f: (0, 0)),
                  pl.BlockSpec((None, D_MODEL, tf), lambda i, f: (layer, 0, f)),
                  pl.BlockSpec((None, D_MODEL, tf), lambda i, f: (layer, 0, f)),
                  pl.BlockSpec((None, tf, D_MODEL), lambda i, f: (layer, f, 0))],
        out_specs=pl.BlockSpec((tm, D_MODEL), lambda i, f: (i, 0)),
        out_shape=jax.ShapeDtypeStruct((R, D_MODEL), F32),
        scratch_shapes=[pltpu.VMEM((tm, D_MODEL), BF16)],
        compiler_params=_params(56, 2), name="ffn",
    )(x, nw, w1, w3, w2)


HIST = 16


def _pool_prompt_kernel(x_ref, nw_ref, pw_ref, ps_ref, o_ref, st_ref, ext_ref, *, tm, tiles_per_seq):
    i = pl.program_id(0)
    t_in_seq = i % tiles_per_seq

    @pl.when(t_in_seq == 0)
    def _():
        ext_ref[0:HIST, :] = jnp.zeros((HIST, D_MODEL), F32)

    x = x_ref[...]
    hp = _rms(x, nw_ref[...])
    ext_ref[HIST:HIST + tm, :] = hp
    pos = t_in_seq * tm + lax.broadcasted_iota(jnp.int32, (tm, 1), 0)
    for g, w in enumerate(POOL_WINDOWS):
        cols = slice(g * POOL_GROUP, (g + 1) * POOL_GROUP)
        acc = hp[:, cols]
        for k in range(1, w):
            acc = acc + ext_ref[HIST - k:HIST - k + tm, cols]
        inv = 1.0 / jnp.minimum(pos + 1, w).astype(F32)
        pooled = (acc * inv - hp[:, cols]).astype(BF16)
        o_ref[:, cols] = x[:, cols] + jnp.dot(pooled, pw_ref[g], preferred_element_type=F32) * ps_ref[:, cols]
    st_ref[...] = ext_ref[tm + HIST - POOL_BUF:tm + HIST, :]
    ext_ref[0:HIST, :] = ext_ref[tm:tm + HIST, :]


def _pool_prompt(x, nw, pw, ps, B, L):
    tm = min(L, 512)
    assert L % tm == 0 and tm >= HIST
    tiles_per_seq = L // tm
    return pl.pallas_call(
        functools.partial(_pool_prompt_kernel, tm=tm, tiles_per_seq=tiles_per_seq),
        grid=(B * tiles_per_seq,),
        in_specs=[pl.BlockSpec((tm, D_MODEL), lambda i: (i, 0)),
                  pl.BlockSpec((1, D_MODEL), lambda i: (0, 0)),
                  pl.BlockSpec(pw.shape, lambda i: (0, 0, 0)),
                  pl.BlockSpec((1, D_MODEL), lambda i: (0, 0))],
        out_specs=[pl.BlockSpec((tm, D_MODEL), lambda i: (i, 0)),
                   pl.BlockSpec((None, POOL_BUF, D_MODEL), lambda i: (i // tiles_per_seq, 0, 0))],
        out_shape=[jax.ShapeDtypeStruct((B * L, D_MODEL), F32), jax.ShapeDtypeStruct((B, POOL_BUF, D_MODEL), F32)],
        scratch_shapes=[pltpu.VMEM((HIST + tm, D_MODEL), F32)],
        compiler_params=_params(40, 1), name="pool_prompt",
    )(x, nw, pw, ps)


def _pool_sample(x, nw, sp, pw, ps, Bd, Q, pos0):
    G = len(POOL_WINDOWS)
    hist = [pl.BlockSpec((Bd, POOL_GROUP), functools.partial(lambda g, j: (0, j * G + g), j=j)) for j in range(POOL_BUF)]
    col = pl.BlockSpec((Q * Bd, POOL_GROUP), lambda g: (0, g))
    return pl.pallas_call(
        functools.partial(_pool_sample_kernel, Bd=Bd, Q=Q, pos0=pos0),
        grid=(G,),
        in_specs=[pl.BlockSpec((Q * Bd, D_MODEL), lambda g: (0, 0)),
                  pl.BlockSpec((1, D_MODEL), lambda g: (0, 0)),
                  pl.BlockSpec((None, POOL_GROUP, POOL_GROUP), lambda g: (g, 0, 0)),
                  pl.BlockSpec((1, POOL_GROUP), lambda g: (0, g))] + hist,
        out_specs=[col, col],
        out_shape=[jax.ShapeDtypeStruct((Q * Bd, D_MODEL), F32)] * 2,
        scratch_shapes=[pltpu.VMEM((Q * Bd, 1), F32)],
        compiler_params=_params(40, 1), name="pool_sample",
    )(x, nw, pw, ps, *([sp] * POOL_BUF))


def _pool_sample_kernel(x_ref, nw_ref, pw_ref, ps_ref, *rest, Bd, Q, pos0):
    hist_refs = rest[:POOL_BUF]
    o_ref, hs_ref, inv_ref = rest[POOL_BUF:]
    g = pl.program_id(0)

    @pl.when(g == 0)
    def _():
        x = x_ref[...]
        inv_ref[...] = lax.rsqrt(jnp.mean(x * x, axis=-1, keepdims=True) + RMS_EPS)

    for wi, w in enumerate(POOL_WINDOWS):
        @pl.when(g == wi)
        def _(wi=wi, w=w):
            cols = slice(wi * POOL_GROUP, (wi + 1) * POOL_GROUP)
            xs, hs = [], []
            for q in range(Q):
                rows = slice(q * Bd, (q + 1) * Bd)
                xq = x_ref[rows, cols]
                xs.append(xq)
                hs.append(xq * inv_ref[rows, :] * nw_ref[:, cols])
                hs_ref[rows, :] = hs[q]
            ext = lambda j: hist_refs[j][...] if j < POOL_BUF else hs[j - POOL_BUF]
            for q in range(Q):
                acc = hs[q]
                for k in range(1, w):
                    acc = acc + ext(POOL_BUF + q - k)
                cnt = float(min(pos0 + q + 1, w))
                pooled = (acc / cnt - hs[q]).astype(BF16)
                val = jnp.dot(pooled, pw_ref[...], preferred_element_type=F32) * ps_ref[...]
                o_ref[q * Bd:(q + 1) * Bd, :] = xs[q] + val


def _block_diag(w):
    G, a, b = w.shape
    gps = SLAB // SSM_GROUP
    w4 = w.reshape(N_SLABS, gps, a, b)
    eye = jnp.eye(gps, dtype=w.dtype)
    return jnp.einsum('sgab,gh->sgahb', w4, eye).reshape(N_SLABS, gps * a, gps * b)


def kernel(x_prompt, x_sample, cache_k, cache_v, cache_logf, state_s5_re, state_s5_im, state_pool, page_table, norm_mix_w, norm_ffn_w, w_in, b_f, q_norm_w, k_norm_w, s5_lam_re, s5_lam_im, s5_log_dt, s5_b_re, s5_b_im, s5_c_re, s5_c_im, s5_d, w_glu, b_glu, w_out, pool_w, pool_scale, ffn_w1, ffn_w3, ffn_w2):
    B, L, D = x_prompt.shape
    Bd, Q, _ = x_sample.shape
    depth = norm_mix_w.shape[0]
    n_attn, n_phys, page = cache_k.shape[:3]
    n_pages = page_table.shape[1]
    past_len = n_pages * page
    T = 16
    assert D == D_MODEL and L % T == 0

    xp = x_prompt.reshape(B * L, D)
    xs = jnp.swapaxes(x_sample, 0, 1).reshape(Q * Bd, D)
    from_tm = lambda a, *tail: jnp.swapaxes(a.reshape(Q, Bd, *tail), 0, 1)
    pt_flat = page_table.reshape(-1)
    assert (page * N_HEADS) % LANES == 0
    ck_flat = cache_k.reshape(n_attn * n_phys, page, N_HEADS, HEAD_DIM)
    cv_flat = cache_v.reshape(n_attn * n_phys, page, N_HEADS, HEAD_DIM)
    clf_flat = cache_logf.reshape(n_attn * n_phys, page * N_HEADS // LANES, LANES)

    kp, vp, fp, srp, sip, plp = [], [], [], [], [], []
    ksm, vsm, fsm, srs, sis, pls = [], [], [], [], [], []
    for layer in range(depth):
        i = layer // 2
        nw = norm_mix_w[layer].reshape(1, D)
        if layer % 2 == 0:
            w = w_in[i]
            wq = w[:, :D_ATTN].astype(BF16)
            wk = w[:, D_ATTN:2 * D_ATTN].astype(BF16)
            wv = w[:, 2 * D_ATTN:3 * D_ATTN].astype(BF16)
            wf = jnp.pad(w[:, 3 * D_ATTN:3 * D_ATTN + N_HEADS], ((0, 0), (0, LANES - N_HEADS))).astype(BF16)
            wu = w[:, 3 * D_ATTN + N_HEADS:].astype(BF16)
            bf = jnp.pad(b_f[i], (0, LANES - N_HEADS)).reshape(1, LANES)
            qg = q_norm_w[i].reshape(1, HEAD_DIM)
            kg = k_norm_w[i].reshape(1, HEAD_DIM)
            wg = w_glu[i].astype(BF16)
            bg = b_glu[i].reshape(1, D_SSM)
            wa = w_out[i][:D_ATTN].astype(BF16)
            ws = w_out[i][D_ATTN:].astype(BF16)
            lam_re = s5_lam_re[i].reshape(1, -1)
            lam_im = s5_lam_im[i].reshape(1, -1)
            log_dt = jnp.broadcast_to(s5_log_dt[i][:, None], (N_SSM_GROUPS, SSM_STATE)).reshape(1, -1)
            bd_re = _block_diag(jnp.swapaxes(s5_b_re[i], 1, 2))
            bd_im = _block_diag(jnp.swapaxes(s5_b_im[i], 1, 2))
            cd_re = _block_diag(jnp.swapaxes(s5_c_re[i], 1, 2)).astype(BF16)
            cd_im = _block_diag(jnp.swapaxes(s5_c_im[i], 1, 2)).astype(BF16)
            d_row = s5_d[i].reshape(1, D_SSM)

            qb, k, kb, v, vb, u, lf, c = _inproj(xp, nw, wq, wk, wv, wu, wf, bf, qg, kg, seq_len=L)
            c_t = jnp.swapaxes(c[:, :N_HEADS].reshape(B, L, N_HEADS), 1, 2)
            att = _attn_prompt(qb, kb, vb, c_t[..., None], c_t[:, :, None, :], B, L)
            zeros = jnp.zeros((B, 1, N_SLABS * SLAB_STATES), F32)
            y, h_re, h_im = _s5(u, zeros, zeros, lam_re, lam_im, log_dt, bd_re, bd_im, cd_re, cd_im, d_row,
                                n_seq=B, T=T, NC=L // T, chained=True)
            xp = _mixout(xp, att, y, wg, bg, wa, ws)
            kp.append(k.reshape(B, L, N_HEADS, HEAD_DIM))
            vp.append(v.reshape(B, L, N_HEADS, HEAD_DIM))
            fp.append(lf[:, :N_HEADS].reshape(B, L, N_HEADS))
            srp.append(h_re.reshape(B, N_SSM_GROUPS, SSM_STATE))
            sip.append(h_im.reshape(B, N_SSM_GROUPS, SSM_STATE))

            qb, k, kb, v, vb, u, lf = _inproj(xs, nw, wq, wk, wv, wu, wf, bf, qg, kg, seq_len=None)
            lfn = lf[:, :N_HEADS].reshape(Q, Bd, N_HEADS)
            heads = lambda a: a.astype(F32).reshape(Q, Bd, N_HEADS, HEAD_DIM)
            att = _attn_sample(pt_flat, heads(qb), heads(k), heads(v),
                               jnp.swapaxes(lfn, 0, 1).reshape(Bd, Q * N_HEADS, 1),
                               ck_flat, cv_flat, clf_flat, i, n_phys, n_pages, Bd, Q).astype(BF16)
            y, h_re, h_im = _s5(u, state_s5_re[i].reshape(Bd, -1), state_s5_im[i].reshape(Bd, -1),
                                lam_re, lam_im, log_dt, bd_re, bd_im, cd_re, cd_im, d_row,
                                n_seq=1, T=Q, NC=Bd, chained=False)
            xs = _mixout(xs, att.reshape(Q * Bd, D_ATTN), y, wg, bg, wa, ws)
            ksm.append(from_tm(k, N_HEADS, HEAD_DIM))
            vsm.append(from_tm(v, N_HEADS, HEAD_DIM))
            fsm.append(jnp.swapaxes(lfn, 0, 1))
            srs.append(h_re.reshape(Bd, N_SSM_GROUPS, SSM_STATE))
            sis.append(h_im.reshape(Bd, N_SSM_GROUPS, SSM_STATE))
        else:
            pw = pool_w[i].astype(BF16)
            ps = pool_scale[i].reshape(1, D)
            xp, st = _pool_prompt(xp, nw, pw, ps, B, L)
            plp.append(st)
            xs, hs = _pool_sample(xs, nw, state_pool[i].reshape(Bd, POOL_BUF * D), pw, ps, Bd, Q, past_len)
            pls.append(jnp.concatenate([state_pool[i], from_tm(hs, D)], axis=1)[:, -POOL_BUF:])
        nf = norm_ffn_w[layer].reshape(1, D)
        xp = _ffn(xp, nf, ffn_w1, ffn_w3, ffn_w2, layer)
        xs = _ffn(xs, nf, ffn_w1, ffn_w3, ffn_w2, layer)
    return (xp.reshape(B, L, D), from_tm(xs, D),
            jnp.stack(kp), jnp.stack(vp), jnp.stack(fp), jnp.stack(srp), jnp.stack(sip), jnp.stack(plp),
            jnp.stack(ksm), jnp.stack(vsm), jnp.stack(fsm), jnp.stack(srs), jnp.stack(sis), jnp.stack(pls))
```

```python
import functools
import math

import jax
import jax.numpy as jnp
from jax import lax
from jax.experimental import pallas as pl
from jax.experimental.pallas import tpu as pltpu

F32 = jnp.float32
BF16 = jnp.bfloat16

D_MODEL = 2048
D_ATTN = D_MODEL // 2
HEAD_DIM = 128
N_HEADS = D_ATTN // HEAD_DIM
D_SSM = D_MODEL - D_ATTN
SSM_GROUP = 16
N_SSM_GROUPS = D_SSM // SSM_GROUP
SSM_STATE = 64
POOL_WINDOWS = (2, 4, 8, 16)
POOL_GROUP = D_MODEL // len(POOL_WINDOWS)
POOL_BUF = max(POOL_WINDOWS) - 1
RMS_EPS = 1e-6

LANES = 128
MXU_DIM = 256
SLAB = MXU_DIM
SLAB_STATES = (SLAB // SSM_GROUP) * SSM_STATE
N_SLABS = D_SSM // SLAB
VMEM_BYTES = 64 * 1024 * 1024


def _params(vmem_mb, n_axes):
    return pltpu.CompilerParams(dimension_semantics=("arbitrary",) * n_axes,
                                vmem_limit_bytes=min(vmem_mb * 1024 * 1024, VMEM_BYTES - 4 * 1024 * 1024))


def _rms(x, w):
    return x * lax.rsqrt(jnp.mean(x * x, axis=-1, keepdims=True) + RMS_EPS) * w


def _log_sigmoid(x):
    return jnp.minimum(x, 0.0) - jnp.log1p(jnp.exp(-jnp.abs(x)))


def _sigmoid(x):
    return 1.0 / (1.0 + jnp.exp(-x))


def _split3(x):
    hi = x.astype(BF16)
    r = x - hi.astype(F32)
    mid = r.astype(BF16)
    lo = (r - mid.astype(F32)).astype(BF16)
    return hi, mid, lo


def _cumsum_rows(x):
    n = x.shape[0]
    tri = (lax.broadcasted_iota(jnp.int32, (n, n), 1) <= lax.broadcasted_iota(jnp.int32, (n, n), 0)).astype(BF16)
    hi, mid, lo = _split3(x)
    z = jnp.dot(tri, jnp.concatenate([hi, mid, lo], axis=1), preferred_element_type=F32)
    return z[:, :LANES] + z[:, LANES:2 * LANES] + z[:, 2 * LANES:]


def _inproj_kernel(x_ref, nw_ref, wq_ref, wk_ref, wv_ref, wu_ref, wf_ref, bf_ref, qg_ref, kg_ref,
                   q_ref, k_ref, kb_ref, v_ref, vb_ref, u_ref, lf_ref, *rest, tiles_per_seq):
    h = _rms(x_ref[...], nw_ref[...]).astype(BF16)
    scale = HEAD_DIM ** -0.5
    for j in range(D_ATTN // MXU_DIM):
        cols = slice(j * MXU_DIM, (j + 1) * MXU_DIM)
        zq = jnp.dot(h, wq_ref[:, cols], preferred_element_type=F32)
        zk = jnp.dot(h, wk_ref[:, cols], preferred_element_type=F32)
        zv = jnp.dot(h, wv_ref[:, cols], preferred_element_type=F32)
        for t in range(MXU_DIM // HEAD_DIM):
            sub = slice(t * HEAD_DIM, (t + 1) * HEAD_DIM)
            dst = slice(j * MXU_DIM + t * HEAD_DIM, j * MXU_DIM + (t + 1) * HEAD_DIM)
            q_ref[:, dst] = (_rms(zq[:, sub], qg_ref[...]) * scale).astype(BF16)
            kn = _rms(zk[:, sub], kg_ref[...])
            k_ref[:, dst] = kn
            kb_ref[:, dst] = kn.astype(BF16)
        v_ref[:, cols] = zv
        vb_ref[:, cols] = zv.astype(BF16)
        u_ref[:, cols] = jnp.dot(h, wu_ref[:, cols], preferred_element_type=F32)
    lf = _log_sigmoid(jnp.dot(h, wf_ref[...], preferred_element_type=F32) + bf_ref[...])
    lf_ref[...] = lf
    if tiles_per_seq is not None:
        c_ref, carry_ref = rest

        @pl.when(pl.program_id(0) % tiles_per_seq == 0)
        def _():
            carry_ref[...] = jnp.zeros_like(carry_ref)

        c = _cumsum_rows(lf) + carry_ref[...]
        c_ref[...] = c
        carry_ref[...] = c[c.shape[0] - 1:, :]


def _inproj(x, nw, wq, wk, wv, wu, wf, bf, qg, kg, seq_len):
    R = x.shape[0]
    tm = min(R, 512)
    row = lambda n: pl.BlockSpec((tm, n), lambda i: (i, 0))
    full = lambda a: pl.BlockSpec(a.shape, lambda i: (0,) * a.ndim, pipeline_mode=pl.Buffered(1))
    out_shape = [jax.ShapeDtypeStruct((R, D_ATTN), BF16), jax.ShapeDtypeStruct((R, D_ATTN), F32),
                 jax.ShapeDtypeStruct((R, D_ATTN), BF16), jax.ShapeDtypeStruct((R, D_ATTN), F32),
                 jax.ShapeDtypeStruct((R, D_ATTN), BF16), jax.ShapeDtypeStruct((R, D_SSM), F32),
                 jax.ShapeDtypeStruct((R, LANES), F32)]
    out_specs = [row(D_ATTN)] * 5 + [row(D_SSM), row(LANES)]
    scratch = []
    tiles_per_seq = None
    if seq_len is not None:
        assert seq_len % tm == 0
        tiles_per_seq = seq_len // tm
        out_shape.append(jax.ShapeDtypeStruct((R, LANES), F32))
        out_specs.append(row(LANES))
        scratch.append(pltpu.VMEM((1, LANES), F32))
    return pl.pallas_call(
        functools.partial(_inproj_kernel, tiles_per_seq=tiles_per_seq),
        grid=(R // tm,),
        in_specs=[row(D_MODEL), full(nw), full(wq), full(wk), full(wv), full(wu), full(wf), full(bf), full(qg), full(kg)],
        out_specs=out_specs, out_shape=out_shape, scratch_shapes=scratch,
        compiler_params=_params(60, 1), name="inproj",
    )(x, nw, wq, wk, wv, wu, wf, bf, qg, kg)


def _attn_prompt_kernel(q_ref, k_ref, v_ref, cq_ref, ck_ref, o_ref, *, L, tq):
    nt = (((1,), (1,)), ((), ()))
    for qi in range(L // tq):
        rows = slice(qi * tq, (qi + 1) * tq)
        q = q_ref[rows, :]
        cq = cq_ref[rows, :]
        m = l = acc = None
        for ki in range(qi + 1):
            cols = slice(ki * tq, (ki + 1) * tq)
            s = lax.dot_general(q, k_ref[cols, :], nt, preferred_element_type=F32)
            s = s + (cq - ck_ref[:, cols])
            if ki == qi:
                keep = lax.broadcasted_iota(jnp.int32, (tq, tq), 1) <= lax.broadcasted_iota(jnp.int32, (tq, tq), 0)
                s = jnp.where(keep, s, -jnp.inf)
            mx = jnp.max(s, axis=-1, keepdims=True)
            if ki == 0:
                m = mx
                p = jnp.exp(s - m)
                l = jnp.sum(p, axis=-1, keepdims=True)
                acc = jnp.dot(p.astype(BF16), v_ref[cols, :], preferred_element_type=F32)
            else:
                m_new = jnp.maximum(m, mx)
                alpha = jnp.exp(m - m_new)
                p = jnp.exp(s - m_new)
                l = alpha * l + jnp.sum(p, axis=-1, keepdims=True)
                acc = alpha * acc + jnp.dot(p.astype(BF16), v_ref[cols, :], preferred_element_type=F32)
                m = m_new
        o_ref[rows, :] = (acc / l).astype(BF16)


def _attn_prompt(qb, kb, vb, cq, ck, B, L):
    tq = min(L, 512)
    blk = pl.BlockSpec((L, HEAD_DIM), lambda b, h: (b, h))
    return pl.pallas_call(
        functools.partial(_attn_prompt_kernel, L=L, tq=tq),
        grid=(B, N_HEADS),
        in_specs=[blk, blk, blk,
                  pl.BlockSpec((None, None, L, 1), lambda b, h: (b, h, 0, 0)),
                  pl.BlockSpec((None, None, 1, L), lambda b, h: (b, h, 0, 0))],
        out_specs=blk, out_shape=jax.ShapeDtypeStruct((B * L, D_ATTN), BF16),
        compiler_params=_params(32, 2), name="attn_prompt",
    )(qb, kb, vb, cq, ck)


def _attn_sample_kernel(pt_ref, q_ref, kn_ref, vn_ref, lfn_ref, *rest, n_pages, Q):
    k_refs = rest[:n_pages]
    v_refs = rest[n_pages:2 * n_pages]
    lf_refs = rest[2 * n_pages:3 * n_pages]
    o_ref, s_ref, d_ref = rest[3 * n_pages:]
    nt = (((1,), (1,)), ((), ()))
    H = N_HEADS
    M = Q * H
    page = k_refs[0].shape[0]
    PL = page * H
    RPP = PL // LANES

    qm = q_ref[...].reshape(M, HEAD_DIM)
    qmb = qm.astype(BF16)
    same_head = (lax.broadcasted_iota(jnp.int32, (M, LANES), 1) % H
                 == lax.broadcasted_iota(jnp.int32, (M, LANES), 0) % H)

    x = jnp.concatenate([r[...] for r in lf_refs], axis=0)
    lane = lax.broadcasted_iota(jnp.int32, x.shape, 1)
    sh = H
    while sh < LANES:
        x = x + jnp.where(lane >= sh, pltpu.roll(x, sh, axis=1), 0.0)
        sh *= 2
    tot = jnp.where(lane >= LANES - H, x, 0.0)
    inc = _cumsum_rows(tot)
    after = inc[inc.shape[0] - 1:, :] - inc
    sh = H
    while sh < LANES:
        after = after + pltpu.roll(after, sh, axis=1)
        sh *= 2
    row_tot = tot
    sh = H
    while sh < LANES:
        row_tot = row_tot + pltpu.roll(row_tot, sh, axis=1)
        sh *= 2
    d_ref[...] = after + (row_tot - x)

    lfn = lfn_ref[...]
    cum = [lfn[0:H, :]]
    for i in range(1, Q):
        cum.append(cum[-1] + lfn[i * H:(i + 1) * H, :])
    cum_q = jnp.concatenate(cum, axis=0)
    cum_k = jnp.concatenate([jnp.concatenate([c] * Q, axis=0) for c in cum], axis=1)

    qm_r = qmb.astype(F32)
    tile_q = lambda a: jnp.concatenate([a.astype(BF16).astype(F32)] * Q, axis=0)
    s_new = jnp.concatenate(
        [jnp.sum(qm_r * tile_q(kn_ref[i]), axis=-1, keepdims=True) for i in range(Q)], axis=1)
    s_new = s_new + (cum_q - cum_k)
    q_of_row = lax.broadcasted_iota(jnp.int32, (M, Q), 0) // H
    s_new = jnp.where(lax.broadcasted_iota(jnp.int32, (M, Q), 1) <= q_of_row, s_new, -jnp.inf)
    m = jnp.max(s_new, axis=-1, keepdims=True)

    for j in range(n_pages):
        kb = k_refs[j][...].reshape(PL, HEAD_DIM).astype(BF16)
        s = lax.dot_general(qmb, kb, nt, preferred_element_type=F32)
        for r in range(RPP):
            blk = s[:, r * LANES:(r + 1) * LANES] + (cum_q + d_ref[j * RPP + r:j * RPP + r + 1, :])
            blk = jnp.where(same_head, blk, -jnp.inf)
            s_ref[:, (j * RPP + r) * LANES:(j * RPP + r + 1) * LANES] = blk
            m = jnp.maximum(m, jnp.max(blk, axis=-1, keepdims=True))

    p_new = jnp.exp(s_new - m)
    l = jnp.sum(p_new, axis=-1, keepdims=True)
    p_new_r = p_new.astype(BF16).astype(F32)
    acc = p_new_r[:, 0:1] * tile_q(vn_ref[0])
    for i in range(1, Q):
        acc = acc + p_new_r[:, i:i + 1] * tile_q(vn_ref[i])
    for j in range(n_pages):
        p = jnp.exp(s_ref[:, j * PL:(j + 1) * PL] - m)
        l = l + jnp.sum(p, axis=-1, keepdims=True)
        vb = v_refs[j][...].reshape(PL, HEAD_DIM).astype(BF16)
        acc = acc + jnp.dot(p.astype(BF16), vb, preferred_element_type=F32)
    o_ref[...] = (acc / l).reshape(Q, H, HEAD_DIM)


def _attn_sample(page_table_flat, q4, k4, v4, lfn_col, cache_k, cache_v, cache_lf, layer, n_phys, n_pages, Bd, Q):
    page = cache_k.shape[1]
    base = layer * n_phys
    rpp = page * N_HEADS // LANES

    def paged(tail):
        return [pl.BlockSpec((None,) + tail, functools.partial(
            lambda b, pt, j: (pt[b * n_pages + j] + base,) + (0,) * len(tail), j=j)) for j in range(n_pages)]

    tok = pl.BlockSpec((Q, None, N_HEADS, HEAD_DIM), lambda b, pt: (0, b, 0, 0))
    grid_spec = pltpu.PrefetchScalarGridSpec(
        num_scalar_prefetch=1, grid=(Bd,),
        in_specs=[tok, tok, tok, pl.BlockSpec((None, Q * N_HEADS, 1), lambda b, pt: (b, 0, 0))]
        + paged((page, N_HEADS, HEAD_DIM)) + paged((page, N_HEADS, HEAD_DIM)) + paged((rpp, LANES)),
        out_specs=tok,
        scratch_shapes=[pltpu.VMEM((Q * N_HEADS, n_pages * page * N_HEADS), F32),
                        pltpu.VMEM((n_pages * rpp, LANES), F32)])
    return pl.pallas_call(
        functools.partial(_attn_sample_kernel, n_pages=n_pages, Q=Q),
        grid_spec=grid_spec, out_shape=jax.ShapeDtypeStruct((Q, Bd, N_HEADS, HEAD_DIM), F32),
        compiler_params=_params(56, 1), name="attn_sample",
    )(page_table_flat, q4, k4, v4, lfn_col, *([cache_k] * n_pages), *([cache_v] * n_pages), *([cache_lf] * n_pages))


def _s5_kernel(u_ref, h0r_ref, h0i_ref, lr_ref, li_ref, ldt_ref, btr_ref, bti_ref, ctr_ref, cti_ref, d_ref,
               y_ref, fr_ref, fi_ref, xr_ref, xi_ref, bdr_ref, bdi_ref, cdr_ref, cdi_ref, *scratch, T, NC, chained):
    @pl.when((pl.program_id(0) == 0) & (pl.program_id(1) == 0))
    def _():
        for ref in (bdr_ref, bdi_ref, cdr_ref, cdi_ref):
            ref[...] = jnp.zeros_like(ref)

    for g in range(SLAB // SSM_GROUP):
        rows_c, rows_p = slice(g * SSM_GROUP, (g + 1) * SSM_GROUP), slice(g * SSM_STATE, (g + 1) * SSM_STATE)
        bdr_ref[rows_c, rows_p] = btr_ref[g]
        bdi_ref[rows_c, rows_p] = bti_ref[g]
        cdr_ref[rows_p, rows_c] = ctr_ref[g]
        cdi_ref[rows_p, rows_c] = cti_ref[g]

    lr, li = lr_ref[...], li_ref[...]
    dt = jnp.exp(ldt_ref[...])
    mag = jnp.exp(lr * dt)
    ar, ai = mag * jnp.cos(li * dt), mag * jnp.sin(li * dt)
    den = lr * lr + li * li
    zr = ((ar - 1.0) * lr + ai * li) / den
    zi = (ai * lr - (ar - 1.0) * li) / den
    bdr, bdi = bdr_ref[...], bdi_ref[...]
    pr = (zr * bdr - zi * bdi).astype(BF16)
    pi = (zr * bdi + zi * bdr).astype(BF16)
    n_cb = SLAB_STATES // LANES
    per = MXU_DIM // LANES
    if chained:
        stage_ref, perm_ref = scratch[4:]
        for t in range(per):
            stage_ref[t] = u_ref[:, t * LANES:(t + 1) * LANES]
        for s in range(T):
            for t in range(per):
                perm_ref[s * NC:(s + 1) * NC, t * LANES:(t + 1) * LANES] = stage_ref[t, pl.ds(s, NC, stride=T), :]
        u = perm_ref[...]
    else:
        u = u_ref[...]
    ub = u.astype(BF16)
    for c2 in range(n_cb // per):
        cols = slice(c2 * MXU_DIM, (c2 + 1) * MXU_DIM)
        x_r = jnp.dot(ub, pr[:, cols], preferred_element_type=F32)
        x_i = jnp.dot(ub, pi[:, cols], preferred_element_type=F32)
        for t in range(per):
            xr_ref[c2 * per + t] = x_r[:, t * LANES:(t + 1) * LANES]
            xi_ref[c2 * per + t] = x_i[:, t * LANES:(t + 1) * LANES]

    def sweep(cb, hr, hi, store):
        cols = slice(cb * LANES, (cb + 1) * LANES)
        a_r, a_i = ar[:, cols], ai[:, cols]
        for s in range(T):
            rows = slice(s * NC, (s + 1) * NC)
            xr, xi = xr_ref[cb, rows, :], xi_ref[cb, rows, :]
            hr, hi = a_r * hr - a_i * hi + xr, a_r * hi + a_i * hr + xi
            if store:
                xr_ref[cb, rows, :] = hr
                xi_ref[cb, rows, :] = hi
        return hr, hi

    if chained:
        er_ref, ei_ref, gr_ref, gi_ref = scratch[:4]
        zero = jnp.zeros((NC, LANES), F32)
        for cb in range(n_cb):
            cols = slice(cb * LANES, (cb + 1) * LANES)
            er_ref[:, cols], ei_ref[:, cols] = sweep(cb, zero, zero, False)
        tr, ti = ar, ai
        for _ in range(T - 1):
            tr, ti = tr * ar - ti * ai, tr * ai + ti * ar

        def chunk_step(k, carry):
            sr, si = carry
            gr_ref[pl.ds(k, 1), :] = sr
            gi_ref[pl.ds(k, 1), :] = si
            er, ei = er_ref[pl.ds(k, 1), :], ei_ref[pl.ds(k, 1), :]
            return tr * sr - ti * si + er, tr * si + ti * sr + ei

        sr, si = lax.fori_loop(0, NC, chunk_step, (h0r_ref[...], h0i_ref[...]))
        fr_ref[...] = sr
        fi_ref[...] = si
        for cb in range(n_cb):
            cols = slice(cb * LANES, (cb + 1) * LANES)
            sweep(cb, gr_ref[:, cols], gi_ref[:, cols], True)
    else:
        for cb in range(n_cb):
            cols = slice(cb * LANES, (cb + 1) * LANES)
            hr, hi = sweep(cb, h0r_ref[:, cols], h0i_ref[:, cols], True)
            fr_ref[:, cols] = hr
            fi_ref[:, cols] = hi

    y = d_ref[...] * u
    for c2 in range(n_cb // per):
        rows = slice(c2 * MXU_DIM, (c2 + 1) * MXU_DIM)
        h_r = jnp.concatenate([xr_ref[c2 * per + t] for t in range(per)], axis=1).astype(BF16)
        h_i = jnp.concatenate([xi_ref[c2 * per + t] for t in range(per)], axis=1).astype(BF16)
        y = y + (jnp.dot(h_r, cdr_ref[rows, :].astype(BF16), preferred_element_type=F32)
                 - jnp.dot(h_i, cdi_ref[rows, :].astype(BF16), preferred_element_type=F32))
    if chained:
        for s in range(T):
            for t in range(per):
                stage_ref[t, pl.ds(s, NC, stride=T), :] = y[s * NC:(s + 1) * NC, t * LANES:(t + 1) * LANES]
        for t in range(per):
            y_ref[:, t * LANES:(t + 1) * LANES] = stage_ref[t]
    else:
        y_ref[...] = y


def _s5(u, h0r, h0i, lam_re, lam_im, log_dt, bt_re, bt_im, ct_re, ct_im, d_row, n_seq, T, NC, chained):
    rows = NC * T
    gps = SLAB // SSM_GROUP
    slab_row = lambda: pl.BlockSpec((1, SLAB_STATES), lambda b, s: (0, s))
    if chained:
        st_spec = pl.BlockSpec((None, 1, SLAB_STATES), lambda b, s: (b, 0, s))
        st_shape = jax.ShapeDtypeStruct((n_seq, 1, N_SLABS * SLAB_STATES), F32)
        scratch = [pltpu.VMEM((NC, SLAB_STATES), F32)] * 4 + [
            pltpu.VMEM((SLAB // LANES, rows, LANES), F32), pltpu.VMEM((rows, SLAB), F32)]
    else:
        st_spec = pl.BlockSpec((NC, SLAB_STATES), lambda b, s: (0, s))
        st_shape = jax.ShapeDtypeStruct((NC, N_SLABS * SLAB_STATES), F32)
        scratch = []
    return pl.pallas_call(
        functools.partial(_s5_kernel, T=T, NC=NC, chained=chained),
        grid=(n_seq, N_SLABS),
        in_specs=[pl.BlockSpec((rows, SLAB), lambda b, s: (b, s)), st_spec, st_spec,
                  slab_row(), slab_row(), slab_row(),
                  pl.BlockSpec((gps, SSM_GROUP, SSM_STATE), lambda b, s: (s, 0, 0)),
                  pl.BlockSpec((gps, SSM_GROUP, SSM_STATE), lambda b, s: (s, 0, 0)),
                  pl.BlockSpec((gps, SSM_STATE, SSM_GROUP), lambda b, s: (s, 0, 0)),
                  pl.BlockSpec((gps, SSM_STATE, SSM_GROUP), lambda b, s: (s, 0, 0)),
                  pl.BlockSpec((1, SLAB), lambda b, s: (0, s))],
        out_specs=[pl.BlockSpec((rows, SLAB), lambda b, s: (b, s)), st_spec, st_spec],
        out_shape=[jax.ShapeDtypeStruct((n_seq * rows, D_SSM), F32), st_shape, st_shape],
        scratch_shapes=[pltpu.VMEM((SLAB_STATES // LANES, rows, LANES), F32)] * 2
        + [pltpu.VMEM((SLAB, SLAB_STATES), F32)] * 2 + [pltpu.VMEM((SLAB_STATES, SLAB), F32)] * 2 + scratch,
        compiler_params=_params(48, 2), name="s5",
    )(u, h0r, h0i, lam_re, lam_im, log_dt, bt_re, bt_im, ct_re, ct_im, d_row)


def _mixout_kernel(x_ref, att_ref, y_ref, wg_ref, bg_ref, wa_ref, ws_ref, o_ref):
    y = y_ref[...]
    z = 0.5 * y * (1.0 + jnp.tanh(math.sqrt(2.0 / math.pi) * (y + 0.044715 * (y * y * y))))
    gate = _sigmoid(jnp.dot(z.astype(BF16), wg_ref[...], preferred_element_type=F32) + bg_ref[...])
    ssm = (z * gate).astype(BF16)
    o_ref[...] = (x_ref[...] + jnp.dot(att_ref[...], wa_ref[...], preferred_element_type=F32)
                  + jnp.dot(ssm, ws_ref[...], preferred_element_type=F32))


def _mixout(x, att, y, wg, bg, wa, ws):
    R = x.shape[0]
    tm = min(R, 512)
    row = lambda n: pl.BlockSpec((tm, n), lambda i: (i, 0))
    full = lambda a: pl.BlockSpec(a.shape, lambda i: (0,) * a.ndim, pipeline_mode=pl.Buffered(1))
    return pl.pallas_call(
        _mixout_kernel, grid=(R // tm,),
        in_specs=[row(D_MODEL), row(D_ATTN), row(D_SSM), full(wg), full(bg), full(wa), full(ws)],
        out_specs=row(D_MODEL), out_shape=jax.ShapeDtypeStruct((R, D_MODEL), F32),
        compiler_params=_params(56, 1), name="mixout",
    )(x, att, y, wg, bg, wa, ws)


def _ffn_kernel(x_ref, nw_ref, w1_ref, w3_ref, w2_ref, o_ref, h_ref):
    @pl.when(pl.program_id(1) == 0)
    def _():
        x = x_ref[...]
        h_ref[...] = _rms(x, nw_ref[...]).astype(BF16)
        o_ref[...] = x

    h = h_ref[...]
    a = jnp.dot(h, w1_ref[...].astype(BF16), preferred_element_type=F32)
    b = jnp.dot(h, w3_ref[...].astype(BF16), preferred_element_type=F32)
    g = (a * _sigmoid(a) * b).astype(BF16)
    o_ref[...] += jnp.dot(g, w2_ref[...].astype(BF16), preferred_element_type=F32)


def _ffn(x, nw, w1, w3, w2, layer):
    R = x.shape[0]
    d_ff = w1.shape[2]
    tm = min(R, 1024)
    tf = MXU_DIM
    assert d_ff % tf == 0 and R % tm == 0
    return pl.pallas_call(
        _ffn_kernel, grid=(R // tm, d_ff // tf),
        in_specs=[pl.BlockSpec((tm, D_MODEL), lambda i, f: (i, 0)),
                  pl.BlockSpec((1, D_MODEL), lambda i, f: (0, 0)),
                  pl.BlockSpec((None, D_MODEL, tf), lambda i, f: (layer, 0, f)),
                  pl.BlockSpec((None, D_MODEL, tf), lambda i, f: (layer, 0, f)),
                  pl.BlockSpec((None, tf, D_MODEL), lambda i, f: (layer, f, 0))],
        out_specs=pl.BlockSpec((tm, D_MODEL), lambda i, f: (i, 0)),
        out_shape=jax.ShapeDtypeStruct((R, D_MODEL), F32),
        scratch_shapes=[pltpu.VMEM((tm, D_MODEL), BF16)],
        compiler_params=_params(56, 2), name="ffn",
    )(x, nw, w1, w3, w2)


SUBLANES = 8
HIST = SUBLANES * (max(POOL_WINDOWS).bit_length() - 1)


def _pool_prompt_kernel(x_ref, nw_ref, pw_ref, ps_ref, o_ref, st_ref, ext_ref, lvl_ref, *, tm, tiles_per_seq):
    i = pl.program_id(0)
    t_in_seq = i % tiles_per_seq
    n = HIST + tm

    @pl.when(t_in_seq == 0)
    def _():
        ext_ref[0:HIST, :] = jnp.zeros((HIST, D_MODEL), F32)

    x = x_ref[...]
    hp = _rms(x, nw_ref[...])
    ext_ref[HIST:n, :] = hp
    pos = t_in_seq * tm + lax.broadcasted_iota(jnp.int32, (tm, 1), 0)
    for g, w in enumerate(POOL_WINDOWS):
        cols = slice(g * POOL_GROUP, (g + 1) * POOL_GROUP)
        n_levels = w.bit_length() - 1
        assert w == 1 << n_levels and SUBLANES * n_levels <= HIST
        acc = None
        for k in range(n_levels):
            start, sh = SUBLANES * (k + 1), 1 << k
            if k == 0:
                acc = ext_ref[start:n, cols] + ext_ref[start - sh:n - sh, cols]
            else:
                acc = lvl_ref[k - 1, start:n, :] + lvl_ref[k - 1, start - sh:n - sh, :]
            if k < n_levels - 1:
                lvl_ref[k, start:n, :] = acc
        acc = acc[HIST - SUBLANES * n_levels:, :]
        inv = 1.0 / jnp.minimum(pos + 1, w).astype(F32)
        pooled = (acc * inv - hp[:, cols]).astype(BF16)
        o_ref[:, cols] = x[:, cols] + jnp.dot(pooled, pw_ref[g], preferred_element_type=F32) * ps_ref[:, cols]
    st_ref[...] = ext_ref[tm + HIST - POOL_BUF:tm + HIST, :]
    ext_ref[0:HIST, :] = ext_ref[tm:tm + HIST, :]


def _pool_prompt(x, nw, pw, ps, B, L):
    tm = min(L, 512)
    assert L % tm == 0 and tm >= HIST
    tiles_per_seq = L // tm
    return pl.pallas_call(
        functools.partial(_pool_prompt_kernel, tm=tm, tiles_per_seq=tiles_per_seq),
        grid=(B * tiles_per_seq,),
        in_specs=[pl.BlockSpec((tm, D_MODEL), lambda i: (i, 0)),
                  pl.BlockSpec((1, D_MODEL), lambda i: (0, 0)),
                  pl.BlockSpec(pw.shape, lambda i: (0, 0, 0)),
                  pl.BlockSpec((1, D_MODEL), lambda i: (0, 0))],
        out_specs=[pl.BlockSpec((tm, D_MODEL), lambda i: (i, 0)),
                   pl.BlockSpec((None, POOL_BUF, D_MODEL), lambda i: (i // tiles_per_seq, 0, 0))],
        out_shape=[jax.ShapeDtypeStruct((B * L, D_MODEL), F32), jax.ShapeDtypeStruct((B, POOL_BUF, D_MODEL), F32)],
        scratch_shapes=[pltpu.VMEM((HIST + tm, D_MODEL), F32),
                        pltpu.VMEM((HIST // SUBLANES - 1, HIST + tm, POOL_GROUP), F32)],
        compiler_params=_params(40, 1), name="pool_prompt",
    )(x, nw, pw, ps)


def _pool_sample(x, nw, sp, pw, ps, Bd, Q, pos0):
    G = len(POOL_WINDOWS)
    hist = [pl.BlockSpec((Bd, POOL_GROUP), functools.partial(lambda g, j: (0, j * G + g), j=j)) for j in range(POOL_BUF)]
    col = pl.BlockSpec((Q * Bd, POOL_GROUP), lambda g: (0, g))
    return pl.pallas_call(
        functools.partial(_pool_sample_kernel, Bd=Bd, Q=Q, pos0=pos0),
        grid=(G,),
        in_specs=[pl.BlockSpec((Q * Bd, D_MODEL), lambda g: (0, 0)),
                  pl.BlockSpec((1, D_MODEL), lambda g: (0, 0)),
                  pl.BlockSpec((None, POOL_GROUP, POOL_GROUP), lambda g: (g, 0, 0)),
                  pl.BlockSpec((1, POOL_GROUP), lambda g: (0, g))] + hist,
        out_specs=[col, col],
        out_shape=[jax.ShapeDtypeStruct((Q * Bd, D_MODEL), F32)] * 2,
        scratch_shapes=[pltpu.VMEM((Q * Bd, 1), F32)],
        compiler_params=_params(40, 1), name="pool_sample",
    )(x, nw, pw, ps, *([sp] * POOL_BUF))


def _pool_sample_kernel(x_ref, nw_ref, pw_ref, ps_ref, *rest, Bd, Q, pos0):
    hist_refs = rest[:POOL_BUF]
    o_ref, hs_ref, inv_ref = rest[POOL_BUF:]
    g = pl.program_id(0)

    @pl.when(g == 0)
    def _():
        x = x_ref[...]
        inv_ref[...] = lax.rsqrt(jnp.mean(x * x, axis=-1, keepdims=True) + RMS_EPS)

    for wi, w in enumerate(POOL_WINDOWS):
        @pl.when(g == wi)
        def _(wi=wi, w=w):
            cols = slice(wi * POOL_GROUP, (wi + 1) * POOL_GROUP)
            xs, hs = [], []
            for q in range(Q):
                rows = slice(q * Bd, (q + 1) * Bd)
                xq = x_ref[rows, cols]
                xs.append(xq)
                hs.append(xq * inv_ref[rows, :] * nw_ref[:, cols])
                hs_ref[rows, :] = hs[q]
            ext = lambda j: hist_refs[j][...] if j < POOL_BUF else hs[j - POOL_BUF]
            for q in range(Q):
                acc = hs[q]
                for k in range(1, w):
                    acc = acc + ext(POOL_BUF + q - k)
                cnt = float(min(pos0 + q + 1, w))
                pooled = (acc / cnt - hs[q]).astype(BF16)
                val = jnp.dot(pooled, pw_ref[...], preferred_element_type=F32) * ps_ref[...]
                o_ref[q * Bd:(q + 1) * Bd, :] = xs[q] + val


def kernel(x_prompt, x_sample, cache_k, cache_v, cache_logf, state_s5_re, state_s5_im, state_pool, page_table, norm_mix_w, norm_ffn_w, w_in, b_f, q_norm_w, k_norm_w, s5_lam_re, s5_lam_im, s5_log_dt, s5_b_re, s5_b_im, s5_c_re, s5_c_im, s5_d, w_glu, b_glu, w_out, pool_w, pool_scale, ffn_w1, ffn_w3, ffn_w2):
    B, L, D = x_prompt.shape
    Bd, Q, _ = x_sample.shape
    depth = norm_mix_w.shape[0]
    n_attn, n_phys, page = cache_k.shape[:3]
    n_pages = page_table.shape[1]
    past_len = n_pages * page
    T = 16
    assert D == D_MODEL and L % T == 0

    xp = x_prompt.reshape(B * L, D)
    xs = jnp.swapaxes(x_sample, 0, 1).reshape(Q * Bd, D)
    from_tm = lambda a, *tail: jnp.swapaxes(a.reshape(Q, Bd, *tail), 0, 1)
    pt_flat = page_table.reshape(-1)
    assert (page * N_HEADS) % LANES == 0
    ck_flat = cache_k.reshape(n_attn * n_phys, page, N_HEADS, HEAD_DIM)
    cv_flat = cache_v.reshape(n_attn * n_phys, page, N_HEADS, HEAD_DIM)
    clf_flat = cache_logf.reshape(n_attn * n_phys, page * N_HEADS // LANES, LANES)

    kp, vp, fp, srp, sip, plp = [], [], [], [], [], []
    ksm, vsm, fsm, srs, sis, pls = [], [], [], [], [], []
    for layer in range(depth):
        i = layer // 2
        nw = norm_mix_w[layer].reshape(1, D)
        if layer % 2 == 0:
            w = w_in[i]
            wq = w[:, :D_ATTN].astype(BF16)
            wk = w[:, D_ATTN:2 * D_ATTN].astype(BF16)
            wv = w[:, 2 * D_ATTN:3 * D_ATTN].astype(BF16)
            wf = jnp.pad(w[:, 3 * D_ATTN:3 * D_ATTN + N_HEADS], ((0, 0), (0, LANES - N_HEADS))).astype(BF16)
            wu = w[:, 3 * D_ATTN + N_HEADS:].astype(BF16)
            bf = jnp.pad(b_f[i], (0, LANES - N_HEADS)).reshape(1, LANES)
            qg = q_norm_w[i].reshape(1, HEAD_DIM)
            kg = k_norm_w[i].reshape(1, HEAD_DIM)
            wg = w_glu[i].astype(BF16)
            bg = b_glu[i].reshape(1, D_SSM)
            wa = w_out[i][:D_ATTN].astype(BF16)
            ws = w_out[i][D_ATTN:].astype(BF16)
            lam_re = s5_lam_re[i].reshape(1, -1)
            lam_im = s5_lam_im[i].reshape(1, -1)
            log_dt = jnp.broadcast_to(s5_log_dt[i][:, None], (N_SSM_GROUPS, SSM_STATE)).reshape(1, -1)
            bd_re = jnp.swapaxes(s5_b_re[i], 1, 2)
            bd_im = jnp.swapaxes(s5_b_im[i], 1, 2)
            cd_re = jnp.swapaxes(s5_c_re[i], 1, 2)
            cd_im = jnp.swapaxes(s5_c_im[i], 1, 2)
            d_row = s5_d[i].reshape(1, D_SSM)

            qb, k, kb, v, vb, u, lf, c = _inproj(xp, nw, wq, wk, wv, wu, wf, bf, qg, kg, seq_len=L)
            c_t = jnp.swapaxes(c[:, :N_HEADS].reshape(B, L, N_HEADS), 1, 2)
            att = _attn_prompt(qb, kb, vb, c_t[..., None], c_t[:, :, None, :], B, L)
            zeros = jnp.zeros((B, 1, N_SLABS * SLAB_STATES), F32)
            y, h_re, h_im = _s5(u, zeros, zeros, lam_re, lam_im, log_dt, bd_re, bd_im, cd_re, cd_im, d_row,
                                n_seq=B, T=T, NC=L // T, chained=True)
            xp = _mixout(xp, att, y, wg, bg, wa, ws)
            kp.append(k.reshape(B, L, N_HEADS, HEAD_DIM))
            vp.append(v.reshape(B, L, N_HEADS, HEAD_DIM))
            fp.append(lf[:, :N_HEADS].reshape(B, L, N_HEADS))
            srp.append(h_re.reshape(B, N_SSM_GROUPS, SSM_STATE))
            sip.append(h_im.reshape(B, N_SSM_GROUPS, SSM_STATE))

            qb, k, kb, v, vb, u, lf = _inproj(xs, nw, wq, wk, wv, wu, wf, bf, qg, kg, seq_len=None)
            lfn = lf[:, :N_HEADS].reshape(Q, Bd, N_HEADS)
            heads = lambda a: a.astype(F32).reshape(Q, Bd, N_HEADS, HEAD_DIM)
            att = _attn_sample(pt_flat, heads(qb), heads(k), heads(v),
                               jnp.swapaxes(lfn, 0, 1).reshape(Bd, Q * N_HEADS, 1),
                               ck_flat, cv_flat, clf_flat, i, n_phys, n_pages, Bd, Q).astype(BF16)
            y, h_re, h_im = _s5(u, state_s5_re[i].reshape(Bd, -1), state_s5_im[i].reshape(Bd, -1),
                                lam_re, lam_im, log_dt, bd_re, bd_im, cd_re, cd_im, d_row,
                                n_seq=1, T=Q, NC=Bd, chained=False)
            xs = _mixout(xs, att.reshape(Q * Bd, D_ATTN), y, wg, bg, wa, ws)
            ksm.append(from_tm(k, N_HEADS, HEAD_DIM))
            vsm.append(from_tm(v, N_HEADS, HEAD_DIM))
            fsm.append(jnp.swapaxes(lfn, 0, 1))
            srs.append(h_re.reshape(Bd, N_SSM_GROUPS, SSM_STATE))
            sis.append(h_im.reshape(Bd, N_SSM_GROUPS, SSM_STATE))
        else:
            pw = pool_w[i].astype(BF16)
            ps = pool_scale[i].reshape(1, D)
            xp, st = _pool_prompt(xp, nw, pw, ps, B, L)
            plp.append(st)
            xs, hs = _pool_sample(xs, nw, state_pool[i].reshape(Bd, POOL_BUF * D), pw, ps, Bd, Q, past_len)
            pls.append(jnp.concatenate([state_pool[i], from_tm(hs, D)], axis=1)[:, -POOL_BUF:])
        nf = norm_ffn_w[layer].reshape(1, D)
        xp = _ffn(xp, nf, ffn_w1, ffn_w3, ffn_w2, layer)
        xs = _ffn(xs, nf, ffn_w1, ffn_w3, ffn_w2, layer)
    return (xp.reshape(B, L, D), from_tm(xs, D),
            jnp.stack(kp), jnp.stack(vp), jnp.stack(fp), jnp.stack(srp), jnp.stack(sip), jnp.stack(plp),
            jnp.stack(ksm), jnp.stack(vsm), jnp.stack(fsm), jnp.stack(srs), jnp.stack(sis), jnp.stack(pls))
```

```python
import functools
import math

import jax
import jax.numpy as jnp
from jax import lax
from jax.experimental import pallas as pl
from jax.experimental.pallas import tpu as pltpu

F32 = jnp.float32
BF16 = jnp.bfloat16

D_MODEL = 2048
D_ATTN = D_MODEL // 2
HEAD_DIM = 128
N_HEADS = D_ATTN // HEAD_DIM
D_SSM = D_MODEL - D_ATTN
SSM_GROUP = 16
N_SSM_GROUPS = D_SSM // SSM_GROUP
SSM_STATE = 64
POOL_WINDOWS = (2, 4, 8, 16)
POOL_GROUP = D_MODEL // len(POOL_WINDOWS)
POOL_BUF = max(POOL_WINDOWS) - 1
RMS_EPS = 1e-6

LANES = 128
MXU_DIM = 256
SLAB = MXU_DIM
SLAB_STATES = (SLAB // SSM_GROUP) * SSM_STATE
N_SLABS = D_SSM // SLAB
VMEM_BYTES = 64 * 1024 * 1024


def _params(vmem_mb, n_axes):
    return pltpu.CompilerParams(dimension_semantics=("arbitrary",) * n_axes,
                                vmem_limit_bytes=min(vmem_mb * 1024 * 1024, VMEM_BYTES - 4 * 1024 * 1024))


def _rms(x, w):
    return x * lax.rsqrt(jnp.mean(x * x, axis=-1, keepdims=True) + RMS_EPS) * w


def _log_sigmoid(x):
    return jnp.minimum(x, 0.0) - jnp.log1p(jnp.exp(-jnp.abs(x)))


def _sigmoid(x):
    return 1.0 / (1.0 + jnp.exp(-x))


def _split3(x):
    hi = x.astype(BF16)
    r = x - hi.astype(F32)
    mid = r.astype(BF16)
    lo = (r - mid.astype(F32)).astype(BF16)
    return hi, mid, lo


def _cumsum_rows(x):
    n = x.shape[0]
    tri = (lax.broadcasted_iota(jnp.int32, (n, n), 1) <= lax.broadcasted_iota(jnp.int32, (n, n), 0)).astype(BF16)
    hi, mid, lo = _split3(x)
    z = jnp.dot(tri, jnp.concatenate([hi, mid, lo], axis=1), preferred_element_type=F32)
    return z[:, :LANES] + z[:, LANES:2 * LANES] + z[:, 2 * LANES:]


def _inproj_kernel(x_ref, nw_ref, wq_ref, wk_ref, wv_ref, wu_ref, wf_ref, bf_ref, qg_ref, kg_ref,
                   q_ref, k_ref, kb_ref, v_ref, vb_ref, u_ref, lf_ref, *rest, tiles_per_seq, scale):
    h = _rms(x_ref[...], nw_ref[...]).astype(BF16)
    for j in range(D_ATTN // MXU_DIM):
        cols = slice(j * MXU_DIM, (j + 1) * MXU_DIM)
        zq = jnp.dot(h, wq_ref[:, cols], preferred_element_type=F32)
        zk = jnp.dot(h, wk_ref[:, cols], preferred_element_type=F32)
        zv = jnp.dot(h, wv_ref[:, cols], preferred_element_type=F32)
        for t in range(MXU_DIM // HEAD_DIM):
            sub = slice(t * HEAD_DIM, (t + 1) * HEAD_DIM)
            dst = slice(j * MXU_DIM + t * HEAD_DIM, j * MXU_DIM + (t + 1) * HEAD_DIM)
            q_ref[:, dst] = (_rms(zq[:, sub], qg_ref[...]) * scale).astype(BF16)
            kn = _rms(zk[:, sub], kg_ref[...])
            k_ref[:, dst] = kn
            kb_ref[:, dst] = kn.astype(BF16)
        v_ref[:, cols] = zv
        vb_ref[:, cols] = zv.astype(BF16)
        u_ref[:, cols] = jnp.dot(h, wu_ref[:, cols], preferred_element_type=F32)
    lf = _log_sigmoid(jnp.dot(h, wf_ref[...], preferred_element_type=F32) + bf_ref[...])
    lf_ref[...] = lf
    if tiles_per_seq is not None:
        c_ref, carry_ref = rest

        @pl.when(pl.program_id(0) % tiles_per_seq == 0)
        def _():
            carry_ref[...] = jnp.zeros_like(carry_ref)

        c = _cumsum_rows(lf) + carry_ref[...]
        c_ref[...] = c
        carry_ref[...] = c[c.shape[0] - 1:, :]


def _inproj(x, nw, wq, wk, wv, wu, wf, bf, qg, kg, seq_len, q_scale):
    R = x.shape[0]
    tm = min(R, 512)
    row = lambda n: pl.BlockSpec((tm, n), lambda i: (i, 0))
    full = lambda a: pl.BlockSpec(a.shape, lambda i: (0,) * a.ndim, pipeline_mode=pl.Buffered(1))
    out_shape = [jax.ShapeDtypeStruct((R, D_ATTN), BF16), jax.ShapeDtypeStruct((R, D_ATTN), F32),
                 jax.ShapeDtypeStruct((R, D_ATTN), BF16), jax.ShapeDtypeStruct((R, D_ATTN), F32),
                 jax.ShapeDtypeStruct((R, D_ATTN), BF16), jax.ShapeDtypeStruct((R, D_SSM), F32),
                 jax.ShapeDtypeStruct((R, LANES), F32)]
    out_specs = [row(D_ATTN)] * 5 + [row(D_SSM), row(LANES)]
    scratch = []
    tiles_per_seq = None
    if seq_len is not None:
        assert seq_len % tm == 0
        tiles_per_seq = seq_len // tm
        out_shape.append(jax.ShapeDtypeStruct((R, LANES), F32))
        out_specs.append(row(LANES))
        scratch.append(pltpu.VMEM((1, LANES), F32))
    return pl.pallas_call(
        functools.partial(_inproj_kernel, tiles_per_seq=tiles_per_seq, scale=q_scale),
        grid=(R // tm,),
        in_specs=[row(D_MODEL), full(nw), full(wq), full(wk), full(wv), full(wu), full(wf), full(bf), full(qg), full(kg)],
        out_specs=out_specs, out_shape=out_shape, scratch_shapes=scratch,
        compiler_params=_params(60, 1), name="inproj",
    )(x, nw, wq, wk, wv, wu, wf, bf, qg, kg)


def _attn_prompt_kernel(q_ref, k_ref, v_ref, c_ref, ck_ref, o_ref, *, L, tq):
    nt = (((1,), (1,)), ((), ()))
    log2e = math.log2(math.e)
    head = lax.broadcasted_iota(jnp.int32, (tq, LANES), 1) == pl.program_id(1)
    for qi in range(L // tq):
        rows = slice(qi * tq, (qi + 1) * tq)
        q = q_ref[rows, :]
        cq = jnp.sum(jnp.where(head, c_ref[rows, :], 0.0), axis=-1, keepdims=True) * log2e
        m = l = acc = None
        for ki in range(qi + 1):
            cols = slice(ki * tq, (ki + 1) * tq)
            t = lax.dot_general(q, k_ref[cols, :], nt, preferred_element_type=F32) - ck_ref[:, cols] * log2e
            if ki == qi:
                keep = lax.broadcasted_iota(jnp.int32, (tq, tq), 1) <= lax.broadcasted_iota(jnp.int32, (tq, tq), 0)
                t = jnp.where(keep, t, -jnp.inf)
            mx = jnp.max(t, axis=-1, keepdims=True) + cq
            if ki == 0:
                m = mx
                p = jnp.exp2(t - (m - cq))
                l = jnp.sum(p, axis=-1, keepdims=True)
                acc = jnp.dot(p.astype(BF16), v_ref[cols, :], preferred_element_type=F32)
            else:
                m_new = jnp.maximum(m, mx)
                alpha = jnp.exp2(m - m_new)
                p = jnp.exp2(t - (m_new - cq))
                l = alpha * l + jnp.sum(p, axis=-1, keepdims=True)
                acc = alpha * acc + jnp.dot(p.astype(BF16), v_ref[cols, :], preferred_element_type=F32)
                m = m_new
        o_ref[rows, :] = (acc / l).astype(BF16)


def _attn_prompt(qb, kb, vb, c, ck, B, L):
    tq = min(L, 512)
    blk = pl.BlockSpec((L, HEAD_DIM), lambda b, h: (b, h))
    return pl.pallas_call(
        functools.partial(_attn_prompt_kernel, L=L, tq=tq),
        grid=(B, N_HEADS),
        in_specs=[blk, blk, blk,
                  pl.BlockSpec((L, LANES), lambda b, h: (b, 0)),
                  pl.BlockSpec((None, None, 1, L), lambda b, h: (b, h, 0, 0))],
        out_specs=blk, out_shape=jax.ShapeDtypeStruct((B * L, D_ATTN), BF16),
        compiler_params=_params(32, 2), name="attn_prompt",
    )(qb, kb, vb, c, ck)


def _attn_sample_kernel(pt_ref, q_ref, kn_ref, vn_ref, lfn_ref, *rest, n_pages, Q):
    k_refs = rest[:n_pages]
    v_refs = rest[n_pages:2 * n_pages]
    lf_refs = rest[2 * n_pages:3 * n_pages]
    o_ref, s_ref, d_ref = rest[3 * n_pages:]
    nt = (((1,), (1,)), ((), ()))
    H = N_HEADS
    M = Q * H
    page = k_refs[0].shape[0]
    PL = page * H
    RPP = PL // LANES

    qm = q_ref[...].reshape(M, HEAD_DIM)
    qmb = qm.astype(BF16)
    same_head = (lax.broadcasted_iota(jnp.int32, (M, LANES), 1) % H
                 == lax.broadcasted_iota(jnp.int32, (M, LANES), 0) % H)

    x = jnp.concatenate([r[...] for r in lf_refs], axis=0)
    lane = lax.broadcasted_iota(jnp.int32, x.shape, 1)
    sh = H
    while sh < LANES:
        x = x + jnp.where(lane >= sh, pltpu.roll(x, sh, axis=1), 0.0)
        sh *= 2
    tot = jnp.where(lane >= LANES - H, x, 0.0)
    inc = _cumsum_rows(tot)
    after = inc[inc.shape[0] - 1:, :] - inc
    sh = H
    while sh < LANES:
        after = after + pltpu.roll(after, sh, axis=1)
        sh *= 2
    row_tot = tot
    sh = H
    while sh < LANES:
        row_tot = row_tot + pltpu.roll(row_tot, sh, axis=1)
        sh *= 2
    d_ref[...] = after + (row_tot - x)

    lfn = lfn_ref[...]
    cum = [lfn[0:H, :]]
    for i in range(1, Q):
        cum.append(cum[-1] + lfn[i * H:(i + 1) * H, :])
    cum_q = jnp.concatenate(cum, axis=0)
    cum_k = jnp.concatenate([jnp.concatenate([c] * Q, axis=0) for c in cum], axis=1)

    qm_r = qmb.astype(F32)
    tile_q = lambda a: jnp.concatenate([a.astype(BF16).astype(F32)] * Q, axis=0)
    s_new = jnp.concatenate(
        [jnp.sum(qm_r * tile_q(kn_ref[i]), axis=-1, keepdims=True) for i in range(Q)], axis=1)
    s_new = s_new + (cum_q - cum_k)
    q_of_row = lax.broadcasted_iota(jnp.int32, (M, Q), 0) // H
    s_new = jnp.where(lax.broadcasted_iota(jnp.int32, (M, Q), 1) <= q_of_row, s_new, -jnp.inf)
    m = jnp.max(s_new, axis=-1, keepdims=True)

    for j in range(n_pages):
        kb = k_refs[j][...].reshape(PL, HEAD_DIM).astype(BF16)
        s = lax.dot_general(qmb, kb, nt, preferred_element_type=F32)
        for r in range(RPP):
            blk = s[:, r * LANES:(r + 1) * LANES] + (cum_q + d_ref[j * RPP + r:j * RPP + r + 1, :])
            blk = jnp.where(same_head, blk, -jnp.inf)
            s_ref[:, (j * RPP + r) * LANES:(j * RPP + r + 1) * LANES] = blk
            m = jnp.maximum(m, jnp.max(blk, axis=-1, keepdims=True))

    p_new = jnp.exp(s_new - m)
    l = jnp.sum(p_new, axis=-1, keepdims=True)
    p_new_r = p_new.astype(BF16).astype(F32)
    acc = p_new_r[:, 0:1] * tile_q(vn_ref[0])
    for i in range(1, Q):
        acc = acc + p_new_r[:, i:i + 1] * tile_q(vn_ref[i])
    for j in range(n_pages):
        p = jnp.exp(s_ref[:, j * PL:(j + 1) * PL] - m)
        l = l + jnp.sum(p, axis=-1, keepdims=True)
        vb = v_refs[j][...].reshape(PL, HEAD_DIM).astype(BF16)
        acc = acc + jnp.dot(p.astype(BF16), vb, preferred_element_type=F32)
    o_ref[...] = (acc / l).reshape(Q, H, HEAD_DIM)


def _attn_sample(page_table_flat, q4, k4, v4, lfn_col, cache_k, cache_v, cache_lf, layer, n_phys, n_pages, Bd, Q):
    page = cache_k.shape[1]
    base = layer * n_phys
    rpp = page * N_HEADS // LANES

    def paged(tail):
        return [pl.BlockSpec((None,) + tail, functools.partial(
            lambda b, pt, j: (pt[b * n_pages + j] + base,) + (0,) * len(tail), j=j)) for j in range(n_pages)]

    tok = pl.BlockSpec((Q, None, N_HEADS, HEAD_DIM), lambda b, pt: (0, b, 0, 0))
    grid_spec = pltpu.PrefetchScalarGridSpec(
        num_scalar_prefetch=1, grid=(Bd,),
        in_specs=[tok, tok, tok, pl.BlockSpec((None, Q * N_HEADS, 1), lambda b, pt: (b, 0, 0))]
        + paged((page, N_HEADS, HEAD_DIM)) + paged((page, N_HEADS, HEAD_DIM)) + paged((rpp, LANES)),
        out_specs=tok,
        scratch_shapes=[pltpu.VMEM((Q * N_HEADS, n_pages * page * N_HEADS), F32),
                        pltpu.VMEM((n_pages * rpp, LANES), F32)])
    return pl.pallas_call(
        functools.partial(_attn_sample_kernel, n_pages=n_pages, Q=Q),
        grid_spec=grid_spec, out_shape=jax.ShapeDtypeStruct((Q, Bd, N_HEADS, HEAD_DIM), F32),
        compiler_params=_params(56, 1), name="attn_sample",
    )(page_table_flat, q4, k4, v4, lfn_col, *([cache_k] * n_pages), *([cache_v] * n_pages), *([cache_lf] * n_pages))


def _s5_kernel(u_ref, h0r_ref, h0i_ref, lr_ref, li_ref, ldt_ref, btr_ref, bti_ref, ctr_ref, cti_ref, d_ref,
               y_ref, fr_ref, fi_ref, xr_ref, xi_ref, bdr_ref, bdi_ref, cdr_ref, cdi_ref, *scratch, T, NC, chained):
    @pl.when((pl.program_id(0) == 0) & (pl.program_id(1) == 0))
    def _():
        for ref in (bdr_ref, bdi_ref, cdr_ref, cdi_ref):
            ref[...] = jnp.zeros_like(ref)

    for g in range(SLAB // SSM_GROUP):
        rows_c, rows_p = slice(g * SSM_GROUP, (g + 1) * SSM_GROUP), slice(g * SSM_STATE, (g + 1) * SSM_STATE)
        bdr_ref[rows_c, rows_p] = btr_ref[g]
        bdi_ref[rows_c, rows_p] = bti_ref[g]
        cdr_ref[rows_p, rows_c] = ctr_ref[g]
        cdi_ref[rows_p, rows_c] = cti_ref[g]

    lr, li = lr_ref[...], li_ref[...]
    dt = jnp.exp(ldt_ref[...])
    mag = jnp.exp(lr * dt)
    ar, ai = mag * jnp.cos(li * dt), mag * jnp.sin(li * dt)
    den = lr * lr + li * li
    zr = ((ar - 1.0) * lr + ai * li) / den
    zi = (ai * lr - (ar - 1.0) * li) / den
    bdr, bdi = bdr_ref[...], bdi_ref[...]
    pr = (zr * bdr - zi * bdi).astype(BF16)
    pi = (zr * bdi + zi * bdr).astype(BF16)
    n_cb = SLAB_STATES // LANES
    per = MXU_DIM // LANES
    if chained:
        stage_ref, perm_ref = scratch[4:]
        for t in range(per):
            stage_ref[t] = u_ref[:, t * LANES:(t + 1) * LANES]
        for s in range(T):
            for t in range(per):
                perm_ref[s * NC:(s + 1) * NC, t * LANES:(t + 1) * LANES] = stage_ref[t, pl.ds(s, NC, stride=T), :]
        u = perm_ref[...]
    else:
        u = u_ref[...]
    ub = u.astype(BF16)
    for c2 in range(n_cb // per):
        cols = slice(c2 * MXU_DIM, (c2 + 1) * MXU_DIM)
        x_r = jnp.dot(ub, pr[:, cols], preferred_element_type=F32)
        x_i = jnp.dot(ub, pi[:, cols], preferred_element_type=F32)
        for t in range(per):
            xr_ref[c2 * per + t] = x_r[:, t * LANES:(t + 1) * LANES]
            xi_ref[c2 * per + t] = x_i[:, t * LANES:(t + 1) * LANES]

    def sweep(cb, hr, hi, store):
        cols = slice(cb * LANES, (cb + 1) * LANES)
        a_r, a_i = ar[:, cols], ai[:, cols]
        for s in range(T):
            rows = slice(s * NC, (s + 1) * NC)
            xr, xi = xr_ref[cb, rows, :], xi_ref[cb, rows, :]
            hr, hi = a_r * hr - a_i * hi + xr, a_r * hi + a_i * hr + xi
            if store:
                xr_ref[cb, rows, :] = hr
                xi_ref[cb, rows, :] = hi
        return hr, hi

    if chained:
        er_ref, ei_ref, gr_ref, gi_ref = scratch[:4]
        zero = jnp.zeros((NC, LANES), F32)
        for cb in range(n_cb):
            cols = slice(cb * LANES, (cb + 1) * LANES)
            er_ref[:, cols], ei_ref[:, cols] = sweep(cb, zero, zero, False)
        tr, ti = ar, ai
        for _ in range(T - 1):
            tr, ti = tr * ar - ti * ai, tr * ai + ti * ar

        def chunk_step(k, carry):
            sr, si = carry
            gr_ref[pl.ds(k, 1), :] = sr
            gi_ref[pl.ds(k, 1), :] = si
            er, ei = er_ref[pl.ds(k, 1), :], ei_ref[pl.ds(k, 1), :]
            return tr * sr - ti * si + er, tr * si + ti * sr + ei

        sr, si = lax.fori_loop(0, NC, chunk_step, (h0r_ref[...], h0i_ref[...]))
        fr_ref[...] = sr
        fi_ref[...] = si
        for cb in range(n_cb):
            cols = slice(cb * LANES, (cb + 1) * LANES)
            sweep(cb, gr_ref[:, cols], gi_ref[:, cols], True)
    else:
        for cb in range(n_cb):
            cols = slice(cb * LANES, (cb + 1) * LANES)
            hr, hi = sweep(cb, h0r_ref[:, cols], h0i_ref[:, cols], True)
            fr_ref[:, cols] = hr
            fi_ref[:, cols] = hi

    y = d_ref[...] * u
    for c2 in range(n_cb // per):
        rows = slice(c2 * MXU_DIM, (c2 + 1) * MXU_DIM)
        h_r = jnp.concatenate([xr_ref[c2 * per + t] for t in range(per)], axis=1).astype(BF16)
        h_i = jnp.concatenate([xi_ref[c2 * per + t] for t in range(per)], axis=1).astype(BF16)
        y = y + (jnp.dot(h_r, cdr_ref[rows, :].astype(BF16), preferred_element_type=F32)
                 - jnp.dot(h_i, cdi_ref[rows, :].astype(BF16), preferred_element_type=F32))
    if chained:
        for s in range(T):
            for t in range(per):
                stage_ref[t, pl.ds(s, NC, stride=T), :] = y[s * NC:(s + 1) * NC, t * LANES:(t + 1) * LANES]
        for t in range(per):
            y_ref[:, t * LANES:(t + 1) * LANES] = stage_ref[t]
    else:
        y_ref[...] = y


def _s5(u, h0r, h0i, lam_re, lam_im, log_dt, bt_re, bt_im, ct_re, ct_im, d_row, n_seq, T, NC, chained):
    rows = NC * T
    gps = SLAB // SSM_GROUP
    slab_row = lambda: pl.BlockSpec((1, SLAB_STATES), lambda b, s: (0, s))
    if chained:
        st_spec = pl.BlockSpec((None, 1, SLAB_STATES), lambda b, s: (b, 0, s))
        st_shape = jax.ShapeDtypeStruct((n_seq, 1, N_SLABS * SLAB_STATES), F32)
        scratch = [pltpu.VMEM((NC, SLAB_STATES), F32)] * 4 + [
            pltpu.VMEM((SLAB // LANES, rows, LANES), F32), pltpu.VMEM((rows, SLAB), F32)]
    else:
        st_spec = pl.BlockSpec((NC, SLAB_STATES), lambda b, s: (0, s))
        st_shape = jax.ShapeDtypeStruct((NC, N_SLABS * SLAB_STATES), F32)
        scratch = []
    return pl.pallas_call(
        functools.partial(_s5_kernel, T=T, NC=NC, chained=chained),
        grid=(n_seq, N_SLABS),
        in_specs=[pl.BlockSpec((rows, SLAB), lambda b, s: (b, s)), st_spec, st_spec,
                  slab_row(), slab_row(), slab_row(),
                  pl.BlockSpec((gps, SSM_GROUP, SSM_STATE), lambda b, s: (s, 0, 0)),
                  pl.BlockSpec((gps, SSM_GROUP, SSM_STATE), lambda b, s: (s, 0, 0)),
                  pl.BlockSpec((gps, SSM_STATE, SSM_GROUP), lambda b, s: (s, 0, 0)),
                  pl.BlockSpec((gps, SSM_STATE, SSM_GROUP), lambda b, s: (s, 0, 0)),
                  pl.BlockSpec((1, SLAB), lambda b, s: (0, s))],
        out_specs=[pl.BlockSpec((rows, SLAB), lambda b, s: (b, s)), st_spec, st_spec],
        out_shape=[jax.ShapeDtypeStruct((n_seq * rows, D_SSM), F32), st_shape, st_shape],
        scratch_shapes=[pltpu.VMEM((SLAB_STATES // LANES, rows, LANES), F32)] * 2
        + [pltpu.VMEM((SLAB, SLAB_STATES), F32)] * 2 + [pltpu.VMEM((SLAB_STATES, SLAB), F32)] * 2 + scratch,
        compiler_params=_params(48, 2), name="s5",
    )(u, h0r, h0i, lam_re, lam_im, log_dt, bt_re, bt_im, ct_re, ct_im, d_row)


def _mixout_kernel(x_ref, att_ref, y_ref, wg_ref, bg_ref, wo_ref, o_ref):
    y = y_ref[...]
    z = 0.5 * y * (1.0 + jnp.tanh(math.sqrt(2.0 / math.pi) * (y + 0.044715 * (y * y * y))))
    gate = _sigmoid(jnp.dot(z.astype(BF16), wg_ref[...], preferred_element_type=F32) + bg_ref[...])
    ssm = (z * gate).astype(BF16)
    o_ref[...] = (x_ref[...] + jnp.dot(att_ref[...], wo_ref[0:D_ATTN, :], preferred_element_type=F32)
                  + jnp.dot(ssm, wo_ref[D_ATTN:, :], preferred_element_type=F32))


def _mixout(x, att, y, wg, bg, wo):
    R = x.shape[0]
    tm = min(R, 512)
    row = lambda n: pl.BlockSpec((tm, n), lambda i: (i, 0))
    full = lambda a: pl.BlockSpec(a.shape, lambda i: (0,) * a.ndim, pipeline_mode=pl.Buffered(1))
    return pl.pallas_call(
        _mixout_kernel, grid=(R // tm,),
        in_specs=[row(D_MODEL), row(D_ATTN), row(D_SSM), full(wg), full(bg), full(wo)],
        out_specs=row(D_MODEL), out_shape=jax.ShapeDtypeStruct((R, D_MODEL), F32),
        compiler_params=_params(56, 1), name="mixout",
    )(x, att, y, wg, bg, wo)


def _ffn_kernel(x_ref, nw_ref, w1_ref, w3_ref, w2_ref, o_ref, h_ref):
    @pl.when(pl.program_id(1) == 0)
    def _():
        x = x_ref[...]
        h_ref[...] = _rms(x, nw_ref[...]).astype(BF16)
        o_ref[...] = x

    h = h_ref[...]
    for c in range(w1_ref.shape[1] // MXU_DIM):
        cols = slice(c * MXU_DIM, (c + 1) * MXU_DIM)
        a = jnp.dot(h, w1_ref[:, cols].astype(BF16), preferred_element_type=F32)
        b = jnp.dot(h, w3_ref[:, cols].astype(BF16), preferred_element_type=F32)
        g = (a * _sigmoid(a) * b).astype(BF16)
        o_ref[...] += jnp.dot(g, w2_ref[cols, :].astype(BF16), preferred_element_type=F32)


def _ffn(x, nw, w1, w3, w2, layer):
    R = x.shape[0]
    d_ff = w1.shape[2]
    tm = min(R, 1024)
    tf = 2 * MXU_DIM
    assert d_ff % tf == 0 and R % tm == 0
    return pl.pallas_call(
        _ffn_kernel, grid=(R // tm, d_ff // tf),
        in_specs=[pl.BlockSpec((tm, D_MODEL), lambda i, f: (i, 0), pipeline_mode=pl.Buffered(1)),
                  pl.BlockSpec((1, D_MODEL), lambda i, f: (0, 0)),
                  pl.BlockSpec((None, D_MODEL, tf), lambda i, f: (layer, 0, f)),
                  pl.BlockSpec((None, D_MODEL, tf), lambda i, f: (layer, 0, f)),
                  pl.BlockSpec((None, tf, D_MODEL), lambda i, f: (layer, f, 0))],
        out_specs=pl.BlockSpec((tm, D_MODEL), lambda i, f: (i, 0)),
        out_shape=jax.ShapeDtypeStruct((R, D_MODEL), F32),
        scratch_shapes=[pltpu.VMEM((tm, D_MODEL), BF16)],
        compiler_params=_params(60, 2), name="ffn",
    )(x, nw, w1, w3, w2)


SUBLANES = 8
HIST = SUBLANES * (max(POOL_WINDOWS).bit_length() - 1)


def _pool_prompt_kernel(x_ref, nw_ref, pw_ref, ps_ref, o_ref, st_ref, ext_ref, lvl_ref, *, tm, tiles_per_seq):
    i = pl.program_id(0)
    t_in_seq = i % tiles_per_seq
    n = HIST + tm

    @pl.when(t_in_seq == 0)
    def _():
        ext_ref[0:HIST, :] = jnp.zeros((HIST, D_MODEL), F32)

    x = x_ref[...]
    hp = _rms(x, nw_ref[...])
    ext_ref[HIST:n, :] = hp
    pos = t_in_seq * tm + lax.broadcasted_iota(jnp.int32, (tm, 1), 0)
    for g, w in enumerate(POOL_WINDOWS):
        cols = slice(g * POOL_GROUP, (g + 1) * POOL_GROUP)
        n_levels = w.bit_length() - 1
        assert w == 1 << n_levels and SUBLANES * n_levels <= HIST
        acc = None
        for k in range(n_levels):
            start, sh = SUBLANES * (k + 1), 1 << k
            if k == 0:
                acc = ext_ref[start:n, cols] + ext_ref[start - sh:n - sh, cols]
            else:
                acc = lvl_ref[k - 1, start:n, :] + lvl_ref[k - 1, start - sh:n - sh, :]
            if k < n_levels - 1:
                lvl_ref[k, start:n, :] = acc
        acc = acc[HIST - SUBLANES * n_levels:, :]
        inv = 1.0 / jnp.minimum(pos + 1, w).astype(F32)
        pooled = (acc * inv - hp[:, cols]).astype(BF16)
        o_ref[:, cols] = x[:, cols] + jnp.dot(pooled, pw_ref[g], preferred_element_type=F32) * ps_ref[:, cols]
    st_ref[...] = ext_ref[tm + HIST - POOL_BUF:tm + HIST, :]
    ext_ref[0:HIST, :] = ext_ref[tm:tm + HIST, :]


def _pool_prompt(x, nw, pw, ps, B, L):
    tm = min(L, 512)
    assert L % tm == 0 and tm >= HIST
    tiles_per_seq = L // tm
    return pl.pallas_call(
        functools.partial(_pool_prompt_kernel, tm=tm, tiles_per_seq=tiles_per_seq),
        grid=(B * tiles_per_seq,),
        in_specs=[pl.BlockSpec((tm, D_MODEL), lambda i: (i, 0)),
                  pl.BlockSpec((1, D_MODEL), lambda i: (0, 0)),
                  pl.BlockSpec(pw.shape, lambda i: (0, 0, 0)),
                  pl.BlockSpec((1, D_MODEL), lambda i: (0, 0))],
        out_specs=[pl.BlockSpec((tm, D_MODEL), lambda i: (i, 0)),
                   pl.BlockSpec((None, POOL_BUF, D_MODEL), lambda i: (i // tiles_per_seq, 0, 0))],
        out_shape=[jax.ShapeDtypeStruct((B * L, D_MODEL), F32), jax.ShapeDtypeStruct((B, POOL_BUF, D_MODEL), F32)],
        scratch_shapes=[pltpu.VMEM((HIST + tm, D_MODEL), F32),
                        pltpu.VMEM((HIST // SUBLANES - 1, HIST + tm, POOL_GROUP), F32)],
        compiler_params=_params(40, 1), name="pool_prompt",
    )(x, nw, pw, ps)


def _pool_sample(x, nw, sp, pw, ps, Bd, Q, pos0):
    G = len(POOL_WINDOWS)
    hist = [pl.BlockSpec((Bd, POOL_GROUP), functools.partial(lambda g, j: (0, j * G + g), j=j)) for j in range(POOL_BUF)]
    col = pl.BlockSpec((Q * Bd, POOL_GROUP), lambda g: (0, g))
    return pl.pallas_call(
        functools.partial(_pool_sample_kernel, Bd=Bd, Q=Q, pos0=pos0),
        grid=(G,),
        in_specs=[pl.BlockSpec((Q * Bd, D_MODEL), lambda g: (0, 0)),
                  pl.BlockSpec((1, D_MODEL), lambda g: (0, 0)),
                  pl.BlockSpec((None, POOL_GROUP, POOL_GROUP), lambda g: (g, 0, 0)),
                  pl.BlockSpec((1, POOL_GROUP), lambda g: (0, g))] + hist,
        out_specs=[col, col],
        out_shape=[jax.ShapeDtypeStruct((Q * Bd, D_MODEL), F32)] * 2,
        scratch_shapes=[pltpu.VMEM((Q * Bd, 1), F32)],
        compiler_params=_params(40, 1), name="pool_sample",
    )(x, nw, pw, ps, *([sp] * POOL_BUF))


def _pool_sample_kernel(x_ref, nw_ref, pw_ref, ps_ref, *rest, Bd, Q, pos0):
    hist_refs = rest[:POOL_BUF]
    o_ref, hs_ref, inv_ref = rest[POOL_BUF:]
    g = pl.program_id(0)

    @pl.when(g == 0)
    def _():
        x = x_ref[...]
        inv_ref[...] = lax.rsqrt(jnp.mean(x * x, axis=-1, keepdims=True) + RMS_EPS)

    for wi, w in enumerate(POOL_WINDOWS):
        @pl.when(g == wi)
        def _(wi=wi, w=w):
            cols = slice(wi * POOL_GROUP, (wi + 1) * POOL_GROUP)
            xs, hs = [], []
            for q in range(Q):
                rows = slice(q * Bd, (q + 1) * Bd)
                xq = x_ref[rows, cols]
                xs.append(xq)
                hs.append(xq * inv_ref[rows, :] * nw_ref[:, cols])
                hs_ref[rows, :] = hs[q]
            ext = lambda j: hist_refs[j][...] if j < POOL_BUF else hs[j - POOL_BUF]
            for q in range(Q):
                acc = hs[q]
                for k in range(1, w):
                    acc = acc + ext(POOL_BUF + q - k)
                cnt = float(min(pos0 + q + 1, w))
                pooled = (acc / cnt - hs[q]).astype(BF16)
                val = jnp.dot(pooled, pw_ref[...], preferred_element_type=F32) * ps_ref[...]
                o_ref[q * Bd:(q + 1) * Bd, :] = xs[q] + val


def kernel(x_prompt, x_sample, cache_k, cache_v, cache_logf, state_s5_re, state_s5_im, state_pool, page_table, norm_mix_w, norm_ffn_w, w_in, b_f, q_norm_w, k_norm_w, s5_lam_re, s5_lam_im, s5_log_dt, s5_b_re, s5_b_im, s5_c_re, s5_c_im, s5_d, w_glu, b_glu, w_out, pool_w, pool_scale, ffn_w1, ffn_w3, ffn_w2):
    B, L, D = x_prompt.shape
    Bd, Q, _ = x_sample.shape
    depth = norm_mix_w.shape[0]
    n_attn, n_phys, page = cache_k.shape[:3]
    n_pages = page_table.shape[1]
    past_len = n_pages * page
    T = 16
    assert D == D_MODEL and L % T == 0

    xp = x_prompt.reshape(B * L, D)
    xs = jnp.swapaxes(x_sample, 0, 1).reshape(Q * Bd, D)
    from_tm = lambda a, *tail: jnp.swapaxes(a.reshape(Q, Bd, *tail), 0, 1)
    pt_flat = page_table.reshape(-1)
    assert (page * N_HEADS) % LANES == 0
    ck_flat = cache_k.reshape(n_attn * n_phys, page, N_HEADS, HEAD_DIM)
    cv_flat = cache_v.reshape(n_attn * n_phys, page, N_HEADS, HEAD_DIM)
    clf_flat = cache_logf.reshape(n_attn * n_phys, page * N_HEADS // LANES, LANES)

    kp, vp, fp, srp, sip, plp = [], [], [], [], [], []
    ksm, vsm, fsm, srs, sis, pls = [], [], [], [], [], []
    for layer in range(depth):
        i = layer // 2
        nw = norm_mix_w[layer].reshape(1, D)
        if layer % 2 == 0:
            w = w_in[i]
            wq = w[:, :D_ATTN].astype(BF16)
            wk = w[:, D_ATTN:2 * D_ATTN].astype(BF16)
            wv = w[:, 2 * D_ATTN:3 * D_ATTN].astype(BF16)
            wf = jnp.pad(w[:, 3 * D_ATTN:3 * D_ATTN + N_HEADS], ((0, 0), (0, LANES - N_HEADS))).astype(BF16)
            wu = w[:, 3 * D_ATTN + N_HEADS:].astype(BF16)
            bf = jnp.pad(b_f[i], (0, LANES - N_HEADS)).reshape(1, LANES)
            qg = q_norm_w[i].reshape(1, HEAD_DIM)
            kg = k_norm_w[i].reshape(1, HEAD_DIM)
            wg = w_glu[i].astype(BF16)
            bg = b_glu[i].reshape(1, D_SSM)
            wo = w_out[i].astype(BF16)
            lam_re = s5_lam_re[i].reshape(1, -1)
            lam_im = s5_lam_im[i].reshape(1, -1)
            log_dt = jnp.broadcast_to(s5_log_dt[i][:, None], (N_SSM_GROUPS, SSM_STATE)).reshape(1, -1)
            bd_re = jnp.swapaxes(s5_b_re[i], 1, 2)
            bd_im = jnp.swapaxes(s5_b_im[i], 1, 2)
            cd_re = jnp.swapaxes(s5_c_re[i], 1, 2)
            cd_im = jnp.swapaxes(s5_c_im[i], 1, 2)
            d_row = s5_d[i].reshape(1, D_SSM)

            qb, k, kb, v, vb, u, lf, c = _inproj(xp, nw, wq, wk, wv, wu, wf, bf, qg, kg, seq_len=L,
                                                 q_scale=HEAD_DIM ** -0.5 * math.log2(math.e))
            c_t = jnp.swapaxes(c[:, :N_HEADS].reshape(B, L, N_HEADS), 1, 2)
            att = _attn_prompt(qb, kb, vb, c, c_t[:, :, None, :], B, L)
            zeros = jnp.zeros((B, 1, N_SLABS * SLAB_STATES), F32)
            y, h_re, h_im = _s5(u, zeros, zeros, lam_re, lam_im, log_dt, bd_re, bd_im, cd_re, cd_im, d_row,
                                n_seq=B, T=T, NC=L // T, chained=True)
            xp = _mixout(xp, att, y, wg, bg, wo)
            kp.append(k.reshape(B, L, N_HEADS, HEAD_DIM))
            vp.append(v.reshape(B, L, N_HEADS, HEAD_DIM))
            fp.append(lf[:, :N_HEADS].reshape(B, L, N_HEADS))
            srp.append(h_re.reshape(B, N_SSM_GROUPS, SSM_STATE))
            sip.append(h_im.reshape(B, N_SSM_GROUPS, SSM_STATE))

            qb, k, kb, v, vb, u, lf = _inproj(xs, nw, wq, wk, wv, wu, wf, bf, qg, kg, seq_len=None,
                                              q_scale=HEAD_DIM ** -0.5)
            lfn = lf[:, :N_HEADS].reshape(Q, Bd, N_HEADS)
            heads = lambda a: a.astype(F32).reshape(Q, Bd, N_HEADS, HEAD_DIM)
            att = _attn_sample(pt_flat, heads(qb), heads(k), heads(v),
                               jnp.swapaxes(lfn, 0, 1).reshape(Bd, Q * N_HEADS, 1),
                               ck_flat, cv_flat, clf_flat, i, n_phys, n_pages, Bd, Q).astype(BF16)
            y, h_re, h_im = _s5(u, state_s5_re[i].reshape(Bd, -1), state_s5_im[i].reshape(Bd, -1),
                                lam_re, lam_im, log_dt, bd_re, bd_im, cd_re, cd_im, d_row,
                                n_seq=1, T=Q, NC=Bd, chained=False)
            xs = _mixout(xs, att.reshape(Q * Bd, D_ATTN), y, wg, bg, wo)
            ksm.append(from_tm(k, N_HEADS, HEAD_DIM))
            vsm.append(from_tm(v, N_HEADS, HEAD_DIM))
            fsm.append(jnp.swapaxes(lfn, 0, 1))
            srs.append(h_re.reshape(Bd, N_SSM_GROUPS, SSM_STATE))
            sis.append(h_im.reshape(Bd, N_SSM_GROUPS, SSM_STATE))
        else:
            pw = pool_w[i].astype(BF16)
            ps = pool_scale[i].reshape(1, D)
            xp, st = _pool_prompt(xp, nw, pw, ps, B, L)
            plp.append(st)
            xs, hs = _pool_sample(xs, nw, state_pool[i].reshape(Bd, POOL_BUF * D), pw, ps, Bd, Q, past_len)
            pls.append(jnp.concatenate([state_pool[i], from_tm(hs, D)], axis=1)[:, -POOL_BUF:])
        nf = norm_ffn_w[layer].reshape(1, D)
        xp = _ffn(xp, nf, ffn_w1, ffn_w3, ffn_w2, layer)
        xs = _ffn(xs, nf, ffn_w1, ffn_w3, ffn_w2, layer)
    return (xp.reshape(B, L, D), from_tm(xs, D),
            jnp.stack(kp), jnp.stack(vp), jnp.stack(fp), jnp.stack(srp), jnp.stack(sip), jnp.stack(plp),
            jnp.stack(ksm), jnp.stack(vsm), jnp.stack(fsm), jnp.stack(srs), jnp.stack(sis), jnp.stack(pls))
```

```python
import functools
import math

import jax
import jax.numpy as jnp
from jax import lax
from jax.experimental import pallas as pl
from jax.experimental.pallas import tpu as pltpu

F32 = jnp.float32
BF16 = jnp.bfloat16

D_MODEL = 2048
D_ATTN = D_MODEL // 2
HEAD_DIM = 128
N_HEADS = D_ATTN // HEAD_DIM
D_SSM = D_MODEL - D_ATTN
SSM_GROUP = 16
N_SSM_GROUPS = D_SSM // SSM_GROUP
SSM_STATE = 64
POOL_WINDOWS = (2, 4, 8, 16)
POOL_GROUP = D_MODEL // len(POOL_WINDOWS)
POOL_BUF = max(POOL_WINDOWS) - 1
RMS_EPS = 1e-6

LANES = 128
MXU_DIM = 256
SLAB = MXU_DIM
SLAB_STATES = (SLAB // SSM_GROUP) * SSM_STATE
N_SLABS = D_SSM // SLAB
VMEM_BYTES = 64 * 1024 * 1024


def _params(vmem_mb, n_axes):
    return pltpu.CompilerParams(dimension_semantics=("arbitrary",) * n_axes,
                                vmem_limit_bytes=min(vmem_mb * 1024 * 1024, VMEM_BYTES - 4 * 1024 * 1024))


def _rms(x, w):
    return x * lax.rsqrt(jnp.mean(x * x, axis=-1, keepdims=True) + RMS_EPS) * w


def _log_sigmoid(x):
    return jnp.minimum(x, 0.0) - jnp.log1p(jnp.exp(-jnp.abs(x)))


def _sigmoid(x):
    return 1.0 / (1.0 + jnp.exp(-x))


def _split3(x):
    hi = x.astype(BF16)
    r = x - hi.astype(F32)
    mid = r.astype(BF16)
    lo = (r - mid.astype(F32)).astype(BF16)
    return hi, mid, lo


def _cumsum_rows(x):
    n = x.shape[0]
    tri = (lax.broadcasted_iota(jnp.int32, (n, n), 1) <= lax.broadcasted_iota(jnp.int32, (n, n), 0)).astype(BF16)
    hi, mid, lo = _split3(x)
    z = jnp.dot(tri, jnp.concatenate([hi, mid, lo], axis=1), preferred_element_type=F32)
    return z[:, :LANES] + z[:, LANES:2 * LANES] + z[:, 2 * LANES:]


def _inproj_kernel(x_ref, nw_ref, wq_ref, wk_ref, wv_ref, wu_ref, wf_ref, bf_ref, qg_ref, kg_ref,
                   q_ref, k_ref, kb_ref, v_ref, vb_ref, u_ref, lf_ref, *rest, tiles_per_seq, scale):
    h = _rms(x_ref[...], nw_ref[...]).astype(BF16)
    for j in range(D_ATTN // MXU_DIM):
        cols = slice(j * MXU_DIM, (j + 1) * MXU_DIM)
        zq = jnp.dot(h, wq_ref[:, cols], preferred_element_type=F32)
        zk = jnp.dot(h, wk_ref[:, cols], preferred_element_type=F32)
        zv = jnp.dot(h, wv_ref[:, cols], preferred_element_type=F32)
        for t in range(MXU_DIM // HEAD_DIM):
            sub = slice(t * HEAD_DIM, (t + 1) * HEAD_DIM)
            dst = slice(j * MXU_DIM + t * HEAD_DIM, j * MXU_DIM + (t + 1) * HEAD_DIM)
            q_ref[:, dst] = (_rms(zq[:, sub], qg_ref[...]) * scale).astype(BF16)
            kn = _rms(zk[:, sub], kg_ref[...])
            k_ref[:, dst] = kn
            kb_ref[:, dst] = kn.astype(BF16)
        v_ref[:, cols] = zv
        vb_ref[:, cols] = zv.astype(BF16)
        u_ref[:, cols] = jnp.dot(h, wu_ref[:, cols], preferred_element_type=F32)
    lf = _log_sigmoid(jnp.dot(h, wf_ref[...], preferred_element_type=F32) + bf_ref[...])
    lf_ref[...] = lf
    if tiles_per_seq is not None:
        c_ref, carry_ref = rest

        @pl.when(pl.program_id(0) % tiles_per_seq == 0)
        def _():
            carry_ref[...] = jnp.zeros_like(carry_ref)

        c = _cumsum_rows(lf) + carry_ref[...]
        c_ref[...] = c
        carry_ref[...] = c[c.shape[0] - 1:, :]


def _inproj(x, nw, wq, wk, wv, wu, wf, bf, qg, kg, seq_len, q_scale):
    R = x.shape[0]
    tm = min(R, 512)
    row = lambda n: pl.BlockSpec((tm, n), lambda i: (i, 0))
    full = lambda a: pl.BlockSpec(a.shape, lambda i: (0,) * a.ndim, pipeline_mode=pl.Buffered(1))
    out_shape = [jax.ShapeDtypeStruct((R, D_ATTN), BF16), jax.ShapeDtypeStruct((R, D_ATTN), F32),
                 jax.ShapeDtypeStruct((R, D_ATTN), BF16), jax.ShapeDtypeStruct((R, D_ATTN), F32),
                 jax.ShapeDtypeStruct((R, D_ATTN), BF16), jax.ShapeDtypeStruct((R, D_SSM), F32),
                 jax.ShapeDtypeStruct((R, LANES), F32)]
    out_specs = [row(D_ATTN)] * 5 + [row(D_SSM), row(LANES)]
    scratch = []
    tiles_per_seq = None
    if seq_len is not None:
        assert seq_len % tm == 0
        tiles_per_seq = seq_len // tm
        out_shape.append(jax.ShapeDtypeStruct((R, LANES), F32))
        out_specs.append(row(LANES))
        scratch.append(pltpu.VMEM((1, LANES), F32))
    return pl.pallas_call(
        functools.partial(_inproj_kernel, tiles_per_seq=tiles_per_seq, scale=q_scale),
        grid=(R // tm,),
        in_specs=[row(D_MODEL), full(nw), full(wq), full(wk), full(wv), full(wu), full(wf), full(bf), full(qg), full(kg)],
        out_specs=out_specs, out_shape=out_shape, scratch_shapes=scratch,
        compiler_params=_params(60, 1), name="inproj",
    )(x, nw, wq, wk, wv, wu, wf, bf, qg, kg)


def _prompt_tile(q_ref, k_ref, v_ref, c_ref, ck_ref, o_ref, qi, head, tq):
    nt = (((1,), (1,)), ((), ()))
    log2e = math.log2(math.e)
    q = q_ref[...]
    r0 = pl.multiple_of(qi * tq, tq)
    sel = lax.broadcasted_iota(jnp.int32, (tq, LANES), 1) == head
    cq = jnp.sum(jnp.where(sel, c_ref[pl.ds(r0, tq), :], 0.0), axis=-1, keepdims=True) * log2e

    def tile(ki, diagonal, state):
        m, l, acc = state
        k0 = pl.multiple_of(ki * tq, tq)
        t = lax.dot_general(q, k_ref[pl.ds(k0, tq), :], nt, preferred_element_type=F32) - ck_ref[ki] * log2e
        if diagonal:
            keep = lax.broadcasted_iota(jnp.int32, (tq, tq), 1) <= lax.broadcasted_iota(jnp.int32, (tq, tq), 0)
            t = jnp.where(keep, t, -jnp.inf)
        m_new = jnp.maximum(m, jnp.max(t, axis=-1, keepdims=True) + cq)
        alpha = jnp.exp2(m - m_new)
        p = jnp.exp2(t - (m_new - cq))
        l = alpha * l + jnp.sum(p, axis=-1, keepdims=True)
        acc = alpha * acc + jnp.dot(p.astype(BF16), v_ref[pl.ds(k0, tq), :], preferred_element_type=F32)
        return m_new, l, acc

    state = (jnp.full((tq, 1), -jnp.inf, F32), jnp.zeros((tq, 1), F32), jnp.zeros((tq, HEAD_DIM), F32))
    state = lax.fori_loop(0, qi, lambda ki, st: tile(ki, False, st), state)
    _, l, acc = tile(qi, True, state)
    o_ref[...] = (acc / l).astype(BF16)


def _decode_element(q_ref, kn_ref, vn_ref, lfn_ref, k_refs, v_refs, lf_refs, o_ref, s_ref, d_ref, Q):
    n_pages = len(k_refs)
    nt = (((1,), (1,)), ((), ()))
    H = N_HEADS
    M = Q * H
    page = k_refs[0].shape[0]
    PL = page * H
    RPP = PL // LANES

    qm = q_ref[...].reshape(M, HEAD_DIM)
    qmb = qm.astype(BF16)
    same_head = (lax.broadcasted_iota(jnp.int32, (M, LANES), 1) % H
                 == lax.broadcasted_iota(jnp.int32, (M, LANES), 0) % H)

    x = jnp.concatenate([r[...] for r in lf_refs], axis=0)
    lane = lax.broadcasted_iota(jnp.int32, x.shape, 1)
    sh = H
    while sh < LANES:
        x = x + jnp.where(lane >= sh, pltpu.roll(x, sh, axis=1), 0.0)
        sh *= 2
    tot = jnp.where(lane >= LANES - H, x, 0.0)
    inc = _cumsum_rows(tot)
    after = inc[inc.shape[0] - 1:, :] - inc
    sh = H
    while sh < LANES:
        after = after + pltpu.roll(after, sh, axis=1)
        sh *= 2
    row_tot = tot
    sh = H
    while sh < LANES:
        row_tot = row_tot + pltpu.roll(row_tot, sh, axis=1)
        sh *= 2
    d_ref[...] = after + (row_tot - x)

    lfn = lfn_ref[...]
    cum = [lfn[0:H, :]]
    for i in range(1, Q):
        cum.append(cum[-1] + lfn[i * H:(i + 1) * H, :])
    cum_q = jnp.concatenate(cum, axis=0)
    cum_k = jnp.concatenate([jnp.concatenate([c] * Q, axis=0) for c in cum], axis=1)

    qm_r = qmb.astype(F32)
    tile_q = lambda a: jnp.concatenate([a.astype(BF16).astype(F32)] * Q, axis=0)
    s_new = jnp.concatenate(
        [jnp.sum(qm_r * tile_q(kn_ref[i]), axis=-1, keepdims=True) for i in range(Q)], axis=1)
    s_new = s_new + (cum_q - cum_k)
    q_of_row = lax.broadcasted_iota(jnp.int32, (M, Q), 0) // H
    s_new = jnp.where(lax.broadcasted_iota(jnp.int32, (M, Q), 1) <= q_of_row, s_new, -jnp.inf)
    m = jnp.max(s_new, axis=-1, keepdims=True)

    for j in range(n_pages):
        kb = k_refs[j][...].reshape(PL, HEAD_DIM).astype(BF16)
        s = lax.dot_general(qmb, kb, nt, preferred_element_type=F32)
        for r in range(RPP):
            blk = s[:, r * LANES:(r + 1) * LANES] + (cum_q + d_ref[j * RPP + r:j * RPP + r + 1, :])
            blk = jnp.where(same_head, blk, -jnp.inf)
            s_ref[:, (j * RPP + r) * LANES:(j * RPP + r + 1) * LANES] = blk
            m = jnp.maximum(m, jnp.max(blk, axis=-1, keepdims=True))

    p_new = jnp.exp(s_new - m)
    l = jnp.sum(p_new, axis=-1, keepdims=True)
    p_new_r = p_new.astype(BF16).astype(F32)
    acc = p_new_r[:, 0:1] * tile_q(vn_ref[0])
    for i in range(1, Q):
        acc = acc + p_new_r[:, i:i + 1] * tile_q(vn_ref[i])
    for j in range(n_pages):
        p = jnp.exp(s_ref[:, j * PL:(j + 1) * PL] - m)
        l = l + jnp.sum(p, axis=-1, keepdims=True)
        vb = v_refs[j][...].reshape(PL, HEAD_DIM).astype(BF16)
        acc = acc + jnp.dot(p.astype(BF16), vb, preferred_element_type=F32)
    o_ref[...] = (acc / l).reshape(Q, H, HEAD_DIM)


def _attn_kernel(pt_ref, q_ref, kn_ref, vn_ref, lfn_ref, *rest, n_pages, Q, n_dec, n_tiles, nq, tq):
    k_refs = rest[:n_pages]
    v_refs = rest[n_pages:2 * n_pages]
    lf_refs = rest[2 * n_pages:3 * n_pages]
    pq_ref, pk_ref, pv_ref, pc_ref, pck_ref, o_ref, po_ref, s_ref, d_ref = rest[3 * n_pages:]
    step = pl.program_id(0)

    @pl.when(step < n_tiles)
    def _():
        _prompt_tile(pq_ref, pk_ref, pv_ref, pc_ref, pck_ref, po_ref, step % nq, (step // nq) % N_HEADS, tq)

    @pl.when(step < n_dec)
    def _():
        _decode_element(q_ref, kn_ref, vn_ref, lfn_ref, k_refs, v_refs, lf_refs, o_ref, s_ref, d_ref, Q)


def _attn(page_table_flat, q4, k4, v4, lfn_col, cache_k, cache_v, cache_lf, layer, n_phys, n_pages, Bd, Q,
          pq, pk, pv, pc, pck, B, L):
    page = cache_k.shape[1]
    base = layer * n_phys
    rpp = page * N_HEADS // LANES
    tq = min(L, 512)
    nq = L // tq
    n_tiles = B * N_HEADS * nq
    n_steps = max(Bd, n_tiles)
    dec = lambda s: jnp.minimum(s, Bd - 1)
    tile = lambda s: jnp.minimum(s, n_tiles - 1)
    pb = lambda s: tile(s) // (N_HEADS * nq)
    ph = lambda s: (tile(s) // nq) % N_HEADS
    pqi = lambda s: tile(s) % nq

    def paged(tail):
        return [pl.BlockSpec((None,) + tail, functools.partial(
            lambda s, pt, j: (pt[dec(s) * n_pages + j] + base,) + (0,) * len(tail), j=j)) for j in range(n_pages)]

    tok = pl.BlockSpec((Q, None, N_HEADS, HEAD_DIM), lambda s, pt: (0, dec(s), 0, 0))
    seq = pl.BlockSpec((L, HEAD_DIM), lambda s, pt: (pb(s), ph(s)))
    qtile = pl.BlockSpec((tq, HEAD_DIM), lambda s, pt: (pb(s) * nq + pqi(s), ph(s)))
    grid_spec = pltpu.PrefetchScalarGridSpec(
        num_scalar_prefetch=1, grid=(n_steps,),
        in_specs=[tok, tok, tok, pl.BlockSpec((None, Q * N_HEADS, 1), lambda s, pt: (dec(s), 0, 0))]
        + paged((page, N_HEADS, HEAD_DIM)) + paged((page, N_HEADS, HEAD_DIM)) + paged((rpp, LANES))
        + [qtile, seq, seq, pl.BlockSpec((L, LANES), lambda s, pt: (pb(s), 0)),
           pl.BlockSpec((None, None, nq, 1, tq), lambda s, pt: (pb(s), ph(s), 0, 0, 0))],
        out_specs=[tok, qtile],
        scratch_shapes=[pltpu.VMEM((Q * N_HEADS, n_pages * page * N_HEADS), F32),
                        pltpu.VMEM((n_pages * rpp, LANES), F32)])
    return pl.pallas_call(
        functools.partial(_attn_kernel, n_pages=n_pages, Q=Q, n_dec=Bd, n_tiles=n_tiles, nq=nq, tq=tq),
        grid_spec=grid_spec,
        out_shape=[jax.ShapeDtypeStruct((Q, Bd, N_HEADS, HEAD_DIM), F32), jax.ShapeDtypeStruct((B * L, D_ATTN), BF16)],
        compiler_params=_params(56, 1), name="attn",
    )(page_table_flat, q4, k4, v4, lfn_col, *([cache_k] * n_pages), *([cache_v] * n_pages), *([cache_lf] * n_pages),
      pq, pk, pv, pc, pck)


def _s5_kernel(u_ref, h0r_ref, h0i_ref, lr_ref, li_ref, ldt_ref, btr_ref, bti_ref, ctr_ref, cti_ref, d_ref,
               y_ref, fr_ref, fi_ref, xr_ref, xi_ref, bdr_ref, bdi_ref, cdr_ref, cdi_ref, *scratch, T, NC, chained):
    @pl.when((pl.program_id(0) == 0) & (pl.program_id(1) == 0))
    def _():
        for ref in (bdr_ref, bdi_ref, cdr_ref, cdi_ref):
            ref[...] = jnp.zeros_like(ref)

    for g in range(SLAB // SSM_GROUP):
        rows_c, rows_p = slice(g * SSM_GROUP, (g + 1) * SSM_GROUP), slice(g * SSM_STATE, (g + 1) * SSM_STATE)
        bdr_ref[rows_c, rows_p] = btr_ref[g]
        bdi_ref[rows_c, rows_p] = bti_ref[g]
        cdr_ref[rows_p, rows_c] = ctr_ref[g]
        cdi_ref[rows_p, rows_c] = cti_ref[g]

    lr, li = lr_ref[...], li_ref[...]
    dt = jnp.exp(ldt_ref[...])
    mag = jnp.exp(lr * dt)
    ar, ai = mag * jnp.cos(li * dt), mag * jnp.sin(li * dt)
    den = lr * lr + li * li
    zr = ((ar - 1.0) * lr + ai * li) / den
    zi = (ai * lr - (ar - 1.0) * li) / den
    bdr, bdi = bdr_ref[...], bdi_ref[...]
    pr = (zr * bdr - zi * bdi).astype(BF16)
    pi = (zr * bdi + zi * bdr).astype(BF16)
    n_cb = SLAB_STATES // LANES
    per = MXU_DIM // LANES
    if chained:
        stage_ref, perm_ref = scratch[4:]
        for t in range(per):
            stage_ref[t] = u_ref[:, t * LANES:(t + 1) * LANES]
        for s in range(T):
            for t in range(per):
                perm_ref[s * NC:(s + 1) * NC, t * LANES:(t + 1) * LANES] = stage_ref[t, pl.ds(s, NC, stride=T), :]
        u = perm_ref[...]
    else:
        u = u_ref[...]
    ub = u.astype(BF16)
    n_c2 = n_cb // per

    def project(c2):
        cols = slice(c2 * MXU_DIM, (c2 + 1) * MXU_DIM)
        x_r = jnp.dot(ub, pr[:, cols], preferred_element_type=F32)
        x_i = jnp.dot(ub, pi[:, cols], preferred_element_type=F32)
        for t in range(per):
            xr_ref[c2 * per + t] = x_r[:, t * LANES:(t + 1) * LANES]
            xi_ref[c2 * per + t] = x_i[:, t * LANES:(t + 1) * LANES]

    def readout(c2, y):
        rows = slice(c2 * MXU_DIM, (c2 + 1) * MXU_DIM)
        h_r = jnp.concatenate([xr_ref[c2 * per + t] for t in range(per)], axis=1).astype(BF16)
        h_i = jnp.concatenate([xi_ref[c2 * per + t] for t in range(per)], axis=1).astype(BF16)
        return y + (jnp.dot(h_r, cdr_ref[rows, :].astype(BF16), preferred_element_type=F32)
                    - jnp.dot(h_i, cdi_ref[rows, :].astype(BF16), preferred_element_type=F32))

    def sweep(cb, hr, hi, store):
        cols = slice(cb * LANES, (cb + 1) * LANES)
        a_r, a_i = ar[:, cols], ai[:, cols]
        for s in range(T):
            rows = slice(s * NC, (s + 1) * NC)
            xr, xi = xr_ref[cb, rows, :], xi_ref[cb, rows, :]
            hr, hi = a_r * hr - a_i * hi + xr, a_r * hi + a_i * hr + xi
            if store:
                xr_ref[cb, rows, :] = hr
                xi_ref[cb, rows, :] = hi
        return hr, hi

    if chained:
        er_ref, ei_ref, gr_ref, gi_ref = scratch[:4]
        zero = jnp.zeros((NC, LANES), F32)
        for c2 in range(n_c2):
            project(c2)
            for cb in range(c2 * per, (c2 + 1) * per):
                cols = slice(cb * LANES, (cb + 1) * LANES)
                er_ref[:, cols], ei_ref[:, cols] = sweep(cb, zero, zero, False)
        tr, ti = ar, ai
        for _ in range(T - 1):
            tr, ti = tr * ar - ti * ai, tr * ai + ti * ar

        def chunk_step(k, carry):
            sr, si = carry
            gr_ref[pl.ds(k, 1), :] = sr
            gi_ref[pl.ds(k, 1), :] = si
            er, ei = er_ref[pl.ds(k, 1), :], ei_ref[pl.ds(k, 1), :]
            return tr * sr - ti * si + er, tr * si + ti * sr + ei

        sr, si = lax.fori_loop(0, NC, chunk_step, (h0r_ref[...], h0i_ref[...]))
        fr_ref[...] = sr
        fi_ref[...] = si
        y = d_ref[...] * u
        for c2 in range(n_c2):
            for cb in range(c2 * per, (c2 + 1) * per):
                cols = slice(cb * LANES, (cb + 1) * LANES)
                sweep(cb, gr_ref[:, cols], gi_ref[:, cols], True)
            y = readout(c2, y)
    else:
        y = d_ref[...] * u
        for c2 in range(n_c2):
            project(c2)
            for cb in range(c2 * per, (c2 + 1) * per):
                cols = slice(cb * LANES, (cb + 1) * LANES)
                hr, hi = sweep(cb, h0r_ref[:, cols], h0i_ref[:, cols], True)
                fr_ref[:, cols] = hr
                fi_ref[:, cols] = hi
            y = readout(c2, y)

    if chained:
        for s in range(T):
            for t in range(per):
                stage_ref[t, pl.ds(s, NC, stride=T), :] = y[s * NC:(s + 1) * NC, t * LANES:(t + 1) * LANES]
        for t in range(per):
            y_ref[:, t * LANES:(t + 1) * LANES] = stage_ref[t]
    else:
        y_ref[...] = y


def _s5(u, h0r, h0i, lam_re, lam_im, log_dt, bt_re, bt_im, ct_re, ct_im, d_row, n_seq, T, NC, chained):
    rows = NC * T
    gps = SLAB // SSM_GROUP
    slab_row = lambda: pl.BlockSpec((1, SLAB_STATES), lambda b, s: (0, s))
    if chained:
        st_spec = pl.BlockSpec((None, 1, SLAB_STATES), lambda b, s: (b, 0, s))
        st_shape = jax.ShapeDtypeStruct((n_seq, 1, N_SLABS * SLAB_STATES), F32)
        scratch = [pltpu.VMEM((NC, SLAB_STATES), F32)] * 4 + [
            pltpu.VMEM((SLAB // LANES, rows, LANES), F32), pltpu.VMEM((rows, SLAB), F32)]
    else:
        st_spec = pl.BlockSpec((NC, SLAB_STATES), lambda b, s: (0, s))
        st_shape = jax.ShapeDtypeStruct((NC, N_SLABS * SLAB_STATES), F32)
        scratch = []
    return pl.pallas_call(
        functools.partial(_s5_kernel, T=T, NC=NC, chained=chained),
        grid=(n_seq, N_SLABS),
        in_specs=[pl.BlockSpec((rows, SLAB), lambda b, s: (b, s)), st_spec, st_spec,
                  slab_row(), slab_row(), slab_row(),
                  pl.BlockSpec((gps, SSM_GROUP, SSM_STATE), lambda b, s: (s, 0, 0)),
                  pl.BlockSpec((gps, SSM_GROUP, SSM_STATE), lambda b, s: (s, 0, 0)),
                  pl.BlockSpec((gps, SSM_STATE, SSM_GROUP), lambda b, s: (s, 0, 0)),
                  pl.BlockSpec((gps, SSM_STATE, SSM_GROUP), lambda b, s: (s, 0, 0)),
                  pl.BlockSpec((1, SLAB), lambda b, s: (0, s))],
        out_specs=[pl.BlockSpec((rows, SLAB), lambda b, s: (b, s)), st_spec, st_spec],
        out_shape=[jax.ShapeDtypeStruct((n_seq * rows, D_SSM), F32), st_shape, st_shape],
        scratch_shapes=[pltpu.VMEM((SLAB_STATES // LANES, rows, LANES), F32)] * 2
        + [pltpu.VMEM((SLAB, SLAB_STATES), F32)] * 2 + [pltpu.VMEM((SLAB_STATES, SLAB), F32)] * 2 + scratch,
        compiler_params=_params(48, 2), name="s5",
    )(u, h0r, h0i, lam_re, lam_im, log_dt, bt_re, bt_im, ct_re, ct_im, d_row)


def _mixout_kernel(x_ref, att_ref, y_ref, wg_ref, bg_ref, wo_ref, o_ref):
    y = y_ref[...]
    z = 0.5 * y * (1.0 + jnp.tanh(math.sqrt(2.0 / math.pi) * (y + 0.044715 * (y * y * y))))
    gate = _sigmoid(jnp.dot(z.astype(BF16), wg_ref[...], preferred_element_type=F32) + bg_ref[...])
    ssm = (z * gate).astype(BF16)
    o_ref[...] = (x_ref[...] + jnp.dot(att_ref[...], wo_ref[0:D_ATTN, :], preferred_element_type=F32)
                  + jnp.dot(ssm, wo_ref[D_ATTN:, :], preferred_element_type=F32))


def _mixout(x, att, y, wg, bg, wo):
    R = x.shape[0]
    tm = min(R, 512)
    row = lambda n: pl.BlockSpec((tm, n), lambda i: (i, 0))
    full = lambda a: pl.BlockSpec(a.shape, lambda i: (0,) * a.ndim, pipeline_mode=pl.Buffered(1))
    return pl.pallas_call(
        _mixout_kernel, grid=(R // tm,),
        in_specs=[row(D_MODEL), row(D_ATTN), row(D_SSM), full(wg), full(bg), full(wo)],
        out_specs=row(D_MODEL), out_shape=jax.ShapeDtypeStruct((R, D_MODEL), F32),
        compiler_params=_params(56, 1), name="mixout",
    )(x, att, y, wg, bg, wo)


def _ffn_kernel(x_ref, nw_ref, w1_ref, w3_ref, w2_ref, o_ref, h_ref):
    @pl.when(pl.program_id(1) == 0)
    def _():
        x = x_ref[...]
        h_ref[...] = _rms(x, nw_ref[...]).astype(BF16)
        o_ref[...] = x

    h = h_ref[...]
    for c in range(w1_ref.shape[1] // MXU_DIM):
        cols = slice(c * MXU_DIM, (c + 1) * MXU_DIM)
        a = jnp.dot(h, w1_ref[:, cols].astype(BF16), preferred_element_type=F32)
        b = jnp.dot(h, w3_ref[:, cols].astype(BF16), preferred_element_type=F32)
        g = (a * _sigmoid(a) * b).astype(BF16)
        o_ref[...] += jnp.dot(g, w2_ref[cols, :].astype(BF16), preferred_element_type=F32)


def _ffn(x, nw, w1, w3, w2, layer):
    R = x.shape[0]
    d_ff = w1.shape[2]
    tm = min(R, 1024)
    tf = 2 * MXU_DIM
    assert d_ff % tf == 0 and R % tm == 0
    return pl.pallas_call(
        _ffn_kernel, grid=(R // tm, d_ff // tf),
        in_specs=[pl.BlockSpec((tm, D_MODEL), lambda i, f: (i, 0), pipeline_mode=pl.Buffered(1)),
                  pl.BlockSpec((1, D_MODEL), lambda i, f: (0, 0)),
                  pl.BlockSpec((None, D_MODEL, tf), lambda i, f: (layer, 0, f)),
                  pl.BlockSpec((None, D_MODEL, tf), lambda i, f: (layer, 0, f)),
                  pl.BlockSpec((None, tf, D_MODEL), lambda i, f: (layer, f, 0))],
        out_specs=pl.BlockSpec((tm, D_MODEL), lambda i, f: (i, 0)),
        out_shape=jax.ShapeDtypeStruct((R, D_MODEL), F32),
        scratch_shapes=[pltpu.VMEM((tm, D_MODEL), BF16)],
        compiler_params=_params(60, 2), name="ffn",
    )(x, nw, w1, w3, w2)


SUBLANES = 8
HIST = SUBLANES * (max(POOL_WINDOWS).bit_length() - 1)


def _pool_prompt_kernel(x_ref, nw_ref, pw_ref, ps_ref, o_ref, st_ref, ext_ref, lvl_ref, *, tm, tiles_per_seq):
    i = pl.program_id(0)
    t_in_seq = i % tiles_per_seq
    n = HIST + tm

    @pl.when(t_in_seq == 0)
    def _():
        ext_ref[0:HIST, :] = jnp.zeros((HIST, D_MODEL), F32)

    x = x_ref[...]
    hp = _rms(x, nw_ref[...])
    ext_ref[HIST:n, :] = hp
    pos = t_in_seq * tm + lax.broadcasted_iota(jnp.int32, (tm, 1), 0)
    for g, w in enumerate(POOL_WINDOWS):
        cols = slice(g * POOL_GROUP, (g + 1) * POOL_GROUP)
        n_levels = w.bit_length() - 1
        assert w == 1 << n_levels and SUBLANES * n_levels <= HIST
        acc = None
        for k in range(n_levels):
            start, sh = SUBLANES * (k + 1), 1 << k
            if k == 0:
                acc = ext_ref[start:n, cols] + ext_ref[start - sh:n - sh, cols]
            else:
                acc = lvl_ref[k - 1, start:n, :] + lvl_ref[k - 1, start - sh:n - sh, :]
            if k < n_levels - 1:
                lvl_ref[k, start:n, :] = acc
        acc = acc[HIST - SUBLANES * n_levels:, :]
        inv = 1.0 / jnp.minimum(pos + 1, w).astype(F32)
        pooled = (acc * inv - hp[:, cols]).astype(BF16)
        o_ref[:, cols] = x[:, cols] + jnp.dot(pooled, pw_ref[g], preferred_element_type=F32) * ps_ref[:, cols]
    st_ref[...] = ext_ref[tm + HIST - POOL_BUF:tm + HIST, :]
    ext_ref[0:HIST, :] = ext_ref[tm:tm + HIST, :]


def _pool_prompt(x, nw, pw, ps, B, L):
    tm = min(L, 512)
    assert L % tm == 0 and tm >= HIST
    tiles_per_seq = L // tm
    return pl.pallas_call(
        functools.partial(_pool_prompt_kernel, tm=tm, tiles_per_seq=tiles_per_seq),
        grid=(B * tiles_per_seq,),
        in_specs=[pl.BlockSpec((tm, D_MODEL), lambda i: (i, 0)),
                  pl.BlockSpec((1, D_MODEL), lambda i: (0, 0)),
                  pl.BlockSpec(pw.shape, lambda i: (0, 0, 0)),
                  pl.BlockSpec((1, D_MODEL), lambda i: (0, 0))],
        out_specs=[pl.BlockSpec((tm, D_MODEL), lambda i: (i, 0)),
                   pl.BlockSpec((None, POOL_BUF, D_MODEL), lambda i: (i // tiles_per_seq, 0, 0))],
        out_shape=[jax.ShapeDtypeStruct((B * L, D_MODEL), F32), jax.ShapeDtypeStruct((B, POOL_BUF, D_MODEL), F32)],
        scratch_shapes=[pltpu.VMEM((HIST + tm, D_MODEL), F32),
                        pltpu.VMEM((HIST // SUBLANES - 1, HIST + tm, POOL_GROUP), F32)],
        compiler_params=_params(40, 1), name="pool_prompt",
    )(x, nw, pw, ps)


def _pool_sample(x, nw, sp, pw, ps, Bd, Q, pos0):
    G = len(POOL_WINDOWS)
    hist = [pl.BlockSpec((Bd, POOL_GROUP), functools.partial(lambda g, j: (0, j * G + g), j=j)) for j in range(POOL_BUF)]
    col = pl.BlockSpec((Q * Bd, POOL_GROUP), lambda g: (0, g))
    return pl.pallas_call(
        functools.partial(_pool_sample_kernel, Bd=Bd, Q=Q, pos0=pos0),
        grid=(G,),
        in_specs=[pl.BlockSpec((Q * Bd, D_MODEL), lambda g: (0, 0)),
                  pl.BlockSpec((1, D_MODEL), lambda g: (0, 0)),
                  pl.BlockSpec((None, POOL_GROUP, POOL_GROUP), lambda g: (g, 0, 0)),
                  pl.BlockSpec((1, POOL_GROUP), lambda g: (0, g))] + hist,
        out_specs=[col, col],
        out_shape=[jax.ShapeDtypeStruct((Q * Bd, D_MODEL), F32)] * 2,
        scratch_shapes=[pltpu.VMEM((Q * Bd, 1), F32)],
        compiler_params=_params(40, 1), name="pool_sample",
    )(x, nw, pw, ps, *([sp] * POOL_BUF))


def _pool_sample_kernel(x_ref, nw_ref, pw_ref, ps_ref, *rest, Bd, Q, pos0):
    hist_refs = rest[:POOL_BUF]
    o_ref, hs_ref, inv_ref = rest[POOL_BUF:]
    g = pl.program_id(0)

    @pl.when(g == 0)
    def _():
        x = x_ref[...]
        inv_ref[...] = lax.rsqrt(jnp.mean(x * x, axis=-1, keepdims=True) + RMS_EPS)

    for wi, w in enumerate(POOL_WINDOWS):
        @pl.when(g == wi)
        def _(wi=wi, w=w):
            cols = slice(wi * POOL_GROUP, (wi + 1) * POOL_GROUP)
            xs, hs = [], []
            for q in range(Q):
                rows = slice(q * Bd, (q + 1) * Bd)
                xq = x_ref[rows, cols]
                xs.append(xq)
                hs.append(xq * inv_ref[rows, :] * nw_ref[:, cols])
                hs_ref[rows, :] = hs[q]
            ext = lambda j: hist_refs[j][...] if j < POOL_BUF else hs[j - POOL_BUF]
            for q in range(Q):
                acc = hs[q]
                for k in range(1, w):
                    acc = acc + ext(POOL_BUF + q - k)
                cnt = float(min(pos0 + q + 1, w))
                pooled = (acc / cnt - hs[q]).astype(BF16)
                val = jnp.dot(pooled, pw_ref[...], preferred_element_type=F32) * ps_ref[...]
                o_ref[q * Bd:(q + 1) * Bd, :] = xs[q] + val


def kernel(x_prompt, x_sample, cache_k, cache_v, cache_logf, state_s5_re, state_s5_im, state_pool, page_table, norm_mix_w, norm_ffn_w, w_in, b_f, q_norm_w, k_norm_w, s5_lam_re, s5_lam_im, s5_log_dt, s5_b_re, s5_b_im, s5_c_re, s5_c_im, s5_d, w_glu, b_glu, w_out, pool_w, pool_scale, ffn_w1, ffn_w3, ffn_w2):
    B, L, D = x_prompt.shape
    Bd, Q, _ = x_sample.shape
    depth = norm_mix_w.shape[0]
    n_attn, n_phys, page = cache_k.shape[:3]
    n_pages = page_table.shape[1]
    past_len = n_pages * page
    T = 16
    assert D == D_MODEL and L % T == 0

    xp = x_prompt.reshape(B * L, D)
    xs = jnp.swapaxes(x_sample, 0, 1).reshape(Q * Bd, D)
    from_tm = lambda a, *tail: jnp.swapaxes(a.reshape(Q, Bd, *tail), 0, 1)
    pt_flat = page_table.reshape(-1)
    assert (page * N_HEADS) % LANES == 0
    ck_flat = cache_k.reshape(n_attn * n_phys, page, N_HEADS, HEAD_DIM)
    cv_flat = cache_v.reshape(n_attn * n_phys, page, N_HEADS, HEAD_DIM)
    clf_flat = cache_logf.reshape(n_attn * n_phys, page * N_HEADS // LANES, LANES)

    kp, vp, fp, srp, sip, plp = [], [], [], [], [], []
    ksm, vsm, fsm, srs, sis, pls = [], [], [], [], [], []
    for layer in range(depth):
        i = layer // 2
        nw = norm_mix_w[layer].reshape(1, D)
        if layer % 2 == 0:
            w = w_in[i]
            wq = w[:, :D_ATTN].astype(BF16)
            wk = w[:, D_ATTN:2 * D_ATTN].astype(BF16)
            wv = w[:, 2 * D_ATTN:3 * D_ATTN].astype(BF16)
            wf = jnp.pad(w[:, 3 * D_ATTN:3 * D_ATTN + N_HEADS], ((0, 0), (0, LANES - N_HEADS))).astype(BF16)
            wu = w[:, 3 * D_ATTN + N_HEADS:].astype(BF16)
            bf = jnp.pad(b_f[i], (0, LANES - N_HEADS)).reshape(1, LANES)
            qg = q_norm_w[i].reshape(1, HEAD_DIM)
            kg = k_norm_w[i].reshape(1, HEAD_DIM)
            wg = w_glu[i].astype(BF16)
            bg = b_glu[i].reshape(1, D_SSM)
            wo = w_out[i].astype(BF16)
            lam_re = s5_lam_re[i].reshape(1, -1)
            lam_im = s5_lam_im[i].reshape(1, -1)
            log_dt = jnp.broadcast_to(s5_log_dt[i][:, None], (N_SSM_GROUPS, SSM_STATE)).reshape(1, -1)
            bd_re = jnp.swapaxes(s5_b_re[i], 1, 2)
            bd_im = jnp.swapaxes(s5_b_im[i], 1, 2)
            cd_re = jnp.swapaxes(s5_c_re[i], 1, 2)
            cd_im = jnp.swapaxes(s5_c_im[i], 1, 2)
            d_row = s5_d[i].reshape(1, D_SSM)

            qb, k, kb, v, vb, u, lf, c = _inproj(xp, nw, wq, wk, wv, wu, wf, bf, qg, kg, seq_len=L,
                                                 q_scale=HEAD_DIM ** -0.5 * math.log2(math.e))
            qb_s, k_s, _, v_s, _, u_s, lf_s = _inproj(xs, nw, wq, wk, wv, wu, wf, bf, qg, kg, seq_len=None,
                                                      q_scale=HEAD_DIM ** -0.5)
            c_t = jnp.swapaxes(c[:, :N_HEADS].reshape(B, L, N_HEADS), 1, 2)
            tq = min(L, 512)
            lfn = lf_s[:, :N_HEADS].reshape(Q, Bd, N_HEADS)
            heads = lambda a: a.astype(F32).reshape(Q, Bd, N_HEADS, HEAD_DIM)
            att_s, att = _attn(pt_flat, heads(qb_s), heads(k_s), heads(v_s),
                               jnp.swapaxes(lfn, 0, 1).reshape(Bd, Q * N_HEADS, 1),
                               ck_flat, cv_flat, clf_flat, i, n_phys, n_pages, Bd, Q,
                               qb, kb, vb, c, c_t.reshape(B, N_HEADS, L // tq, 1, tq), B, L)

            zeros = jnp.zeros((B, 1, N_SLABS * SLAB_STATES), F32)
            y, h_re, h_im = _s5(u, zeros, zeros, lam_re, lam_im, log_dt, bd_re, bd_im, cd_re, cd_im, d_row,
                                n_seq=B, T=T, NC=L // T, chained=True)
            xp = _mixout(xp, att, y, wg, bg, wo)
            kp.append(k.reshape(B, L, N_HEADS, HEAD_DIM))
            vp.append(v.reshape(B, L, N_HEADS, HEAD_DIM))
            fp.append(lf[:, :N_HEADS].reshape(B, L, N_HEADS))
            srp.append(h_re.reshape(B, N_SSM_GROUPS, SSM_STATE))
            sip.append(h_im.reshape(B, N_SSM_GROUPS, SSM_STATE))

            y, h_re, h_im = _s5(u_s, state_s5_re[i].reshape(Bd, -1), state_s5_im[i].reshape(Bd, -1),
                                lam_re, lam_im, log_dt, bd_re, bd_im, cd_re, cd_im, d_row,
                                n_seq=1, T=Q, NC=Bd, chained=False)
            xs = _mixout(xs, att_s.astype(BF16).reshape(Q * Bd, D_ATTN), y, wg, bg, wo)
            ksm.append(from_tm(k_s, N_HEADS, HEAD_DIM))
            vsm.append(from_tm(v_s, N_HEADS, HEAD_DIM))
            fsm.append(jnp.swapaxes(lfn, 0, 1))
            srs.append(h_re.reshape(Bd, N_SSM_GROUPS, SSM_STATE))
            sis.append(h_im.reshape(Bd, N_SSM_GROUPS, SSM_STATE))
        else:
            pw = pool_w[i].astype(BF16)
            ps = pool_scale[i].reshape(1, D)
            xp, st = _pool_prompt(xp, nw, pw, ps, B, L)
            plp.append(st)
            xs, hs = _pool_sample(xs, nw, state_pool[i].reshape(Bd, POOL_BUF * D), pw, ps, Bd, Q, past_len)
            pls.append(jnp.concatenate([state_pool[i], from_tm(hs, D)], axis=1)[:, -POOL_BUF:])
        nf = norm_ffn_w[layer].reshape(1, D)
        xp = _ffn(xp, nf, ffn_w1, ffn_w3, ffn_w2, layer)
        xs = _ffn(xs, nf, ffn_w1, ffn_w3, ffn_w2, layer)
    return (xp.reshape(B, L, D), from_tm(xs, D),
            jnp.stack(kp), jnp.stack(vp), jnp.stack(fp), jnp.stack(srp), jnp.stack(sip), jnp.stack(plp),
            jnp.stack(ksm), jnp.stack(vsm), jnp.stack(fsm), jnp.stack(srs), jnp.stack(sis), jnp.stack(pls))
```

```python
import functools
import math

import jax
import jax.numpy as jnp
from jax import lax
from jax.experimental import pallas as pl
from jax.experimental.pallas import tpu as pltpu

F32 = jnp.float32
BF16 = jnp.bfloat16

D_MODEL = 2048
D_ATTN = D_MODEL // 2
HEAD_DIM = 128
N_HEADS = D_ATTN // HEAD_DIM
D_SSM = D_MODEL - D_ATTN
SSM_GROUP = 16
N_SSM_GROUPS = D_SSM // SSM_GROUP
SSM_STATE = 64
POOL_WINDOWS = (2, 4, 8, 16)
POOL_GROUP = D_MODEL // len(POOL_WINDOWS)
POOL_BUF = max(POOL_WINDOWS) - 1
RMS_EPS = 1e-6

LANES = 128
MXU_DIM = 256
SLAB = MXU_DIM
SLAB_STATES = (SLAB // SSM_GROUP) * SSM_STATE
N_SLABS = D_SSM // SLAB
VMEM_BYTES = 64 * 1024 * 1024


def _params(vmem_mb, n_axes):
    return pltpu.CompilerParams(dimension_semantics=("arbitrary",) * n_axes,
                                vmem_limit_bytes=min(vmem_mb * 1024 * 1024, VMEM_BYTES - 4 * 1024 * 1024))


def _rms(x, w):
    return x * lax.rsqrt(jnp.mean(x * x, axis=-1, keepdims=True) + RMS_EPS) * w


def _log_sigmoid(x):
    return jnp.minimum(x, 0.0) - jnp.log1p(jnp.exp(-jnp.abs(x)))


def _sigmoid(x):
    return 1.0 / (1.0 + jnp.exp(-x))


def _split3(x):
    hi = x.astype(BF16)
    r = x - hi.astype(F32)
    mid = r.astype(BF16)
    lo = (r - mid.astype(F32)).astype(BF16)
    return hi, mid, lo


def _cumsum_rows(x):
    n = x.shape[0]
    tri = (lax.broadcasted_iota(jnp.int32, (n, n), 1) <= lax.broadcasted_iota(jnp.int32, (n, n), 0)).astype(BF16)
    hi, mid, lo = _split3(x)
    z = jnp.dot(tri, jnp.concatenate([hi, mid, lo], axis=1), preferred_element_type=F32)
    return z[:, :LANES] + z[:, LANES:2 * LANES] + z[:, 2 * LANES:]


def _inproj_kernel(x_ref, nw_ref, wq_ref, wk_ref, wv_ref, wu_ref, wf_ref, bf_ref, qg_ref, kg_ref,
                   q_ref, k_ref, kb_ref, v_ref, vb_ref, u_ref, lf_ref, *rest, tiles_per_seq, scale):
    h = _rms(x_ref[...], nw_ref[...]).astype(BF16)
    for j in range(D_ATTN // MXU_DIM):
        cols = slice(j * MXU_DIM, (j + 1) * MXU_DIM)
        zq = jnp.dot(h, wq_ref[:, cols], preferred_element_type=F32)
        zk = jnp.dot(h, wk_ref[:, cols], preferred_element_type=F32)
        zv = jnp.dot(h, wv_ref[:, cols], preferred_element_type=F32)
        for t in range(MXU_DIM // HEAD_DIM):
            sub = slice(t * HEAD_DIM, (t + 1) * HEAD_DIM)
            dst = slice(j * MXU_DIM + t * HEAD_DIM, j * MXU_DIM + (t + 1) * HEAD_DIM)
            q_ref[:, dst] = (_rms(zq[:, sub], qg_ref[...]) * scale).astype(BF16)
            kn = _rms(zk[:, sub], kg_ref[...])
            k_ref[:, dst] = kn
            kb_ref[:, dst] = kn.astype(BF16)
        v_ref[:, cols] = zv
        vb_ref[:, cols] = zv.astype(BF16)
        u_ref[:, cols] = jnp.dot(h, wu_ref[:, cols], preferred_element_type=F32)
    lf = _log_sigmoid(jnp.dot(h, wf_ref[...], preferred_element_type=F32) + bf_ref[...])
    lf_ref[...] = lf
    if tiles_per_seq is not None:
        c_ref, carry_ref = rest

        @pl.when(pl.program_id(0) % tiles_per_seq == 0)
        def _():
            carry_ref[...] = jnp.zeros_like(carry_ref)

        c = _cumsum_rows(lf) + carry_ref[...]
        c_ref[...] = c
        carry_ref[...] = c[c.shape[0] - 1:, :]


def _inproj(x, nw, wq, wk, wv, wu, wf, bf, qg, kg, seq_len, q_scale):
    R = x.shape[0]
    tm = min(R, 512)
    row = lambda n: pl.BlockSpec((tm, n), lambda i: (i, 0))
    full = lambda a: pl.BlockSpec(a.shape, lambda i: (0,) * a.ndim, pipeline_mode=pl.Buffered(1))
    out_shape = [jax.ShapeDtypeStruct((R, D_ATTN), BF16), jax.ShapeDtypeStruct((R, D_ATTN), F32),
                 jax.ShapeDtypeStruct((R, D_ATTN), BF16), jax.ShapeDtypeStruct((R, D_ATTN), F32),
                 jax.ShapeDtypeStruct((R, D_ATTN), BF16), jax.ShapeDtypeStruct((R, D_SSM), F32),
                 jax.ShapeDtypeStruct((R, LANES), F32)]
    out_specs = [row(D_ATTN)] * 5 + [row(D_SSM), row(LANES)]
    scratch = []
    tiles_per_seq = None
    if seq_len is not None:
        assert seq_len % tm == 0
        tiles_per_seq = seq_len // tm
        out_shape.append(jax.ShapeDtypeStruct((R, LANES), F32))
        out_specs.append(row(LANES))
        scratch.append(pltpu.VMEM((1, LANES), F32))
    return pl.pallas_call(
        functools.partial(_inproj_kernel, tiles_per_seq=tiles_per_seq, scale=q_scale),
        grid=(R // tm,),
        in_specs=[row(D_MODEL), full(nw), full(wq), full(wk), full(wv), full(wu), full(wf), full(bf), full(qg), full(kg)],
        out_specs=out_specs, out_shape=out_shape, scratch_shapes=scratch,
        compiler_params=_params(60, 1), name="inproj",
    )(x, nw, wq, wk, wv, wu, wf, bf, qg, kg)


def _attn_prompt_kernel(q_ref, k_ref, v_ref, c_ref, ck_ref, o_ref, *, L, tq):
    nt = (((1,), (1,)), ((), ()))
    log2e = math.log2(math.e)
    head = lax.broadcasted_iota(jnp.int32, (tq, LANES), 1) == pl.program_id(1)
    for qi in range(L // tq):
        rows = slice(qi * tq, (qi + 1) * tq)
        q = q_ref[rows, :]
        cq = jnp.sum(jnp.where(head, c_ref[rows, :], 0.0), axis=-1, keepdims=True) * log2e
        m = l = acc = None
        for ki in range(qi + 1):
            cols = slice(ki * tq, (ki + 1) * tq)
            t = lax.dot_general(q, k_ref[cols, :], nt, preferred_element_type=F32) - ck_ref[:, cols] * log2e
            if ki == qi:
                keep = lax.broadcasted_iota(jnp.int32, (tq, tq), 1) <= lax.broadcasted_iota(jnp.int32, (tq, tq), 0)
                t = jnp.where(keep, t, -jnp.inf)
            mx = jnp.max(t, axis=-1, keepdims=True) + cq
            if ki == 0:
                m = mx
                p = jnp.exp2(t - (m - cq))
                l = jnp.sum(p, axis=-1, keepdims=True)
                acc = jnp.dot(p.astype(BF16), v_ref[cols, :], preferred_element_type=F32)
            else:
                m_new = jnp.maximum(m, mx)
                alpha = jnp.exp2(m - m_new)
                p = jnp.exp2(t - (m_new - cq))
                l = alpha * l + jnp.sum(p, axis=-1, keepdims=True)
                acc = alpha * acc + jnp.dot(p.astype(BF16), v_ref[cols, :], preferred_element_type=F32)
                m = m_new
        o_ref[rows, :] = (acc / l).astype(BF16)


def _attn_prompt(qb, kb, vb, c, ck, B, L):
    tq = min(L, 512)
    blk = pl.BlockSpec((L, HEAD_DIM), lambda b, h: (b, h))
    return pl.pallas_call(
        functools.partial(_attn_prompt_kernel, L=L, tq=tq),
        grid=(B, N_HEADS),
        in_specs=[blk, blk, blk,
                  pl.BlockSpec((L, LANES), lambda b, h: (b, 0)),
                  pl.BlockSpec((None, None, 1, L), lambda b, h: (b, h, 0, 0))],
        out_specs=blk, out_shape=jax.ShapeDtypeStruct((B * L, D_ATTN), BF16),
        compiler_params=_params(32, 2), name="attn_prompt",
    )(qb, kb, vb, c, ck)


def _attn_sample_kernel(pt_ref, q_ref, kn_ref, vn_ref, lfn_ref, *rest, n_pages, Q):
    k_refs = rest[:n_pages]
    v_refs = rest[n_pages:2 * n_pages]
    lf_refs = rest[2 * n_pages:3 * n_pages]
    o_ref, s_ref, d_ref = rest[3 * n_pages:]
    nt = (((1,), (1,)), ((), ()))
    H = N_HEADS
    M = Q * H
    page = k_refs[0].shape[0]
    PL = page * H
    RPP = PL // LANES

    qm = q_ref[...].reshape(M, HEAD_DIM)
    qmb = qm.astype(BF16)
    same_head = (lax.broadcasted_iota(jnp.int32, (M, LANES), 1) % H
                 == lax.broadcasted_iota(jnp.int32, (M, LANES), 0) % H)

    x = jnp.concatenate([r[...] for r in lf_refs], axis=0)
    lane = lax.broadcasted_iota(jnp.int32, x.shape, 1)
    sh = H
    while sh < LANES:
        x = x + jnp.where(lane >= sh, pltpu.roll(x, sh, axis=1), 0.0)
        sh *= 2
    tot = jnp.where(lane >= LANES - H, x, 0.0)
    inc = _cumsum_rows(tot)
    after = inc[inc.shape[0] - 1:, :] - inc
    sh = H
    while sh < LANES:
        after = after + pltpu.roll(after, sh, axis=1)
        sh *= 2
    row_tot = tot
    sh = H
    while sh < LANES:
        row_tot = row_tot + pltpu.roll(row_tot, sh, axis=1)
        sh *= 2
    d_ref[...] = after + (row_tot - x)

    lfn = lfn_ref[...]
    cum = [lfn[0:H, :]]
    for i in range(1, Q):
        cum.append(cum[-1] + lfn[i * H:(i + 1) * H, :])
    cum_q = jnp.concatenate(cum, axis=0)
    cum_k = jnp.concatenate([jnp.concatenate([c] * Q, axis=0) for c in cum], axis=1)

    qm_r = qmb.astype(F32)
    tile_q = lambda a: jnp.concatenate([a.astype(BF16).astype(F32)] * Q, axis=0)
    s_new = jnp.concatenate(
        [jnp.sum(qm_r * tile_q(kn_ref[i]), axis=-1, keepdims=True) for i in range(Q)], axis=1)
    s_new = s_new + (cum_q - cum_k)
    q_of_row = lax.broadcasted_iota(jnp.int32, (M, Q), 0) // H
    s_new = jnp.where(lax.broadcasted_iota(jnp.int32, (M, Q), 1) <= q_of_row, s_new, -jnp.inf)
    m = jnp.max(s_new, axis=-1, keepdims=True)

    for j in range(n_pages):
        kb = k_refs[j][...].reshape(PL, HEAD_DIM).astype(BF16)
        s = lax.dot_general(qmb, kb, nt, preferred_element_type=F32)
        for r in range(RPP):
            blk = s[:, r * LANES:(r + 1) * LANES] + (cum_q + d_ref[j * RPP + r:j * RPP + r + 1, :])
            blk = jnp.where(same_head, blk, -jnp.inf)
            s_ref[:, (j * RPP + r) * LANES:(j * RPP + r + 1) * LANES] = blk
            m = jnp.maximum(m, jnp.max(blk, axis=-1, keepdims=True))

    p_new = jnp.exp(s_new - m)
    l = jnp.sum(p_new, axis=-1, keepdims=True)
    p_new_r = p_new.astype(BF16).astype(F32)
    acc = p_new_r[:, 0:1] * tile_q(vn_ref[0])
    for i in range(1, Q):
        acc = acc + p_new_r[:, i:i + 1] * tile_q(vn_ref[i])
    for j in range(n_pages):
        p = jnp.exp(s_ref[:, j * PL:(j + 1) * PL] - m)
        l = l + jnp.sum(p, axis=-1, keepdims=True)
        vb = v_refs[j][...].reshape(PL, HEAD_DIM).astype(BF16)
        acc = acc + jnp.dot(p.astype(BF16), vb, preferred_element_type=F32)
    o_ref[...] = (acc / l).reshape(Q, H, HEAD_DIM)


def _attn_sample(page_table_flat, q4, k4, v4, lfn_col, cache_k, cache_v, cache_lf, layer, n_phys, n_pages, Bd, Q):
    page = cache_k.shape[1]
    base = layer * n_phys
    rpp = page * N_HEADS // LANES

    def paged(tail):
        return [pl.BlockSpec((None,) + tail, functools.partial(
            lambda b, pt, j: (pt[b * n_pages + j] + base,) + (0,) * len(tail), j=j)) for j in range(n_pages)]

    tok = pl.BlockSpec((Q, None, N_HEADS, HEAD_DIM), lambda b, pt: (0, b, 0, 0))
    grid_spec = pltpu.PrefetchScalarGridSpec(
        num_scalar_prefetch=1, grid=(Bd,),
        in_specs=[tok, tok, tok, pl.BlockSpec((None, Q * N_HEADS, 1), lambda b, pt: (b, 0, 0))]
        + paged((page, N_HEADS, HEAD_DIM)) + paged((page, N_HEADS, HEAD_DIM)) + paged((rpp, LANES)),
        out_specs=tok,
        scratch_shapes=[pltpu.VMEM((Q * N_HEADS, n_pages * page * N_HEADS), F32),
                        pltpu.VMEM((n_pages * rpp, LANES), F32)])
    return pl.pallas_call(
        functools.partial(_attn_sample_kernel, n_pages=n_pages, Q=Q),
        grid_spec=grid_spec, out_shape=jax.ShapeDtypeStruct((Q, Bd, N_HEADS, HEAD_DIM), F32),
        compiler_params=_params(56, 1), name="attn_sample",
    )(page_table_flat, q4, k4, v4, lfn_col, *([cache_k] * n_pages), *([cache_v] * n_pages), *([cache_lf] * n_pages))


def _s5_kernel(u_ref, h0r_ref, h0i_ref, lr_ref, li_ref, ldt_ref, btr_ref, bti_ref, ctr_ref, cti_ref, d_ref,
               y_ref, fr_ref, fi_ref, xr_ref, xi_ref, bdr_ref, bdi_ref, cdr_ref, cdi_ref, *scratch, T, NC, chained):
    @pl.when((pl.program_id(0) == 0) & (pl.program_id(1) == 0))
    def _():
        for ref in (bdr_ref, bdi_ref, cdr_ref, cdi_ref):
            ref[...] = jnp.zeros_like(ref)

    for g in range(SLAB // SSM_GROUP):
        rows_c, rows_p = slice(g * SSM_GROUP, (g + 1) * SSM_GROUP), slice(g * SSM_STATE, (g + 1) * SSM_STATE)
        bdr_ref[rows_c, rows_p] = btr_ref[g]
        bdi_ref[rows_c, rows_p] = bti_ref[g]
        cdr_ref[rows_p, rows_c] = ctr_ref[g]
        cdi_ref[rows_p, rows_c] = cti_ref[g]

    lr, li = lr_ref[...], li_ref[...]
    dt = jnp.exp(ldt_ref[...])
    mag = jnp.exp(lr * dt)
    ar, ai = mag * jnp.cos(li * dt), mag * jnp.sin(li * dt)
    den = lr * lr + li * li
    zr = ((ar - 1.0) * lr + ai * li) / den
    zi = (ai * lr - (ar - 1.0) * li) / den
    bdr, bdi = bdr_ref[...], bdi_ref[...]
    pr = (zr * bdr - zi * bdi).astype(BF16)
    pi = (zr * bdi + zi * bdr).astype(BF16)
    n_cb = SLAB_STATES // LANES
    per = MXU_DIM // LANES
    if chained:
        stage_ref, perm_ref = scratch[4:]
        for t in range(per):
            stage_ref[t] = u_ref[:, t * LANES:(t + 1) * LANES]
        for s in range(T):
            for t in range(per):
                perm_ref[s * NC:(s + 1) * NC, t * LANES:(t + 1) * LANES] = stage_ref[t, pl.ds(s, NC, stride=T), :]
        u = perm_ref[...]
    else:
        u = u_ref[...]
    ub = u.astype(BF16)
    for c2 in range(n_cb // per):
        cols = slice(c2 * MXU_DIM, (c2 + 1) * MXU_DIM)
        x_r = jnp.dot(ub, pr[:, cols], preferred_element_type=F32)
        x_i = jnp.dot(ub, pi[:, cols], preferred_element_type=F32)
        for t in range(per):
            xr_ref[c2 * per + t] = x_r[:, t * LANES:(t + 1) * LANES]
            xi_ref[c2 * per + t] = x_i[:, t * LANES:(t + 1) * LANES]

    def sweep(cb, hr, hi, store):
        cols = slice(cb * LANES, (cb + 1) * LANES)
        a_r, a_i = ar[:, cols], ai[:, cols]
        for s in range(T):
            rows = slice(s * NC, (s + 1) * NC)
            xr, xi = xr_ref[cb, rows, :], xi_ref[cb, rows, :]
            hr, hi = a_r * hr - a_i * hi + xr, a_r * hi + a_i * hr + xi
            if store:
                xr_ref[cb, rows, :] = hr
                xi_ref[cb, rows, :] = hi
        return hr, hi

    if chained:
        er_ref, ei_ref, gr_ref, gi_ref = scratch[:4]
        zero = jnp.zeros((NC, LANES), F32)
        for cb in range(n_cb):
            cols = slice(cb * LANES, (cb + 1) * LANES)
            er_ref[:, cols], ei_ref[:, cols] = sweep(cb, zero, zero, False)
        tr, ti = ar, ai
        for _ in range(T - 1):
            tr, ti = tr * ar - ti * ai, tr * ai + ti * ar

        def chunk_step(k, carry):
            sr, si = carry
            gr_ref[pl.ds(k, 1), :] = sr
            gi_ref[pl.ds(k, 1), :] = si
            er, ei = er_ref[pl.ds(k, 1), :], ei_ref[pl.ds(k, 1), :]
            return tr * sr - ti * si + er, tr * si + ti * sr + ei

        sr, si = lax.fori_loop(0, NC, chunk_step, (h0r_ref[...], h0i_ref[...]))
        fr_ref[...] = sr
        fi_ref[...] = si
        for cb in range(n_cb):
            cols = slice(cb * LANES, (cb + 1) * LANES)
            sweep(cb, gr_ref[:, cols], gi_ref[:, cols], True)
    else:
        for cb in range(n_cb):
            cols = slice(cb * LANES, (cb + 1) * LANES)
            hr, hi = sweep(cb, h0r_ref[:, cols], h0i_ref[:, cols], True)
            fr_ref[:, cols] = hr
            fi_ref[:, cols] = hi

    y = d_ref[...] * u
    for c2 in range(n_cb // per):
        rows = slice(c2 * MXU_DIM, (c2 + 1) * MXU_DIM)
        h_r = jnp.concatenate([xr_ref[c2 * per + t] for t in range(per)], axis=1).astype(BF16)
        h_i = jnp.concatenate([xi_ref[c2 * per + t] for t in range(per)], axis=1).astype(BF16)
        y = y + (jnp.dot(h_r, cdr_ref[rows, :].astype(BF16), preferred_element_type=F32)
                 - jnp.dot(h_i, cdi_ref[rows, :].astype(BF16), preferred_element_type=F32))
    if chained:
        for s in range(T):
            for t in range(per):
                stage_ref[t, pl.ds(s, NC, stride=T), :] = y[s * NC:(s + 1) * NC, t * LANES:(t + 1) * LANES]
        for t in range(per):
            y_ref[:, t * LANES:(t + 1) * LANES] = stage_ref[t]
    else:
        y_ref[...] = y


def _s5(u, h0r, h0i, lam_re, lam_im, log_dt, bt_re, bt_im, ct_re, ct_im, d_row, n_seq, T, NC, chained):
    rows = NC * T
    gps = SLAB // SSM_GROUP
    slab_row = lambda: pl.BlockSpec((1, SLAB_STATES), lambda b, s: (0, s))
    if chained:
        st_spec = pl.BlockSpec((None, 1, SLAB_STATES), lambda b, s: (b, 0, s))
        st_shape = jax.ShapeDtypeStruct((n_seq, 1, N_SLABS * SLAB_STATES), F32)
        scratch = [pltpu.VMEM((NC, SLAB_STATES), F32)] * 4 + [
            pltpu.VMEM((SLAB // LANES, rows, LANES), F32), pltpu.VMEM((rows, SLAB), F32)]
    else:
        st_spec = pl.BlockSpec((NC, SLAB_STATES), lambda b, s: (0, s))
        st_shape = jax.ShapeDtypeStruct((NC, N_SLABS * SLAB_STATES), F32)
        scratch = []
    return pl.pallas_call(
        functools.partial(_s5_kernel, T=T, NC=NC, chained=chained),
        grid=(n_seq, N_SLABS),
        in_specs=[pl.BlockSpec((rows, SLAB), lambda b, s: (b, s)), st_spec, st_spec,
                  slab_row(), slab_row(), slab_row(),
                  pl.BlockSpec((gps, SSM_GROUP, SSM_STATE), lambda b, s: (s, 0, 0)),
                  pl.BlockSpec((gps, SSM_GROUP, SSM_STATE), lambda b, s: (s, 0, 0)),
                  pl.BlockSpec((gps, SSM_STATE, SSM_GROUP), lambda b, s: (s, 0, 0)),
                  pl.BlockSpec((gps, SSM_STATE, SSM_GROUP), lambda b, s: (s, 0, 0)),
                  pl.BlockSpec((1, SLAB), lambda b, s: (0, s))],
        out_specs=[pl.BlockSpec((rows, SLAB), lambda b, s: (b, s)), st_spec, st_spec],
        out_shape=[jax.ShapeDtypeStruct((n_seq * rows, D_SSM), F32), st_shape, st_shape],
        scratch_shapes=[pltpu.VMEM((SLAB_STATES // LANES, rows, LANES), F32)] * 2
        + [pltpu.VMEM((SLAB, SLAB_STATES), F32)] * 2 + [pltpu.VMEM((SLAB_STATES, SLAB), F32)] * 2 + scratch,
        compiler_params=_params(48, 2), name="s5",
    )(u, h0r, h0i, lam_re, lam_im, log_dt, bt_re, bt_im, ct_re, ct_im, d_row)


def _mixout_kernel(x_ref, att_ref, y_ref, wg_ref, bg_ref, wo_ref, o_ref):
    y = y_ref[...]
    z = 0.5 * y * (1.0 + jnp.tanh(math.sqrt(2.0 / math.pi) * (y + 0.044715 * (y * y * y))))
    gate = _sigmoid(jnp.dot(z.astype(BF16), wg_ref[...], preferred_element_type=F32) + bg_ref[...])
    ssm = (z * gate).astype(BF16)
    o_ref[...] = (x_ref[...] + jnp.dot(att_ref[...], wo_ref[0:D_ATTN, :], preferred_element_type=F32)
                  + jnp.dot(ssm, wo_ref[D_ATTN:, :], preferred_element_type=F32))


def _mixout(x, att, y, wg, bg, wo):
    R = x.shape[0]
    tm = min(R, 512)
    row = lambda n: pl.BlockSpec((tm, n), lambda i: (i, 0))
    full = lambda a: pl.BlockSpec(a.shape, lambda i: (0,) * a.ndim, pipeline_mode=pl.Buffered(1))
    return pl.pallas_call(
        _mixout_kernel, grid=(R // tm,),
        in_specs=[row(D_MODEL), row(D_ATTN), row(D_SSM), full(wg), full(bg), full(wo)],
        out_specs=row(D_MODEL), out_shape=jax.ShapeDtypeStruct((R, D_MODEL), F32),
        compiler_params=_params(56, 1), name="mixout",
    )(x, att, y, wg, bg, wo)


def _ffn_kernel(xa_ref, xb_ref, nw_ref, w1_ref, w3_ref, w2_ref, oa_ref, ob_ref, h_ref):
    ta = xa_ref.shape[0]

    @pl.when(pl.program_id(1) == 0)
    def _():
        for x_ref, o_ref, rows in ((xa_ref, oa_ref, slice(0, ta)), (xb_ref, ob_ref, slice(ta, None))):
            x = x_ref[...]
            h_ref[rows, :] = _rms(x, nw_ref[...]).astype(BF16)
            o_ref[...] = x

    h = h_ref[...]
    for c in range(w1_ref.shape[1] // MXU_DIM):
        cols = slice(c * MXU_DIM, (c + 1) * MXU_DIM)
        a = jnp.dot(h, w1_ref[:, cols].astype(BF16), preferred_element_type=F32)
        b = jnp.dot(h, w3_ref[:, cols].astype(BF16), preferred_element_type=F32)
        g = (a * _sigmoid(a) * b).astype(BF16)
        part = jnp.dot(g, w2_ref[cols, :].astype(BF16), preferred_element_type=F32)
        oa_ref[...] += part[:ta]
        ob_ref[...] += part[ta:]


def _ffn(xa, xb, nw, w1, w3, w2, layer):
    Ra, Rb = xa.shape[0], xb.shape[0]
    d_ff = w1.shape[2]
    ta = min(Ra, 1024)
    n_tiles = Ra // ta
    tb = Rb // n_tiles
    tf = 2 * MXU_DIM
    assert d_ff % tf == 0 and Ra % ta == 0 and Rb % n_tiles == 0 and tb % 16 == 0
    rows = lambda t: pl.BlockSpec((t, D_MODEL), lambda i, f: (i, 0))
    return pl.pallas_call(
        _ffn_kernel, grid=(n_tiles, d_ff // tf),
        in_specs=[pl.BlockSpec((ta, D_MODEL), lambda i, f: (i, 0), pipeline_mode=pl.Buffered(1)),
                  pl.BlockSpec((tb, D_MODEL), lambda i, f: (i, 0), pipeline_mode=pl.Buffered(1)),
                  pl.BlockSpec((1, D_MODEL), lambda i, f: (0, 0)),
                  pl.BlockSpec((None, D_MODEL, tf), lambda i, f: (layer, 0, f)),
                  pl.BlockSpec((None, D_MODEL, tf), lambda i, f: (layer, 0, f)),
                  pl.BlockSpec((None, tf, D_MODEL), lambda i, f: (layer, f, 0))],
        out_specs=[rows(ta), rows(tb)],
        out_shape=[jax.ShapeDtypeStruct((Ra, D_MODEL), F32), jax.ShapeDtypeStruct((Rb, D_MODEL), F32)],
        scratch_shapes=[pltpu.VMEM((ta + tb, D_MODEL), BF16)],
        compiler_params=_params(60, 2), name="ffn",
    )(xa, xb, nw, w1, w3, w2)


SUBLANES = 8
HIST = SUBLANES * (max(POOL_WINDOWS).bit_length() - 1)


def _pool_prompt_kernel(x_ref, nw_ref, pw_ref, ps_ref, o_ref, st_ref, ext_ref, lvl_ref, *, tm, tiles_per_seq):
    i = pl.program_id(0)
    t_in_seq = i % tiles_per_seq
    n = HIST + tm

    @pl.when(t_in_seq == 0)
    def _():
        ext_ref[0:HIST, :] = jnp.zeros((HIST, D_MODEL), F32)

    x = x_ref[...]
    hp = _rms(x, nw_ref[...])
    ext_ref[HIST:n, :] = hp
    pos = t_in_seq * tm + lax.broadcasted_iota(jnp.int32, (tm, 1), 0)
    for g, w in enumerate(POOL_WINDOWS):
        cols = slice(g * POOL_GROUP, (g + 1) * POOL_GROUP)
        n_levels = w.bit_length() - 1
        assert w == 1 << n_levels and SUBLANES * n_levels <= HIST
        acc = None
        for k in range(n_levels):
            start, sh = SUBLANES * (k + 1), 1 << k
            if k == 0:
                acc = ext_ref[start:n, cols] + ext_ref[start - sh:n - sh, cols]
            else:
                acc = lvl_ref[k - 1, start:n, :] + lvl_ref[k - 1, start - sh:n - sh, :]
            if k < n_levels - 1:
                lvl_ref[k, start:n, :] = acc
        acc = acc[HIST - SUBLANES * n_levels:, :]
        inv = 1.0 / jnp.minimum(pos + 1, w).astype(F32)
        pooled = (acc * inv - hp[:, cols]).astype(BF16)
        o_ref[:, cols] = x[:, cols] + jnp.dot(pooled, pw_ref[g], preferred_element_type=F32) * ps_ref[:, cols]
    st_ref[...] = ext_ref[tm + HIST - POOL_BUF:tm + HIST, :]
    ext_ref[0:HIST, :] = ext_ref[tm:tm + HIST, :]


def _pool_prompt(x, nw, pw, ps, B, L):
    tm = min(L, 512)
    assert L % tm == 0 and tm >= HIST
    tiles_per_seq = L // tm
    return pl.pallas_call(
        functools.partial(_pool_prompt_kernel, tm=tm, tiles_per_seq=tiles_per_seq),
        grid=(B * tiles_per_seq,),
        in_specs=[pl.BlockSpec((tm, D_MODEL), lambda i: (i, 0)),
                  pl.BlockSpec((1, D_MODEL), lambda i: (0, 0)),
                  pl.BlockSpec(pw.shape, lambda i: (0, 0, 0)),
                  pl.BlockSpec((1, D_MODEL), lambda i: (0, 0))],
        out_specs=[pl.BlockSpec((tm, D_MODEL), lambda i: (i, 0)),
                   pl.BlockSpec((None, POOL_BUF, D_MODEL), lambda i: (i // tiles_per_seq, 0, 0))],
        out_shape=[jax.ShapeDtypeStruct((B * L, D_MODEL), F32), jax.ShapeDtypeStruct((B, POOL_BUF, D_MODEL), F32)],
        scratch_shapes=[pltpu.VMEM((HIST + tm, D_MODEL), F32),
                        pltpu.VMEM((HIST // SUBLANES - 1, HIST + tm, POOL_GROUP), F32)],
        compiler_params=_params(40, 1), name="pool_prompt",
    )(x, nw, pw, ps)


def _pool_sample(x, nw, sp, pw, ps, Bd, Q, pos0):
    G = len(POOL_WINDOWS)
    hist = [pl.BlockSpec((Bd, POOL_GROUP), functools.partial(lambda g, j: (0, j * G + g), j=j)) for j in range(POOL_BUF)]
    col = pl.BlockSpec((Q * Bd, POOL_GROUP), lambda g: (0, g))
    return pl.pallas_call(
        functools.partial(_pool_sample_kernel, Bd=Bd, Q=Q, pos0=pos0),
        grid=(G,),
        in_specs=[pl.BlockSpec((Q * Bd, D_MODEL), lambda g: (0, 0)),
                  pl.BlockSpec((1, D_MODEL), lambda g: (0, 0)),
                  pl.BlockSpec((None, POOL_GROUP, POOL_GROUP), lambda g: (g, 0, 0)),
                  pl.BlockSpec((1, POOL_GROUP), lambda g: (0, g))] + hist,
        out_specs=[col, col],
        out_shape=[jax.ShapeDtypeStruct((Q * Bd, D_MODEL), F32)] * 2,
        scratch_shapes=[pltpu.VMEM((Q * Bd, 1), F32)],
        compiler_params=_params(40, 1), name="pool_sample",
    )(x, nw, pw, ps, *([sp] * POOL_BUF))


def _pool_sample_kernel(x_ref, nw_ref, pw_ref, ps_ref, *rest, Bd, Q, pos0):
    hist_refs = rest[:POOL_BUF]
    o_ref, hs_ref, inv_ref = rest[POOL_BUF:]
    g = pl.program_id(0)

    @pl.when(g == 0)
    def _():
        x = x_ref[...]
        inv_ref[...] = lax.rsqrt(jnp.mean(x * x, axis=-1, keepdims=True) + RMS_EPS)

    for wi, w in enumerate(POOL_WINDOWS):
        @pl.when(g == wi)
        def _(wi=wi, w=w):
            cols = slice(wi * POOL_GROUP, (wi + 1) * POOL_GROUP)
            xs, hs = [], []
            for q in range(Q):
                rows = slice(q * Bd, (q + 1) * Bd)
                xq = x_ref[rows, cols]
                xs.append(xq)
                hs.append(xq * inv_ref[rows, :] * nw_ref[:, cols])
                hs_ref[rows, :] = hs[q]
            ext = lambda j: hist_refs[j][...] if j < POOL_BUF else hs[j - POOL_BUF]
            for q in range(Q):
                acc = hs[q]
                for k in range(1, w):
                    acc = acc + ext(POOL_BUF + q - k)
                cnt = float(min(pos0 + q + 1, w))
                pooled = (acc / cnt - hs[q]).astype(BF16)
                val = jnp.dot(pooled, pw_ref[...], preferred_element_type=F32) * ps_ref[...]
                o_ref[q * Bd:(q + 1) * Bd, :] = xs[q] + val


def kernel(x_prompt, x_sample, cache_k, cache_v, cache_logf, state_s5_re, state_s5_im, state_pool, page_table, norm_mix_w, norm_ffn_w, w_in, b_f, q_norm_w, k_norm_w, s5_lam_re, s5_lam_im, s5_log_dt, s5_b_re, s5_b_im, s5_c_re, s5_c_im, s5_d, w_glu, b_glu, w_out, pool_w, pool_scale, ffn_w1, ffn_w3, ffn_w2):
    B, L, D = x_prompt.shape
    Bd, Q, _ = x_sample.shape
    depth = norm_mix_w.shape[0]
    n_attn, n_phys, page = cache_k.shape[:3]
    n_pages = page_table.shape[1]
    past_len = n_pages * page
    T = 16
    assert D == D_MODEL and L % T == 0

    xp = x_prompt.reshape(B * L, D)
    xs = jnp.swapaxes(x_sample, 0, 1).reshape(Q * Bd, D)
    from_tm = lambda a, *tail: jnp.swapaxes(a.reshape(Q, Bd, *tail), 0, 1)
    pt_flat = page_table.reshape(-1)
    assert (page * N_HEADS) % LANES == 0
    ck_flat = cache_k.reshape(n_attn * n_phys, page, N_HEADS, HEAD_DIM)
    cv_flat = cache_v.reshape(n_attn * n_phys, page, N_HEADS, HEAD_DIM)
    clf_flat = cache_logf.reshape(n_attn * n_phys, page * N_HEADS // LANES, LANES)

    kp, vp, fp, srp, sip, plp = [], [], [], [], [], []
    ksm, vsm, fsm, srs, sis, pls = [], [], [], [], [], []
    for layer in range(depth):
        i = layer // 2
        nw = norm_mix_w[layer].reshape(1, D)
        if layer % 2 == 0:
            w = w_in[i]
            wq = w[:, :D_ATTN].astype(BF16)
            wk = w[:, D_ATTN:2 * D_ATTN].astype(BF16)
            wv = w[:, 2 * D_ATTN:3 * D_ATTN].astype(BF16)
            wf = jnp.pad(w[:, 3 * D_ATTN:3 * D_ATTN + N_HEADS], ((0, 0), (0, LANES - N_HEADS))).astype(BF16)
            wu = w[:, 3 * D_ATTN + N_HEADS:].astype(BF16)
            bf = jnp.pad(b_f[i], (0, LANES - N_HEADS)).reshape(1, LANES)
            qg = q_norm_w[i].reshape(1, HEAD_DIM)
            kg = k_norm_w[i].reshape(1, HEAD_DIM)
            wg = w_glu[i].astype(BF16)
            bg = b_glu[i].reshape(1, D_SSM)
            wo = w_out[i].astype(BF16)
            lam_re = s5_lam_re[i].reshape(1, -1)
            lam_im = s5_lam_im[i].reshape(1, -1)
            log_dt = jnp.broadcast_to(s5_log_dt[i][:, None], (N_SSM_GROUPS, SSM_STATE)).reshape(1, -1)
            bd_re = jnp.swapaxes(s5_b_re[i], 1, 2)
            bd_im = jnp.swapaxes(s5_b_im[i], 1, 2)
            cd_re = jnp.swapaxes(s5_c_re[i], 1, 2)
            cd_im = jnp.swapaxes(s5_c_im[i], 1, 2)
            d_row = s5_d[i].reshape(1, D_SSM)

            qb, k, kb, v, vb, u, lf, c = _inproj(xp, nw, wq, wk, wv, wu, wf, bf, qg, kg, seq_len=L,
                                                 q_scale=HEAD_DIM ** -0.5 * math.log2(math.e))
            c_t = jnp.swapaxes(c[:, :N_HEADS].reshape(B, L, N_HEADS), 1, 2)
            att = _attn_prompt(qb, kb, vb, c, c_t[:, :, None, :], B, L)
            zeros = jnp.zeros((B, 1, N_SLABS * SLAB_STATES), F32)
            y, h_re, h_im = _s5(u, zeros, zeros, lam_re, lam_im, log_dt, bd_re, bd_im, cd_re, cd_im, d_row,
                                n_seq=B, T=T, NC=L // T, chained=True)
            xp = _mixout(xp, att, y, wg, bg, wo)
            kp.append(k.reshape(B, L, N_HEADS, HEAD_DIM))
            vp.append(v.reshape(B, L, N_HEADS, HEAD_DIM))
            fp.append(lf[:, :N_HEADS].reshape(B, L, N_HEADS))
            srp.append(h_re.reshape(B, N_SSM_GROUPS, SSM_STATE))
            sip.append(h_im.reshape(B, N_SSM_GROUPS, SSM_STATE))

            qb, k, kb, v, vb, u, lf = _inproj(xs, nw, wq, wk, wv, wu, wf, bf, qg, kg, seq_len=None,
                                              q_scale=HEAD_DIM ** -0.5)
            lfn = lf[:, :N_HEADS].reshape(Q, Bd, N_HEADS)
            heads = lambda a: a.astype(F32).reshape(Q, Bd, N_HEADS, HEAD_DIM)
            att = _attn_sample(pt_flat, heads(qb), heads(k), heads(v),
                               jnp.swapaxes(lfn, 0, 1).reshape(Bd, Q * N_HEADS, 1),
                               ck_flat, cv_flat, clf_flat, i, n_phys, n_pages, Bd, Q).astype(BF16)
            y, h_re, h_im = _s5(u, state_s5_re[i].reshape(Bd, -1), state_s5_im[i].reshape(Bd, -1),
                                lam_re, lam_im, log_dt, bd_re, bd_im, cd_re, cd_im, d_row,
                                n_seq=1, T=Q, NC=Bd, chained=False)
            xs = _mixout(xs, att.reshape(Q * Bd, D_ATTN), y, wg, bg, wo)
            ksm.append(from_tm(k, N_HEADS, HEAD_DIM))
            vsm.append(from_tm(v, N_HEADS, HEAD_DIM))
            fsm.append(jnp.swapaxes(lfn, 0, 1))
            srs.append(h_re.reshape(Bd, N_SSM_GROUPS, SSM_STATE))
            sis.append(h_im.reshape(Bd, N_SSM_GROUPS, SSM_STATE))
        else:
            pw = pool_w[i].astype(BF16)
            ps = pool_scale[i].reshape(1, D)
            xp, st = _pool_prompt(xp, nw, pw, ps, B, L)
            plp.append(st)
            xs, hs = _pool_sample(xs, nw, state_pool[i].reshape(Bd, POOL_BUF * D), pw, ps, Bd, Q, past_len)
            pls.append(jnp.concatenate([state_pool[i], from_tm(hs, D)], axis=1)[:, -POOL_BUF:])
        nf = norm_ffn_w[layer].reshape(1, D)
        xp, xs = _ffn(xp, xs, nf, ffn_w1, ffn_w3, ffn_w2, layer)
    return (xp.reshape(B, L, D), from_tm(xs, D),
            jnp.stack(kp), jnp.stack(vp), jnp.stack(fp), jnp.stack(srp), jnp.stack(sip), jnp.stack(plp),
            jnp.stack(ksm), jnp.stack(vsm), jnp.stack(fsm), jnp.stack(srs), jnp.stack(sis), jnp.stack(pls))
```

```python
import functools
import math

import jax
import jax.numpy as jnp
from jax import lax
from jax.experimental import pallas as pl
from jax.experimental.pallas import tpu as pltpu

F32 = jnp.float32
BF16 = jnp.bfloat16

D_MODEL = 2048
D_ATTN = D_MODEL // 2
HEAD_DIM = 128
N_HEADS = D_ATTN // HEAD_DIM
D_SSM = D_MODEL - D_ATTN
D_IN = 3 * D_ATTN + N_HEADS + D_SSM
SSM_GROUP = 16
N_SSM_GROUPS = D_SSM // SSM_GROUP
SSM_STATE = 64
POOL_WINDOWS = (2, 4, 8, 16)
POOL_GROUP = D_MODEL // len(POOL_WINDOWS)
POOL_BUF = max(POOL_WINDOWS) - 1
RMS_EPS = 1e-6

LANES = 128
MXU_DIM = 256
SLAB = MXU_DIM
SLAB_STATES = (SLAB // SSM_GROUP) * SSM_STATE
N_SLABS = D_SSM // SLAB
VMEM_BYTES = 64 * 1024 * 1024


def _params(vmem_mb, n_axes):
    return pltpu.CompilerParams(dimension_semantics=("arbitrary",) * n_axes,
                                vmem_limit_bytes=min(vmem_mb * 1024 * 1024, VMEM_BYTES - 4 * 1024 * 1024))


def _rms(x, w):
    return x * lax.rsqrt(jnp.mean(x * x, axis=-1, keepdims=True) + RMS_EPS) * w


def _log_sigmoid(x):
    return jnp.minimum(x, 0.0) - jnp.log1p(jnp.exp(-jnp.abs(x)))


def _sigmoid(x):
    return 1.0 / (1.0 + jnp.exp(-x))


def _split3(x):
    hi = x.astype(BF16)
    r = x - hi.astype(F32)
    mid = r.astype(BF16)
    lo = (r - mid.astype(F32)).astype(BF16)
    return hi, mid, lo


def _cumsum_rows(x):
    n = x.shape[0]
    tri = (lax.broadcasted_iota(jnp.int32, (n, n), 1) <= lax.broadcasted_iota(jnp.int32, (n, n), 0)).astype(BF16)
    hi, mid, lo = _split3(x)
    z = jnp.dot(tri, jnp.concatenate([hi, mid, lo], axis=1), preferred_element_type=F32)
    return z[:, :LANES] + z[:, LANES:2 * LANES] + z[:, 2 * LANES:]


def _split_w_in_kernel(w_ref, tail_ref, wq_ref, wk_ref, wv_ref, wu_ref, wf_ref):
    for j, ref in enumerate((wq_ref, wk_ref, wv_ref)):
        ref[...] = w_ref[:, j * D_ATTN:(j + 1) * D_ATTN].astype(BF16)
    base = 3 * D_ATTN
    lane = lax.broadcasted_iota(jnp.int32, (w_ref.shape[0], LANES), 1)
    wf_ref[...] = jnp.where(lane < N_HEADS, w_ref[:, base:base + LANES], 0.0).astype(BF16)
    n_blk = D_SSM // LANES
    for j in range(n_blk):
        a = w_ref[:, base + j * LANES:base + (j + 1) * LANES]
        b = tail_ref[...] if j == n_blk - 1 else w_ref[:, base + (j + 1) * LANES:base + (j + 2) * LANES]
        shifted = jnp.where(lane < LANES - N_HEADS, pltpu.roll(a, LANES - N_HEADS, axis=1),
                            pltpu.roll(b, LANES - N_HEADS, axis=1))
        wu_ref[:, j * LANES:(j + 1) * LANES] = shifted.astype(BF16)


def _split_w_in(w):
    D, d_in = w.shape
    assert d_in == D_IN
    tr = 256
    tail = jnp.pad(w[:, 3 * D_ATTN + D_SSM:], ((0, 0), (0, LANES - N_HEADS)))
    out = lambda n: pl.BlockSpec((tr, n), lambda i: (i, 0))
    return pl.pallas_call(
        _split_w_in_kernel, grid=(D // tr,),
        in_specs=[out(d_in), out(LANES)],
        out_specs=[out(D_ATTN)] * 3 + [out(D_SSM), out(LANES)],
        out_shape=[jax.ShapeDtypeStruct((D, D_ATTN), BF16)] * 3
        + [jax.ShapeDtypeStruct((D, D_SSM), BF16), jax.ShapeDtypeStruct((D, LANES), BF16)],
        compiler_params=_params(32, 1), name="split_w_in",
    )(w, tail)


def _inproj_kernel(x_ref, nw_ref, wq_ref, wk_ref, wv_ref, wu_ref, wf_ref, bf_ref, qg_ref, kg_ref,
                   q_ref, k_ref, kb_ref, v_ref, vb_ref, u_ref, lf_ref, *rest, tiles_per_seq, scale):
    h = _rms(x_ref[...], nw_ref[...]).astype(BF16)
    for j in range(D_ATTN // MXU_DIM):
        cols = slice(j * MXU_DIM, (j + 1) * MXU_DIM)
        zq = jnp.dot(h, wq_ref[:, cols], preferred_element_type=F32)
        zk = jnp.dot(h, wk_ref[:, cols], preferred_element_type=F32)
        zv = jnp.dot(h, wv_ref[:, cols], preferred_element_type=F32)
        for t in range(MXU_DIM // HEAD_DIM):
            sub = slice(t * HEAD_DIM, (t + 1) * HEAD_DIM)
            dst = slice(j * MXU_DIM + t * HEAD_DIM, j * MXU_DIM + (t + 1) * HEAD_DIM)
            q_ref[:, dst] = (_rms(zq[:, sub], qg_ref[...]) * scale).astype(BF16)
            kn = _rms(zk[:, sub], kg_ref[...])
            k_ref[:, dst] = kn
            kb_ref[:, dst] = kn.astype(BF16)
        v_ref[:, cols] = zv
        vb_ref[:, cols] = zv.astype(BF16)
        u_ref[:, cols] = jnp.dot(h, wu_ref[:, cols], preferred_element_type=F32)
    lf = _log_sigmoid(jnp.dot(h, wf_ref[...], preferred_element_type=F32) + bf_ref[...])
    lf_ref[...] = lf
    if tiles_per_seq is not None:
        c_ref, carry_ref = rest

        @pl.when(pl.program_id(0) % tiles_per_seq == 0)
        def _():
            carry_ref[...] = jnp.zeros_like(carry_ref)

        c = _cumsum_rows(lf) + carry_ref[...]
        c_ref[...] = c
        carry_ref[...] = c[c.shape[0] - 1:, :]


def _inproj(x, nw, wq, wk, wv, wu, wf, bf, qg, kg, seq_len, q_scale):
    R = x.shape[0]
    tm = min(R, 512)
    row = lambda n: pl.BlockSpec((tm, n), lambda i: (i, 0))
    full = lambda a: pl.BlockSpec(a.shape, lambda i: (0,) * a.ndim, pipeline_mode=pl.Buffered(1))
    out_shape = [jax.ShapeDtypeStruct((R, D_ATTN), BF16), jax.ShapeDtypeStruct((R, D_ATTN), F32),
                 jax.ShapeDtypeStruct((R, D_ATTN), BF16), jax.ShapeDtypeStruct((R, D_ATTN), F32),
                 jax.ShapeDtypeStruct((R, D_ATTN), BF16), jax.ShapeDtypeStruct((R, D_SSM), F32),
                 jax.ShapeDtypeStruct((R, LANES), F32)]
    out_specs = [row(D_ATTN)] * 5 + [row(D_SSM), row(LANES)]
    scratch = []
    tiles_per_seq = None
    if seq_len is not None:
        assert seq_len % tm == 0
        tiles_per_seq = seq_len // tm
        out_shape.append(jax.ShapeDtypeStruct((R, LANES), F32))
        out_specs.append(row(LANES))
        scratch.append(pltpu.VMEM((1, LANES), F32))
    return pl.pallas_call(
        functools.partial(_inproj_kernel, tiles_per_seq=tiles_per_seq, scale=q_scale),
        grid=(R // tm,),
        in_specs=[row(D_MODEL), full(nw), full(wq), full(wk), full(wv), full(wu), full(wf), full(bf), full(qg), full(kg)],
        out_specs=out_specs, out_shape=out_shape, scratch_shapes=scratch,
        compiler_params=_params(60, 1), name="inproj",
    )(x, nw, wq, wk, wv, wu, wf, bf, qg, kg)


def _attn_prompt_kernel(q_ref, k_ref, v_ref, c_ref, ck_ref, o_ref, *, L, tq):
    nt = (((1,), (1,)), ((), ()))
    log2e = math.log2(math.e)
    head = lax.broadcasted_iota(jnp.int32, (tq, LANES), 1) == pl.program_id(1)
    for qi in range(L // tq):
        rows = slice(qi * tq, (qi + 1) * tq)
        q = q_ref[rows, :]
        cq = jnp.sum(jnp.where(head, c_ref[rows, :], 0.0), axis=-1, keepdims=True) * log2e
        m = l = acc = None
        for ki in range(qi + 1):
            cols = slice(ki * tq, (ki + 1) * tq)
            t = lax.dot_general(q, k_ref[cols, :], nt, preferred_element_type=F32) - ck_ref[:, cols] * log2e
            if ki == qi:
                keep = lax.broadcasted_iota(jnp.int32, (tq, tq), 1) <= lax.broadcasted_iota(jnp.int32, (tq, tq), 0)
                t = jnp.where(keep, t, -jnp.inf)
            mx = jnp.max(t, axis=-1, keepdims=True) + cq
            if ki == 0:
                m = mx
                p = jnp.exp2(t - (m - cq))
                l = jnp.sum(p, axis=-1, keepdims=True)
                acc = jnp.dot(p.astype(BF16), v_ref[cols, :], preferred_element_type=F32)
            else:
                m_new = jnp.maximum(m, mx)
                alpha = jnp.exp2(m - m_new)
                p = jnp.exp2(t - (m_new - cq))
                l = alpha * l + jnp.sum(p, axis=-1, keepdims=True)
                acc = alpha * acc + jnp.dot(p.astype(BF16), v_ref[cols, :], preferred_element_type=F32)
                m = m_new
        o_ref[rows, :] = (acc / l).astype(BF16)


def _attn_prompt(qb, kb, vb, c, ck, B, L):
    tq = min(L, 512)
    blk = pl.BlockSpec((L, HEAD_DIM), lambda b, h: (b, h))
    return pl.pallas_call(
        functools.partial(_attn_prompt_kernel, L=L, tq=tq),
        grid=(B, N_HEADS),
        in_specs=[blk, blk, blk,
                  pl.BlockSpec((L, LANES), lambda b, h: (b, 0)),
                  pl.BlockSpec((None, None, 1, L), lambda b, h: (b, h, 0, 0))],
        out_specs=blk, out_shape=jax.ShapeDtypeStruct((B * L, D_ATTN), BF16),
        compiler_params=_params(32, 2), name="attn_prompt",
    )(qb, kb, vb, c, ck)


def _attn_sample_kernel(pt_ref, q_ref, kn_ref, vn_ref, lfn_ref, *rest, n_pages, Q):
    k_refs = rest[:n_pages]
    v_refs = rest[n_pages:2 * n_pages]
    lf_refs = rest[2 * n_pages:3 * n_pages]
    o_ref, s_ref, d_ref = rest[3 * n_pages:]
    nt = (((1,), (1,)), ((), ()))
    H = N_HEADS
    M = Q * H
    page = k_refs[0].shape[0]
    PL = page * H
    RPP = PL // LANES

    qm = q_ref[...].reshape(M, HEAD_DIM)
    qmb = qm.astype(BF16)
    same_head = (lax.broadcasted_iota(jnp.int32, (M, LANES), 1) % H
                 == lax.broadcasted_iota(jnp.int32, (M, LANES), 0) % H)

    x = jnp.concatenate([r[...] for r in lf_refs], axis=0)
    lane = lax.broadcasted_iota(jnp.int32, x.shape, 1)
    sh = H
    while sh < LANES:
        x = x + jnp.where(lane >= sh, pltpu.roll(x, sh, axis=1), 0.0)
        sh *= 2
    tot = jnp.where(lane >= LANES - H, x, 0.0)
    inc = _cumsum_rows(tot)
    after = inc[inc.shape[0] - 1:, :] - inc
    sh = H
    while sh < LANES:
        after = after + pltpu.roll(after, sh, axis=1)
        sh *= 2
    row_tot = tot
    sh = H
    while sh < LANES:
        row_tot = row_tot + pltpu.roll(row_tot, sh, axis=1)
        sh *= 2
    d_ref[...] = after + (row_tot - x)

    lfn = lfn_ref[...]
    cum = [lfn[0:H, :]]
    for i in range(1, Q):
        cum.append(cum[-1] + lfn[i * H:(i + 1) * H, :])
    cum_q = jnp.concatenate(cum, axis=0)
    cum_k = jnp.concatenate([jnp.concatenate([c] * Q, axis=0) for c in cum], axis=1)

    qm_r = qmb.astype(F32)
    tile_q = lambda a: jnp.concatenate([a.astype(BF16).astype(F32)] * Q, axis=0)
    s_new = jnp.concatenate(
        [jnp.sum(qm_r * tile_q(kn_ref[i]), axis=-1, keepdims=True) for i in range(Q)], axis=1)
    s_new = s_new + (cum_q - cum_k)
    q_of_row = lax.broadcasted_iota(jnp.int32, (M, Q), 0) // H
    s_new = jnp.where(lax.broadcasted_iota(jnp.int32, (M, Q), 1) <= q_of_row, s_new, -jnp.inf)
    m = jnp.max(s_new, axis=-1, keepdims=True)

    for j in range(n_pages):
        kb = k_refs[j][...].reshape(PL, HEAD_DIM).astype(BF16)
        s = lax.dot_general(qmb, kb, nt, preferred_element_type=F32)
        for r in range(RPP):
            blk = s[:, r * LANES:(r + 1) * LANES] + (cum_q + d_ref[j * RPP + r:j * RPP + r + 1, :])
            blk = jnp.where(same_head, blk, -jnp.inf)
            s_ref[:, (j * RPP + r) * LANES:(j * RPP + r + 1) * LANES] = blk
            m = jnp.maximum(m, jnp.max(blk, axis=-1, keepdims=True))

    p_new = jnp.exp(s_new - m)
    l = jnp.sum(p_new, axis=-1, keepdims=True)
    p_new_r = p_new.astype(BF16).astype(F32)
    acc = p_new_r[:, 0:1] * tile_q(vn_ref[0])
    for i in range(1, Q):
        acc = acc + p_new_r[:, i:i + 1] * tile_q(vn_ref[i])
    for j in range(n_pages):
        p = jnp.exp(s_ref[:, j * PL:(j + 1) * PL] - m)
        l = l + jnp.sum(p, axis=-1, keepdims=True)
        vb = v_refs[j][...].reshape(PL, HEAD_DIM).astype(BF16)
        acc = acc + jnp.dot(p.astype(BF16), vb, preferred_element_type=F32)
    o_ref[...] = (acc / l).reshape(Q, H, HEAD_DIM)


def _attn_sample(page_table_flat, q4, k4, v4, lfn_col, cache_k, cache_v, cache_lf, layer, n_phys, n_pages, Bd, Q):
    page = cache_k.shape[1]
    base = layer * n_phys
    rpp = page * N_HEADS // LANES

    def paged(tail):
        return [pl.BlockSpec((None,) + tail, functools.partial(
            lambda b, pt, j: (pt[b * n_pages + j] + base,) + (0,) * len(tail), j=j)) for j in range(n_pages)]

    tok = pl.BlockSpec((Q, None, N_HEADS, HEAD_DIM), lambda b, pt: (0, b, 0, 0))
    grid_spec = pltpu.PrefetchScalarGridSpec(
        num_scalar_prefetch=1, grid=(Bd,),
        in_specs=[tok, tok, tok, pl.BlockSpec((None, Q * N_HEADS, 1), lambda b, pt: (b, 0, 0))]
        + paged((page, N_HEADS, HEAD_DIM)) + paged((page, N_HEADS, HEAD_DIM)) + paged((rpp, LANES)),
        out_specs=tok,
        scratch_shapes=[pltpu.VMEM((Q * N_HEADS, n_pages * page * N_HEADS), F32),
                        pltpu.VMEM((n_pages * rpp, LANES), F32)])
    return pl.pallas_call(
        functools.partial(_attn_sample_kernel, n_pages=n_pages, Q=Q),
        grid_spec=grid_spec, out_shape=jax.ShapeDtypeStruct((Q, Bd, N_HEADS, HEAD_DIM), F32),
        compiler_params=_params(56, 1), name="attn_sample",
    )(page_table_flat, q4, k4, v4, lfn_col, *([cache_k] * n_pages), *([cache_v] * n_pages), *([cache_lf] * n_pages))


def _s5_kernel(u_ref, h0r_ref, h0i_ref, lr_ref, li_ref, ldt_ref, btr_ref, bti_ref, ctr_ref, cti_ref, d_ref,
               y_ref, fr_ref, fi_ref, xr_ref, xi_ref, bdr_ref, bdi_ref, cdr_ref, cdi_ref, *scratch, T, NC, chained):
    @pl.when((pl.program_id(0) == 0) & (pl.program_id(1) == 0))
    def _():
        for ref in (bdr_ref, bdi_ref, cdr_ref, cdi_ref):
            ref[...] = jnp.zeros_like(ref)

    for g in range(SLAB // SSM_GROUP):
        rows_c, rows_p = slice(g * SSM_GROUP, (g + 1) * SSM_GROUP), slice(g * SSM_STATE, (g + 1) * SSM_STATE)
        bdr_ref[rows_c, rows_p] = btr_ref[g]
        bdi_ref[rows_c, rows_p] = bti_ref[g]
        cdr_ref[rows_p, rows_c] = ctr_ref[g]
        cdi_ref[rows_p, rows_c] = cti_ref[g]

    lr, li = lr_ref[...], li_ref[...]
    dt = jnp.exp(ldt_ref[...])
    mag = jnp.exp(lr * dt)
    ar, ai = mag * jnp.cos(li * dt), mag * jnp.sin(li * dt)
    den = lr * lr + li * li
    zr = ((ar - 1.0) * lr + ai * li) / den
    zi = (ai * lr - (ar - 1.0) * li) / den
    bdr, bdi = bdr_ref[...], bdi_ref[...]
    pr = (zr * bdr - zi * bdi).astype(BF16)
    pi = (zr * bdi + zi * bdr).astype(BF16)
    n_cb = SLAB_STATES // LANES
    per = MXU_DIM // LANES
    if chained:
        stage_ref, perm_ref = scratch[4:]
        for t in range(per):
            stage_ref[t] = u_ref[:, t * LANES:(t + 1) * LANES]
        for s in range(T):
            for t in range(per):
                perm_ref[s * NC:(s + 1) * NC, t * LANES:(t + 1) * LANES] = stage_ref[t, pl.ds(s, NC, stride=T), :]
        u = perm_ref[...]
    else:
        u = u_ref[...]
    ub = u.astype(BF16)
    for c2 in range(n_cb // per):
        cols = slice(c2 * MXU_DIM, (c2 + 1) * MXU_DIM)
        x_r = jnp.dot(ub, pr[:, cols], preferred_element_type=F32)
        x_i = jnp.dot(ub, pi[:, cols], preferred_element_type=F32)
        for t in range(per):
            xr_ref[c2 * per + t] = x_r[:, t * LANES:(t + 1) * LANES]
            xi_ref[c2 * per + t] = x_i[:, t * LANES:(t + 1) * LANES]

    def sweep(cb, hr, hi, store):
        cols = slice(cb * LANES, (cb + 1) * LANES)
        a_r, a_i = ar[:, cols], ai[:, cols]
        for s in range(T):
            rows = slice(s * NC, (s + 1) * NC)
            xr, xi = xr_ref[cb, rows, :], xi_ref[cb, rows, :]
            hr, hi = a_r * hr - a_i * hi + xr, a_r * hi + a_i * hr + xi
            if store:
                xr_ref[cb, rows, :] = hr
                xi_ref[cb, rows, :] = hi
        return hr, hi

    if chained:
        er_ref, ei_ref, gr_ref, gi_ref = scratch[:4]
        zero = jnp.zeros((NC, LANES), F32)
        for cb in range(n_cb):
            cols = slice(cb * LANES, (cb + 1) * LANES)
            er_ref[:, cols], ei_ref[:, cols] = sweep(cb, zero, zero, False)
        tr, ti = ar, ai
        for _ in range(T - 1):
            tr, ti = tr * ar - ti * ai, tr * ai + ti * ar

        def chunk_step(k, carry):
            sr, si = carry
            gr_ref[pl.ds(k, 1), :] = sr
            gi_ref[pl.ds(k, 1), :] = si
            er, ei = er_ref[pl.ds(k, 1), :], ei_ref[pl.ds(k, 1), :]
            return tr * sr - ti * si + er, tr * si + ti * sr + ei

        sr, si = lax.fori_loop(0, NC, chunk_step, (h0r_ref[...], h0i_ref[...]))
        fr_ref[...] = sr
        fi_ref[...] = si
        for cb in range(n_cb):
            cols = slice(cb * LANES, (cb + 1) * LANES)
            sweep(cb, gr_ref[:, cols], gi_ref[:, cols], True)
    else:
        for cb in range(n_cb):
            cols = slice(cb * LANES, (cb + 1) * LANES)
            hr, hi = sweep(cb, h0r_ref[:, cols], h0i_ref[:, cols], True)
            fr_ref[:, cols] = hr
            fi_ref[:, cols] = hi

    y = d_ref[...] * u
    for c2 in range(n_cb // per):
        rows = slice(c2 * MXU_DIM, (c2 + 1) * MXU_DIM)
        h_r = jnp.concatenate([xr_ref[c2 * per + t] for t in range(per)], axis=1).astype(BF16)
        h_i = jnp.concatenate([xi_ref[c2 * per + t] for t in range(per)], axis=1).astype(BF16)
        y = y + (jnp.dot(h_r, cdr_ref[rows, :].astype(BF16), preferred_element_type=F32)
                 - jnp.dot(h_i, cdi_ref[rows, :].astype(BF16), preferred_element_type=F32))
    if chained:
        for s in range(T):
            for t in range(per):
                stage_ref[t, pl.ds(s, NC, stride=T), :] = y[s * NC:(s + 1) * NC, t * LANES:(t + 1) * LANES]
        for t in range(per):
            y_ref[:, t * LANES:(t + 1) * LANES] = stage_ref[t]
    else:
        y_ref[...] = y


def _s5(u, h0r, h0i, lam_re, lam_im, log_dt, bt_re, bt_im, ct_re, ct_im, d_row, n_seq, T, NC, chained):
    rows = NC * T
    gps = SLAB // SSM_GROUP
    slab_row = lambda: pl.BlockSpec((1, SLAB_STATES), lambda b, s: (0, s))
    if chained:
        st_spec = pl.BlockSpec((None, 1, SLAB_STATES), lambda b, s: (b, 0, s))
        st_shape = jax.ShapeDtypeStruct((n_seq, 1, N_SLABS * SLAB_STATES), F32)
        scratch = [pltpu.VMEM((NC, SLAB_STATES), F32)] * 4 + [
            pltpu.VMEM((SLAB // LANES, rows, LANES), F32), pltpu.VMEM((rows, SLAB), F32)]
    else:
        st_spec = pl.BlockSpec((NC, SLAB_STATES), lambda b, s: (0, s))
        st_shape = jax.ShapeDtypeStruct((NC, N_SLABS * SLAB_STATES), F32)
        scratch = []
    return pl.pallas_call(
        functools.partial(_s5_kernel, T=T, NC=NC, chained=chained),
        grid=(n_seq, N_SLABS),
        in_specs=[pl.BlockSpec((rows, SLAB), lambda b, s: (b, s)), st_spec, st_spec,
                  slab_row(), slab_row(), slab_row(),
                  pl.BlockSpec((gps, SSM_GROUP, SSM_STATE), lambda b, s: (s, 0, 0)),
                  pl.BlockSpec((gps, SSM_GROUP, SSM_STATE), lambda b, s: (s, 0, 0)),
                  pl.BlockSpec((gps, SSM_STATE, SSM_GROUP), lambda b, s: (s, 0, 0)),
                  pl.BlockSpec((gps, SSM_STATE, SSM_GROUP), lambda b, s: (s, 0, 0)),
                  pl.BlockSpec((1, SLAB), lambda b, s: (0, s))],
        out_specs=[pl.BlockSpec((rows, SLAB), lambda b, s: (b, s)), st_spec, st_spec],
        out_shape=[jax.ShapeDtypeStruct((n_seq * rows, D_SSM), F32), st_shape, st_shape],
        scratch_shapes=[pltpu.VMEM((SLAB_STATES // LANES, rows, LANES), F32)] * 2
        + [pltpu.VMEM((SLAB, SLAB_STATES), F32)] * 2 + [pltpu.VMEM((SLAB_STATES, SLAB), F32)] * 2 + scratch,
        compiler_params=_params(48, 2), name="s5",
    )(u, h0r, h0i, lam_re, lam_im, log_dt, bt_re, bt_im, ct_re, ct_im, d_row)


def _mixout_kernel(x_ref, att_ref, y_ref, wg_ref, bg_ref, wo_ref, o_ref):
    y = y_ref[...]
    z = 0.5 * y * (1.0 + jnp.tanh(math.sqrt(2.0 / math.pi) * (y + 0.044715 * (y * y * y))))
    gate = _sigmoid(jnp.dot(z.astype(BF16), wg_ref[...], preferred_element_type=F32) + bg_ref[...])
    ssm = (z * gate).astype(BF16)
    o_ref[...] = (x_ref[...] + jnp.dot(att_ref[...], wo_ref[0:D_ATTN, :], preferred_element_type=F32)
                  + jnp.dot(ssm, wo_ref[D_ATTN:, :], preferred_element_type=F32))


def _mixout(x, att, y, wg, bg, wo):
    R = x.shape[0]
    tm = min(R, 512)
    row = lambda n: pl.BlockSpec((tm, n), lambda i: (i, 0))
    full = lambda a: pl.BlockSpec(a.shape, lambda i: (0,) * a.ndim, pipeline_mode=pl.Buffered(1))
    return pl.pallas_call(
        _mixout_kernel, grid=(R // tm,),
        in_specs=[row(D_MODEL), row(D_ATTN), row(D_SSM), full(wg), full(bg), full(wo)],
        out_specs=row(D_MODEL), out_shape=jax.ShapeDtypeStruct((R, D_MODEL), F32),
        compiler_params=_params(56, 1), name="mixout",
    )(x, att, y, wg, bg, wo)


def _ffn_kernel(x_ref, nw_ref, w1_ref, w3_ref, w2_ref, o_ref, h_ref):
    @pl.when(pl.program_id(1) == 0)
    def _():
        x = x_ref[...]
        h_ref[...] = _rms(x, nw_ref[...]).astype(BF16)
        o_ref[...] = x

    h = h_ref[...]
    for c in range(w1_ref.shape[1] // MXU_DIM):
        cols = slice(c * MXU_DIM, (c + 1) * MXU_DIM)
        a = jnp.dot(h, w1_ref[:, cols].astype(BF16), preferred_element_type=F32)
        b = jnp.dot(h, w3_ref[:, cols].astype(BF16), preferred_element_type=F32)
        g = (a * _sigmoid(a) * b).astype(BF16)
        o_ref[...] += jnp.dot(g, w2_ref[cols, :].astype(BF16), preferred_element_type=F32)


def _ffn(x, nw, w1, w3, w2, layer):
    R = x.shape[0]
    d_ff = w1.shape[2]
    tm = min(R, 1024)
    tf = 2 * MXU_DIM
    assert d_ff % tf == 0 and R % tm == 0
    return pl.pallas_call(
        _ffn_kernel, grid=(R // tm, d_ff // tf),
        in_specs=[pl.BlockSpec((tm, D_MODEL), lambda i, f: (i, 0), pipeline_mode=pl.Buffered(1)),
                  pl.BlockSpec((1, D_MODEL), lambda i, f: (0, 0)),
                  pl.BlockSpec((None, D_MODEL, tf), lambda i, f: (layer, 0, f)),
                  pl.BlockSpec((None, D_MODEL, tf), lambda i, f: (layer, 0, f)),
                  pl.BlockSpec((None, tf, D_MODEL), lambda i, f: (layer, f, 0))],
        out_specs=pl.BlockSpec((tm, D_MODEL), lambda i, f: (i, 0)),
        out_shape=jax.ShapeDtypeStruct((R, D_MODEL), F32),
        scratch_shapes=[pltpu.VMEM((tm, D_MODEL), BF16)],
        compiler_params=_params(60, 2), name="ffn",
    )(x, nw, w1, w3, w2)


SUBLANES = 8
HIST = SUBLANES * (max(POOL_WINDOWS).bit_length() - 1)


def _pool_prompt_kernel(x_ref, nw_ref, pw_ref, ps_ref, o_ref, st_ref, ext_ref, lvl_ref, *, tm, tiles_per_seq):
    i = pl.program_id(0)
    t_in_seq = i % tiles_per_seq
    n = HIST + tm

    @pl.when(t_in_seq == 0)
    def _():
        ext_ref[0:HIST, :] = jnp.zeros((HIST, D_MODEL), F32)

    x = x_ref[...]
    hp = _rms(x, nw_ref[...])
    ext_ref[HIST:n, :] = hp
    pos = t_in_seq * tm + lax.broadcasted_iota(jnp.int32, (tm, 1), 0)
    for g, w in enumerate(POOL_WINDOWS):
        cols = slice(g * POOL_GROUP, (g + 1) * POOL_GROUP)
        n_levels = w.bit_length() - 1
        assert w == 1 << n_levels and SUBLANES * n_levels <= HIST
        acc = None
        for k in range(n_levels):
            start, sh = SUBLANES * (k + 1), 1 << k
            if k == 0:
                acc = ext_ref[start:n, cols] + ext_ref[start - sh:n - sh, cols]
            else:
                acc = lvl_ref[k - 1, start:n, :] + lvl_ref[k - 1, start - sh:n - sh, :]
            if k < n_levels - 1:
                lvl_ref[k, start:n, :] = acc
        acc = acc[HIST - SUBLANES * n_levels:, :]
        inv = 1.0 / jnp.minimum(pos + 1, w).astype(F32)
        pooled = (acc * inv - hp[:, cols]).astype(BF16)
        o_ref[:, cols] = x[:, cols] + jnp.dot(pooled, pw_ref[g], preferred_element_type=F32) * ps_ref[:, cols]
    st_ref[...] = ext_ref[tm + HIST - POOL_BUF:tm + HIST, :]
    ext_ref[0:HIST, :] = ext_ref[tm:tm + HIST, :]


def _pool_prompt(x, nw, pw, ps, B, L):
    tm = min(L, 512)
    assert L % tm == 0 and tm >= HIST
    tiles_per_seq = L // tm
    return pl.pallas_call(
        functools.partial(_pool_prompt_kernel, tm=tm, tiles_per_seq=tiles_per_seq),
        grid=(B * tiles_per_seq,),
        in_specs=[pl.BlockSpec((tm, D_MODEL), lambda i: (i, 0)),
                  pl.BlockSpec((1, D_MODEL), lambda i: (0, 0)),
                  pl.BlockSpec(pw.shape, lambda i: (0, 0, 0)),
                  pl.BlockSpec((1, D_MODEL), lambda i: (0, 0))],
        out_specs=[pl.BlockSpec((tm, D_MODEL), lambda i: (i, 0)),
                   pl.BlockSpec((None, POOL_BUF, D_MODEL), lambda i: (i // tiles_per_seq, 0, 0))],
        out_shape=[jax.ShapeDtypeStruct((B * L, D_MODEL), F32), jax.ShapeDtypeStruct((B, POOL_BUF, D_MODEL), F32)],
        scratch_shapes=[pltpu.VMEM((HIST + tm, D_MODEL), F32),
                        pltpu.VMEM((HIST // SUBLANES - 1, HIST + tm, POOL_GROUP), F32)],
        compiler_params=_params(40, 1), name="pool_prompt",
    )(x, nw, pw, ps)


def _pool_sample(x, nw, sp, pw, ps, Bd, Q, pos0):
    G = len(POOL_WINDOWS)
    hist = [pl.BlockSpec((Bd, POOL_GROUP), functools.partial(lambda g, j: (0, j * G + g), j=j)) for j in range(POOL_BUF)]
    col = pl.BlockSpec((Q * Bd, POOL_GROUP), lambda g: (0, g))
    return pl.pallas_call(
        functools.partial(_pool_sample_kernel, Bd=Bd, Q=Q, pos0=pos0),
        grid=(G,),
        in_specs=[pl.BlockSpec((Q * Bd, D_MODEL), lambda g: (0, 0)),
                  pl.BlockSpec((1, D_MODEL), lambda g: (0, 0)),
                  pl.BlockSpec((None, POOL_GROUP, POOL_GROUP), lambda g: (g, 0, 0)),
                  pl.BlockSpec((1, POOL_GROUP), lambda g: (0, g))] + hist,
        out_specs=[col, col],
        out_shape=[jax.ShapeDtypeStruct((Q * Bd, D_MODEL), F32)] * 2,
        scratch_shapes=[pltpu.VMEM((Q * Bd, 1), F32)],
        compiler_params=_params(40, 1), name="pool_sample",
    )(x, nw, pw, ps, *([sp] * POOL_BUF))


def _pool_sample_kernel(x_ref, nw_ref, pw_ref, ps_ref, *rest, Bd, Q, pos0):
    hist_refs = rest[:POOL_BUF]
    o_ref, hs_ref, inv_ref = rest[POOL_BUF:]
    g = pl.program_id(0)

    @pl.when(g == 0)
    def _():
        x = x_ref[...]
        inv_ref[...] = lax.rsqrt(jnp.mean(x * x, axis=-1, keepdims=True) + RMS_EPS)

    for wi, w in enumerate(POOL_WINDOWS):
        @pl.when(g == wi)
        def _(wi=wi, w=w):
            cols = slice(wi * POOL_GROUP, (wi + 1) * POOL_GROUP)
            xs, hs = [], []
            for q in range(Q):
                rows = slice(q * Bd, (q + 1) * Bd)
                xq = x_ref[rows, cols]
                xs.append(xq)
                hs.append(xq * inv_ref[rows, :] * nw_ref[:, cols])
                hs_ref[rows, :] = hs[q]
            ext = lambda j: hist_refs[j][...] if j < POOL_BUF else hs[j - POOL_BUF]
            for q in range(Q):
                acc = hs[q]
                for k in range(1, w):
                    acc = acc + ext(POOL_BUF + q - k)
                cnt = float(min(pos0 + q + 1, w))
                pooled = (acc / cnt - hs[q]).astype(BF16)
                val = jnp.dot(pooled, pw_ref[...], preferred_element_type=F32) * ps_ref[...]
                o_ref[q * Bd:(q + 1) * Bd, :] = xs[q] + val


def kernel(x_prompt, x_sample, cache_k, cache_v, cache_logf, state_s5_re, state_s5_im, state_pool, page_table, norm_mix_w, norm_ffn_w, w_in, b_f, q_norm_w, k_norm_w, s5_lam_re, s5_lam_im, s5_log_dt, s5_b_re, s5_b_im, s5_c_re, s5_c_im, s5_d, w_glu, b_glu, w_out, pool_w, pool_scale, ffn_w1, ffn_w3, ffn_w2):
    B, L, D = x_prompt.shape
    Bd, Q, _ = x_sample.shape
    depth = norm_mix_w.shape[0]
    n_attn, n_phys, page = cache_k.shape[:3]
    n_pages = page_table.shape[1]
    past_len = n_pages * page
    T = 16
    assert D == D_MODEL and L % T == 0

    xp = x_prompt.reshape(B * L, D)
    xs = jnp.swapaxes(x_sample, 0, 1).reshape(Q * Bd, D)
    from_tm = lambda a, *tail: jnp.swapaxes(a.reshape(Q, Bd, *tail), 0, 1)
    pt_flat = page_table.reshape(-1)
    assert (page * N_HEADS) % LANES == 0
    ck_flat = cache_k.reshape(n_attn * n_phys, page, N_HEADS, HEAD_DIM)
    cv_flat = cache_v.reshape(n_attn * n_phys, page, N_HEADS, HEAD_DIM)
    clf_flat = cache_logf.reshape(n_attn * n_phys, page * N_HEADS // LANES, LANES)

    kp, vp, fp, srp, sip, plp = [], [], [], [], [], []
    ksm, vsm, fsm, srs, sis, pls = [], [], [], [], [], []
    for layer in range(depth):
        i = layer // 2
        nw = norm_mix_w[layer].reshape(1, D)
        if layer % 2 == 0:
            wq, wk, wv, wu, wf = _split_w_in(w_in[i])
            bf = jnp.pad(b_f[i], (0, LANES - N_HEADS)).reshape(1, LANES)
            qg = q_norm_w[i].reshape(1, HEAD_DIM)
            kg = k_norm_w[i].reshape(1, HEAD_DIM)
            wg = w_glu[i].astype(BF16)
            bg = b_glu[i].reshape(1, D_SSM)
            wo = w_out[i].astype(BF16)
            lam_re = s5_lam_re[i].reshape(1, -1)
            lam_im = s5_lam_im[i].reshape(1, -1)
            log_dt = jnp.broadcast_to(s5_log_dt[i][:, None], (N_SSM_GROUPS, SSM_STATE)).reshape(1, -1)
            bd_re = jnp.swapaxes(s5_b_re[i], 1, 2)
            bd_im = jnp.swapaxes(s5_b_im[i], 1, 2)
            cd_re = jnp.swapaxes(s5_c_re[i], 1, 2)
            cd_im = jnp.swapaxes(s5_c_im[i], 1, 2)
            d_row = s5_d[i].reshape(1, D_SSM)

            qb, k, kb, v, vb, u, lf, c = _inproj(xp, nw, wq, wk, wv, wu, wf, bf, qg, kg, seq_len=L,
                                                 q_scale=HEAD_DIM ** -0.5 * math.log2(math.e))
            c_t = jnp.swapaxes(c[:, :N_HEADS].reshape(B, L, N_HEADS), 1, 2)
            att = _attn_prompt(qb, kb, vb, c, c_t[:, :, None, :], B, L)
            zeros = jnp.zeros((B, 1, N_SLABS * SLAB_STATES), F32)
            y, h_re, h_im = _s5(u, zeros, zeros, lam_re, lam_im, log_dt, bd_re, bd_im, cd_re, cd_im, d_row,
                                n_seq=B, T=T, NC=L // T, chained=True)
            xp = _mixout(xp, att, y, wg, bg, wo)
            kp.append(k.reshape(B, L, N_HEADS, HEAD_DIM))
            vp.append(v.reshape(B, L, N_HEADS, HEAD_DIM))
            fp.append(lf[:, :N_HEADS].reshape(B, L, N_HEADS))
            srp.append(h_re.reshape(B, N_SSM_GROUPS, SSM_STATE))
            sip.append(h_im.reshape(B, N_SSM_GROUPS, SSM_STATE))

            qb, k, kb, v, vb, u, lf = _inproj(xs, nw, wq, wk, wv, wu, wf, bf, qg, kg, seq_len=None,
                                              q_scale=HEAD_DIM ** -0.5)
            lfn = lf[:, :N_HEADS].reshape(Q, Bd, N_HEADS)
            heads = lambda a: a.astype(F32).reshape(Q, Bd, N_HEADS, HEAD_DIM)
            att = _attn_sample(pt_flat, heads(qb), heads(k), heads(v),
                               jnp.swapaxes(lfn, 0, 1).reshape(Bd, Q * N_HEADS, 1),
                               ck_flat, cv_flat, clf_flat, i, n_phys, n_pages, Bd, Q).astype(BF16)
            y, h_re, h_im = _s5(u, state_s5_re[i].reshape(Bd, -1), state_s5_im[i].reshape(Bd, -1),
                                lam_re, lam_im, log_dt, bd_re, bd_im, cd_re, cd_im, d_row,
                                n_seq=1, T=Q, NC=Bd, chained=False)
            xs = _mixout(xs, att.reshape(Q * Bd, D_ATTN), y, wg, bg, wo)
            ksm.append(from_tm(k, N_HEADS, HEAD_DIM))
            vsm.append(from_tm(v, N_HEADS, HEAD_DIM))
            fsm.append(jnp.swapaxes(lfn, 0, 1))
            srs.append(h_re.reshape(Bd, N_SSM_GROUPS, SSM_STATE))
            sis.append(h_im.reshape(Bd, N_SSM_GROUPS, SSM_STATE))
        else:
            pw = pool_w[i].astype(BF16)
            ps = pool_scale[i].reshape(1, D)
            xp, st = _pool_prompt(xp, nw, pw, ps, B, L)
            plp.append(st)
            xs, hs = _pool_sample(xs, nw, state_pool[i].reshape(Bd, POOL_BUF * D), pw, ps, Bd, Q, past_len)
            pls.append(jnp.concatenate([state_pool[i], from_tm(hs, D)], axis=1)[:, -POOL_BUF:])
        nf = norm_ffn_w[layer].reshape(1, D)
        xp = _ffn(xp, nf, ffn_w1, ffn_w3, ffn_w2, layer)
        xs = _ffn(xs, nf, ffn_w1, ffn_w3, ffn_w2, layer)
    return (xp.reshape(B, L, D), from_tm(xs, D),
            jnp.stack(kp), jnp.stack(vp), jnp.stack(fp), jnp.stack(srp), jnp.stack(sip), jnp.stack(plp),
            jnp.stack(ksm), jnp.stack(vsm), jnp.stack(fsm), jnp.stack(srs), jnp.stack(sis), jnp.stack(pls))
```

```python
import functools
import math

import jax
import jax.numpy as jnp
from jax import lax
from jax.experimental import pallas as pl
from jax.experimental.pallas import tpu as pltpu

F32 = jnp.float32
BF16 = jnp.bfloat16

D_MODEL = 2048
D_ATTN = D_MODEL // 2
HEAD_DIM = 128
N_HEADS = D_ATTN // HEAD_DIM
D_SSM = D_MODEL - D_ATTN
SSM_GROUP = 16
N_SSM_GROUPS = D_SSM // SSM_GROUP
SSM_STATE = 64
POOL_WINDOWS = (2, 4, 8, 16)
POOL_GROUP = D_MODEL // len(POOL_WINDOWS)
POOL_BUF = max(POOL_WINDOWS) - 1
RMS_EPS = 1e-6

LANES = 128
MXU_DIM = 256
SLAB = MXU_DIM
SLAB_STATES = (SLAB // SSM_GROUP) * SSM_STATE
N_SLABS = D_SSM // SLAB
VMEM_BYTES = 64 * 1024 * 1024


def _params(vmem_mb, n_axes):
    return pltpu.CompilerParams(dimension_semantics=("arbitrary",) * n_axes,
                                vmem_limit_bytes=min(vmem_mb * 1024 * 1024, VMEM_BYTES - 4 * 1024 * 1024))


def _rms(x, w):
    return x * lax.rsqrt(jnp.mean(x * x, axis=-1, keepdims=True) + RMS_EPS) * w


def _log_sigmoid(x):
    return jnp.minimum(x, 0.0) - jnp.log1p(jnp.exp(-jnp.abs(x)))


def _sigmoid(x):
    return 1.0 / (1.0 + jnp.exp(-x))


def _split3(x):
    hi = x.astype(BF16)
    r = x - hi.astype(F32)
    mid = r.astype(BF16)
    lo = (r - mid.astype(F32)).astype(BF16)
    return hi, mid, lo


def _cumsum_rows(x):
    n = x.shape[0]
    tri = (lax.broadcasted_iota(jnp.int32, (n, n), 1) <= lax.broadcasted_iota(jnp.int32, (n, n), 0)).astype(BF16)
    hi, mid, lo = _split3(x)
    z = jnp.dot(tri, jnp.concatenate([hi, mid, lo], axis=1), preferred_element_type=F32)
    return z[:, :LANES] + z[:, LANES:2 * LANES] + z[:, 2 * LANES:]


def _inproj_kernel(x_ref, nw_ref, wq_ref, wk_ref, wv_ref, wu_ref, wf_ref, bf_ref, qg_ref, kg_ref,
                   q_ref, k_ref, kb_ref, v_ref, vb_ref, u_ref, lf_ref, *rest, tiles_per_seq, scale):
    h = _rms(x_ref[...], nw_ref[...]).astype(BF16)
    for j in range(D_ATTN // MXU_DIM):
        cols = slice(j * MXU_DIM, (j + 1) * MXU_DIM)
        zq = jnp.dot(h, wq_ref[:, cols], preferred_element_type=F32)
        zk = jnp.dot(h, wk_ref[:, cols], preferred_element_type=F32)
        zv = jnp.dot(h, wv_ref[:, cols], preferred_element_type=F32)
        for t in range(MXU_DIM // HEAD_DIM):
            sub = slice(t * HEAD_DIM, (t + 1) * HEAD_DIM)
            dst = slice(j * MXU_DIM + t * HEAD_DIM, j * MXU_DIM + (t + 1) * HEAD_DIM)
            q_ref[:, dst] = (_rms(zq[:, sub], qg_ref[...]) * scale).astype(BF16)
            kn = _rms(zk[:, sub], kg_ref[...])
            k_ref[:, dst] = kn
            kb_ref[:, dst] = kn.astype(BF16)
        v_ref[:, cols] = zv
        vb_ref[:, cols] = zv.astype(BF16)
        u_ref[:, cols] = jnp.dot(h, wu_ref[:, cols], preferred_element_type=F32)
    lf = _log_sigmoid(jnp.dot(h, wf_ref[...], preferred_element_type=F32) + bf_ref[...])
    lf_ref[...] = lf
    if tiles_per_seq is not None:
        c_ref, carry_ref = rest

        @pl.when(pl.program_id(0) % tiles_per_seq == 0)
        def _():
            carry_ref[...] = jnp.zeros_like(carry_ref)

        c = _cumsum_rows(lf) + carry_ref[...]
        c_ref[...] = c
        carry_ref[...] = c[c.shape[0] - 1:, :]


def _inproj(x, nw, wq, wk, wv, wu, wf, bf, qg, kg, seq_len, q_scale):
    R = x.shape[0]
    tm = min(R, 512)
    row = lambda n: pl.BlockSpec((tm, n), lambda i: (i, 0))
    full = lambda a: pl.BlockSpec(a.shape, lambda i: (0,) * a.ndim, pipeline_mode=pl.Buffered(1))
    out_shape = [jax.ShapeDtypeStruct((R, D_ATTN), BF16), jax.ShapeDtypeStruct((R, D_ATTN), F32),
                 jax.ShapeDtypeStruct((R, D_ATTN), BF16), jax.ShapeDtypeStruct((R, D_ATTN), F32),
                 jax.ShapeDtypeStruct((R, D_ATTN), BF16), jax.ShapeDtypeStruct((R, D_SSM), F32),
                 jax.ShapeDtypeStruct((R, LANES), F32)]
    out_specs = [row(D_ATTN)] * 5 + [row(D_SSM), row(LANES)]
    scratch = []
    tiles_per_seq = None
    if seq_len is not None:
        assert seq_len % tm == 0
        tiles_per_seq = seq_len // tm
        out_shape.append(jax.ShapeDtypeStruct((R, LANES), F32))
        out_specs.append(row(LANES))
        scratch.append(pltpu.VMEM((1, LANES), F32))
    return pl.pallas_call(
        functools.partial(_inproj_kernel, tiles_per_seq=tiles_per_seq, scale=q_scale),
        grid=(R // tm,),
        in_specs=[row(D_MODEL), full(nw), full(wq), full(wk), full(wv), full(wu), full(wf), full(bf), full(qg), full(kg)],
        out_specs=out_specs, out_shape=out_shape, scratch_shapes=scratch,
        compiler_params=_params(60, 1), name="inproj",
    )(x, nw, wq, wk, wv, wu, wf, bf, qg, kg)


def _attn_prompt_kernel(q_ref, k_ref, v_ref, c_ref, ck_ref, o_ref, *, L, tq):
    nt = (((1,), (1,)), ((), ()))
    log2e = math.log2(math.e)
    head = lax.broadcasted_iota(jnp.int32, (tq, LANES), 1) == pl.program_id(1)
    for qi in range(L // tq):
        rows = slice(qi * tq, (qi + 1) * tq)
        q = q_ref[rows, :]
        cq = jnp.sum(jnp.where(head, c_ref[rows, :], 0.0), axis=-1, keepdims=True) * log2e
        m = l = acc = None
        for ki in range(qi + 1):
            cols = slice(ki * tq, (ki + 1) * tq)
            t = lax.dot_general(q, k_ref[cols, :], nt, preferred_element_type=F32) - ck_ref[:, cols] * log2e
            if ki == qi:
                keep = lax.broadcasted_iota(jnp.int32, (tq, tq), 1) <= lax.broadcasted_iota(jnp.int32, (tq, tq), 0)
                t = jnp.where(keep, t, -jnp.inf)
            mx = jnp.max(t, axis=-1, keepdims=True) + cq
            if ki == 0:
                m = mx
                p = jnp.exp2(t - (m - cq))
                l = jnp.sum(p, axis=-1, keepdims=True)
                acc = jnp.dot(p.astype(BF16), v_ref[cols, :], preferred_element_type=F32)
            else:
                m_new = jnp.maximum(m, mx)
                alpha = jnp.exp2(m - m_new)
                p = jnp.exp2(t - (m_new - cq))
                l = alpha * l + jnp.sum(p, axis=-1, keepdims=True)
                acc = alpha * acc + jnp.dot(p.astype(BF16), v_ref[cols, :], preferred_element_type=F32)
                m = m_new
        o_ref[rows, :] = (acc / l).astype(BF16)


def _attn_prompt(qb, kb, vb, c, ck, B, L):
    tq = min(L, 512)
    blk = pl.BlockSpec((L, HEAD_DIM), lambda b, h: (b, h))
    return pl.pallas_call(
        functools.partial(_attn_prompt_kernel, L=L, tq=tq),
        grid=(B, N_HEADS),
        in_specs=[blk, blk, blk,
                  pl.BlockSpec((L, LANES), lambda b, h: (b, 0)),
                  pl.BlockSpec((None, None, 1, L), lambda b, h: (b, h, 0, 0))],
        out_specs=blk, out_shape=jax.ShapeDtypeStruct((B * L, D_ATTN), BF16),
        compiler_params=_params(32, 2), name="attn_prompt",
    )(qb, kb, vb, c, ck)


PAGE_RING = 3


def _attn_sample_kernel(pt_ref, q_ref, kn_ref, vn_ref, lfn_ref, *rest, n_pages, Q, n_steps, base):
    lf_refs = rest[:n_pages]
    ck_hbm, cv_hbm, o_ref, s_ref, d_ref, kbuf, vbuf, sem = rest[n_pages:]
    b = pl.program_id(0)

    def page_copies(step, slot):
        out = []
        for j in range(n_pages):
            pg = pt_ref[step * n_pages + j] + base
            out.append(pltpu.make_async_copy(ck_hbm.at[pg], kbuf.at[slot, j], sem.at[slot, 0]))
            out.append(pltpu.make_async_copy(cv_hbm.at[pg], vbuf.at[slot, j], sem.at[slot, 1]))
        return out

    @pl.when(b == 0)
    def _():
        for s in range(min(PAGE_RING - 1, n_steps)):
            for c in page_copies(s, s):
                c.start()

    ahead = b + (PAGE_RING - 1)

    @pl.when(ahead < n_steps)
    def _():
        for c in page_copies(ahead, ahead % PAGE_RING):
            c.start()

    slot = b % PAGE_RING
    for c in page_copies(b, slot):
        c.wait()
    k_refs = [kbuf.at[slot, j] for j in range(n_pages)]
    v_refs = [vbuf.at[slot, j] for j in range(n_pages)]
    nt = (((1,), (1,)), ((), ()))
    H = N_HEADS
    M = Q * H
    page = k_refs[0].shape[0]
    PL = page * H
    RPP = PL // LANES

    qm = q_ref[...].reshape(M, HEAD_DIM)
    qmb = qm.astype(BF16)
    same_head = (lax.broadcasted_iota(jnp.int32, (M, LANES), 1) % H
                 == lax.broadcasted_iota(jnp.int32, (M, LANES), 0) % H)

    x = jnp.concatenate([r[...] for r in lf_refs], axis=0)
    lane = lax.broadcasted_iota(jnp.int32, x.shape, 1)
    sh = H
    while sh < LANES:
        x = x + jnp.where(lane >= sh, pltpu.roll(x, sh, axis=1), 0.0)
        sh *= 2
    tot = jnp.where(lane >= LANES - H, x, 0.0)
    inc = _cumsum_rows(tot)
    after = inc[inc.shape[0] - 1:, :] - inc
    sh = H
    while sh < LANES:
        after = after + pltpu.roll(after, sh, axis=1)
        sh *= 2
    row_tot = tot
    sh = H
    while sh < LANES:
        row_tot = row_tot + pltpu.roll(row_tot, sh, axis=1)
        sh *= 2
    d_ref[...] = after + (row_tot - x)

    lfn = lfn_ref[...]
    cum = [lfn[0:H, :]]
    for i in range(1, Q):
        cum.append(cum[-1] + lfn[i * H:(i + 1) * H, :])
    cum_q = jnp.concatenate(cum, axis=0)
    cum_k = jnp.concatenate([jnp.concatenate([c] * Q, axis=0) for c in cum], axis=1)

    qm_r = qmb.astype(F32)
    tile_q = lambda a: jnp.concatenate([a.astype(BF16).astype(F32)] * Q, axis=0)
    s_new = jnp.concatenate(
        [jnp.sum(qm_r * tile_q(kn_ref[i]), axis=-1, keepdims=True) for i in range(Q)], axis=1)
    s_new = s_new + (cum_q - cum_k)
    q_of_row = lax.broadcasted_iota(jnp.int32, (M, Q), 0) // H
    s_new = jnp.where(lax.broadcasted_iota(jnp.int32, (M, Q), 1) <= q_of_row, s_new, -jnp.inf)
    m = jnp.max(s_new, axis=-1, keepdims=True)

    for j in range(n_pages):
        kb = k_refs[j][...].reshape(PL, HEAD_DIM).astype(BF16)
        s = lax.dot_general(qmb, kb, nt, preferred_element_type=F32)
        for r in range(RPP):
            blk = s[:, r * LANES:(r + 1) * LANES] + (cum_q + d_ref[j * RPP + r:j * RPP + r + 1, :])
            blk = jnp.where(same_head, blk, -jnp.inf)
            s_ref[:, (j * RPP + r) * LANES:(j * RPP + r + 1) * LANES] = blk
            m = jnp.maximum(m, jnp.max(blk, axis=-1, keepdims=True))

    p_new = jnp.exp(s_new - m)
    l = jnp.sum(p_new, axis=-1, keepdims=True)
    p_new_r = p_new.astype(BF16).astype(F32)
    acc = p_new_r[:, 0:1] * tile_q(vn_ref[0])
    for i in range(1, Q):
        acc = acc + p_new_r[:, i:i + 1] * tile_q(vn_ref[i])
    for j in range(n_pages):
        p = jnp.exp(s_ref[:, j * PL:(j + 1) * PL] - m)
        l = l + jnp.sum(p, axis=-1, keepdims=True)
        vb = v_refs[j][...].reshape(PL, HEAD_DIM).astype(BF16)
        acc = acc + jnp.dot(p.astype(BF16), vb, preferred_element_type=F32)
    o_ref[...] = (acc / l).reshape(Q, H, HEAD_DIM)


def _attn_sample(page_table_flat, q4, k4, v4, lfn_col, cache_k, cache_v, cache_lf, layer, n_phys, n_pages, Bd, Q):
    page = cache_k.shape[1]
    base = layer * n_phys
    rpp = page * N_HEADS // LANES

    def paged(tail):
        return [pl.BlockSpec((None,) + tail, functools.partial(
            lambda b, pt, j: (pt[b * n_pages + j] + base,) + (0,) * len(tail), j=j)) for j in range(n_pages)]

    tok = pl.BlockSpec((Q, None, N_HEADS, HEAD_DIM), lambda b, pt: (0, b, 0, 0))
    ring = pltpu.VMEM((PAGE_RING, n_pages, page, N_HEADS, HEAD_DIM), F32)
    grid_spec = pltpu.PrefetchScalarGridSpec(
        num_scalar_prefetch=1, grid=(Bd,),
        in_specs=[tok, tok, tok, pl.BlockSpec((None, Q * N_HEADS, 1), lambda b, pt: (b, 0, 0))]
        + paged((rpp, LANES)) + [pl.BlockSpec(memory_space=pl.ANY)] * 2,
        out_specs=tok,
        scratch_shapes=[pltpu.VMEM((Q * N_HEADS, n_pages * page * N_HEADS), F32),
                        pltpu.VMEM((n_pages * rpp, LANES), F32), ring, ring,
                        pltpu.SemaphoreType.DMA((PAGE_RING, 2))])
    return pl.pallas_call(
        functools.partial(_attn_sample_kernel, n_pages=n_pages, Q=Q, n_steps=Bd, base=base),
        grid_spec=grid_spec, out_shape=jax.ShapeDtypeStruct((Q, Bd, N_HEADS, HEAD_DIM), F32),
        compiler_params=_params(60, 1), name="attn_sample",
    )(page_table_flat, q4, k4, v4, lfn_col, *([cache_lf] * n_pages), cache_k, cache_v)


def _s5_kernel(u_ref, h0r_ref, h0i_ref, lr_ref, li_ref, ldt_ref, btr_ref, bti_ref, ctr_ref, cti_ref, d_ref,
               y_ref, fr_ref, fi_ref, xr_ref, xi_ref, bdr_ref, bdi_ref, cdr_ref, cdi_ref, *scratch, T, NC, chained):
    @pl.when((pl.program_id(0) == 0) & (pl.program_id(1) == 0))
    def _():
        for ref in (bdr_ref, bdi_ref, cdr_ref, cdi_ref):
            ref[...] = jnp.zeros_like(ref)

    for g in range(SLAB // SSM_GROUP):
        rows_c, rows_p = slice(g * SSM_GROUP, (g + 1) * SSM_GROUP), slice(g * SSM_STATE, (g + 1) * SSM_STATE)
        bdr_ref[rows_c, rows_p] = btr_ref[g]
        bdi_ref[rows_c, rows_p] = bti_ref[g]
        cdr_ref[rows_p, rows_c] = ctr_ref[g]
        cdi_ref[rows_p, rows_c] = cti_ref[g]

    lr, li = lr_ref[...], li_ref[...]
    dt = jnp.exp(ldt_ref[...])
    mag = jnp.exp(lr * dt)
    ar, ai = mag * jnp.cos(li * dt), mag * jnp.sin(li * dt)
    den = lr * lr + li * li
    zr = ((ar - 1.0) * lr + ai * li) / den
    zi = (ai * lr - (ar - 1.0) * li) / den
    bdr, bdi = bdr_ref[...], bdi_ref[...]
    pr = (zr * bdr - zi * bdi).astype(BF16)
    pi = (zr * bdi + zi * bdr).astype(BF16)
    n_cb = SLAB_STATES // LANES
    per = MXU_DIM // LANES
    if chained:
        stage_ref, perm_ref = scratch[4:]
        for t in range(per):
            stage_ref[t] = u_ref[:, t * LANES:(t + 1) * LANES]
        for s in range(T):
            for t in range(per):
                perm_ref[s * NC:(s + 1) * NC, t * LANES:(t + 1) * LANES] = stage_ref[t, pl.ds(s, NC, stride=T), :]
        u = perm_ref[...]
    else:
        u = u_ref[...]
    ub = u.astype(BF16)
    for c2 in range(n_cb // per):
        cols = slice(c2 * MXU_DIM, (c2 + 1) * MXU_DIM)
        x_r = jnp.dot(ub, pr[:, cols], preferred_element_type=F32)
        x_i = jnp.dot(ub, pi[:, cols], preferred_element_type=F32)
        for t in range(per):
            xr_ref[c2 * per + t] = x_r[:, t * LANES:(t + 1) * LANES]
            xi_ref[c2 * per + t] = x_i[:, t * LANES:(t + 1) * LANES]

    def sweep(cb, hr, hi, store):
        cols = slice(cb * LANES, (cb + 1) * LANES)
        a_r, a_i = ar[:, cols], ai[:, cols]
        for s in range(T):
            rows = slice(s * NC, (s + 1) * NC)
            xr, xi = xr_ref[cb, rows, :], xi_ref[cb, rows, :]
            hr, hi = a_r * hr - a_i * hi + xr, a_r * hi + a_i * hr + xi
            if store:
                xr_ref[cb, rows, :] = hr
                xi_ref[cb, rows, :] = hi
        return hr, hi

    if chained:
        er_ref, ei_ref, gr_ref, gi_ref = scratch[:4]
        zero = jnp.zeros((NC, LANES), F32)
        for cb in range(n_cb):
            cols = slice(cb * LANES, (cb + 1) * LANES)
            er_ref[:, cols], ei_ref[:, cols] = sweep(cb, zero, zero, False)
        tr, ti = ar, ai
        for _ in range(T - 1):
            tr, ti = tr * ar - ti * ai, tr * ai + ti * ar

        def chunk_step(k, carry):
            sr, si = carry
            gr_ref[pl.ds(k, 1), :] = sr
            gi_ref[pl.ds(k, 1), :] = si
            er, ei = er_ref[pl.ds(k, 1), :], ei_ref[pl.ds(k, 1), :]
            return tr * sr - ti * si + er, tr * si + ti * sr + ei

        sr, si = lax.fori_loop(0, NC, chunk_step, (h0r_ref[...], h0i_ref[...]))
        fr_ref[...] = sr
        fi_ref[...] = si
        for cb in range(n_cb):
            cols = slice(cb * LANES, (cb + 1) * LANES)
            sweep(cb, gr_ref[:, cols], gi_ref[:, cols], True)
    else:
        for cb in range(n_cb):
            cols = slice(cb * LANES, (cb + 1) * LANES)
            hr, hi = sweep(cb, h0r_ref[:, cols], h0i_ref[:, cols], True)
            fr_ref[:, cols] = hr
            fi_ref[:, cols] = hi

    y = d_ref[...] * u
    for c2 in range(n_cb // per):
        rows = slice(c2 * MXU_DIM, (c2 + 1) * MXU_DIM)
        h_r = jnp.concatenate([xr_ref[c2 * per + t] for t in range(per)], axis=1).astype(BF16)
        h_i = jnp.concatenate([xi_ref[c2 * per + t] for t in range(per)], axis=1).astype(BF16)
        y = y + (jnp.dot(h_r, cdr_ref[rows, :].astype(BF16), preferred_element_type=F32)
                 - jnp.dot(h_i, cdi_ref[rows, :].astype(BF16), preferred_element_type=F32))
    if chained:
        for s in range(T):
            for t in range(per):
                stage_ref[t, pl.ds(s, NC, stride=T), :] = y[s * NC:(s + 1) * NC, t * LANES:(t + 1) * LANES]
        for t in range(per):
            y_ref[:, t * LANES:(t + 1) * LANES] = stage_ref[t]
    else:
        y_ref[...] = y


def _s5(u, h0r, h0i, lam_re, lam_im, log_dt, bt_re, bt_im, ct_re, ct_im, d_row, n_seq, T, NC, chained):
    rows = NC * T
    gps = SLAB // SSM_GROUP
    slab_row = lambda: pl.BlockSpec((1, SLAB_STATES), lambda b, s: (0, s))
    if chained:
        st_spec = pl.BlockSpec((None, 1, SLAB_STATES), lambda b, s: (b, 0, s))
        st_shape = jax.ShapeDtypeStruct((n_seq, 1, N_SLABS * SLAB_STATES), F32)
        scratch = [pltpu.VMEM((NC, SLAB_STATES), F32)] * 4 + [
            pltpu.VMEM((SLAB // LANES, rows, LANES), F32), pltpu.VMEM((rows, SLAB), F32)]
    else:
        st_spec = pl.BlockSpec((NC, SLAB_STATES), lambda b, s: (0, s))
        st_shape = jax.ShapeDtypeStruct((NC, N_SLABS * SLAB_STATES), F32)
        scratch = []
    return pl.pallas_call(
        functools.partial(_s5_kernel, T=T, NC=NC, chained=chained),
        grid=(n_seq, N_SLABS),
        in_specs=[pl.BlockSpec((rows, SLAB), lambda b, s: (b, s)), st_spec, st_spec,
                  slab_row(), slab_row(), slab_row(),
                  pl.BlockSpec((gps, SSM_GROUP, SSM_STATE), lambda b, s: (s, 0, 0)),
                  pl.BlockSpec((gps, SSM_GROUP, SSM_STATE), lambda b, s: (s, 0, 0)),
                  pl.BlockSpec((gps, SSM_STATE, SSM_GROUP), lambda b, s: (s, 0, 0)),
                  pl.BlockSpec((gps, SSM_STATE, SSM_GROUP), lambda b, s: (s, 0, 0)),
                  pl.BlockSpec((1, SLAB), lambda b, s: (0, s))],
        out_specs=[pl.BlockSpec((rows, SLAB), lambda b, s: (b, s)), st_spec, st_spec],
        out_shape=[jax.ShapeDtypeStruct((n_seq * rows, D_SSM), F32), st_shape, st_shape],
        scratch_shapes=[pltpu.VMEM((SLAB_STATES // LANES, rows, LANES), F32)] * 2
        + [pltpu.VMEM((SLAB, SLAB_STATES), F32)] * 2 + [pltpu.VMEM((SLAB_STATES, SLAB), F32)] * 2 + scratch,
        compiler_params=_params(48, 2), name="s5",
    )(u, h0r, h0i, lam_re, lam_im, log_dt, bt_re, bt_im, ct_re, ct_im, d_row)


def _mixout_kernel(x_ref, att_ref, y_ref, wg_ref, bg_ref, wo_ref, o_ref):
    y = y_ref[...]
    z = 0.5 * y * (1.0 + jnp.tanh(math.sqrt(2.0 / math.pi) * (y + 0.044715 * (y * y * y))))
    gate = _sigmoid(jnp.dot(z.astype(BF16), wg_ref[...], preferred_element_type=F32) + bg_ref[...])
    ssm = (z * gate).astype(BF16)
    o_ref[...] = (x_ref[...] + jnp.dot(att_ref[...], wo_ref[0:D_ATTN, :], preferred_element_type=F32)
                  + jnp.dot(ssm, wo_ref[D_ATTN:, :], preferred_element_type=F32))


def _mixout(x, att, y, wg, bg, wo):
    R = x.shape[0]
    tm = min(R, 512)
    row = lambda n: pl.BlockSpec((tm, n), lambda i: (i, 0))
    full = lambda a: pl.BlockSpec(a.shape, lambda i: (0,) * a.ndim, pipeline_mode=pl.Buffered(1))
    return pl.pallas_call(
        _mixout_kernel, grid=(R // tm,),
        in_specs=[row(D_MODEL), row(D_ATTN), row(D_SSM), full(wg), full(bg), full(wo)],
        out_specs=row(D_MODEL), out_shape=jax.ShapeDtypeStruct((R, D_MODEL), F32),
        compiler_params=_params(56, 1), name="mixout",
    )(x, att, y, wg, bg, wo)


def _ffn_kernel(x_ref, nw_ref, w1_ref, w3_ref, w2_ref, o_ref, h_ref):
    @pl.when(pl.program_id(1) == 0)
    def _():
        x = x_ref[...]
        h_ref[...] = _rms(x, nw_ref[...]).astype(BF16)
        o_ref[...] = x

    h = h_ref[...]
    for c in range(w1_ref.shape[1] // MXU_DIM):
        cols = slice(c * MXU_DIM, (c + 1) * MXU_DIM)
        a = jnp.dot(h, w1_ref[:, cols].astype(BF16), preferred_element_type=F32)
        b = jnp.dot(h, w3_ref[:, cols].astype(BF16), preferred_element_type=F32)
        g = (a * _sigmoid(a) * b).astype(BF16)
        o_ref[...] += jnp.dot(g, w2_ref[cols, :].astype(BF16), preferred_element_type=F32)


def _ffn(x, nw, w1, w3, w2, layer):
    R = x.shape[0]
    d_ff = w1.shape[2]
    tm = min(R, 1024)
    tf = 2 * MXU_DIM
    assert d_ff % tf == 0 and R % tm == 0
    return pl.pallas_call(
        _ffn_kernel, grid=(R // tm, d_ff // tf),
        in_specs=[pl.BlockSpec((tm, D_MODEL), lambda i, f: (i, 0), pipeline_mode=pl.Buffered(1)),
                  pl.BlockSpec((1, D_MODEL), lambda i, f: (0, 0)),
                  pl.BlockSpec((None, D_MODEL, tf), lambda i, f: (layer, 0, f)),
                  pl.BlockSpec((None, D_MODEL, tf), lambda i, f: (layer, 0, f)),
                  pl.BlockSpec((None, tf, D_MODEL), lambda i, f: (layer, f, 0))],
        out_specs=pl.BlockSpec((tm, D_MODEL), lambda i, f: (i, 0)),
        out_shape=jax.ShapeDtypeStruct((R, D_MODEL), F32),
        scratch_shapes=[pltpu.VMEM((tm, D_MODEL), BF16)],
        compiler_params=_params(60, 2), name="ffn",
    )(x, nw, w1, w3, w2)


SUBLANES = 8
HIST = SUBLANES * (max(POOL_WINDOWS).bit_length() - 1)


def _pool_prompt_kernel(x_ref, nw_ref, pw_ref, ps_ref, o_ref, st_ref, ext_ref, lvl_ref, *, tm, tiles_per_seq):
    i = pl.program_id(0)
    t_in_seq = i % tiles_per_seq
    n = HIST + tm

    @pl.when(t_in_seq == 0)
    def _():
        ext_ref[0:HIST, :] = jnp.zeros((HIST, D_MODEL), F32)

    x = x_ref[...]
    hp = _rms(x, nw_ref[...])
    ext_ref[HIST:n, :] = hp
    pos = t_in_seq * tm + lax.broadcasted_iota(jnp.int32, (tm, 1), 0)
    for g, w in enumerate(POOL_WINDOWS):
        cols = slice(g * POOL_GROUP, (g + 1) * POOL_GROUP)
        n_levels = w.bit_length() - 1
        assert w == 1 << n_levels and SUBLANES * n_levels <= HIST
        acc = None
        for k in range(n_levels):
            start, sh = SUBLANES * (k + 1), 1 << k
            if k == 0:
                acc = ext_ref[start:n, cols] + ext_ref[start - sh:n - sh, cols]
            else:
                acc = lvl_ref[k - 1, start:n, :] + lvl_ref[k - 1, start - sh:n - sh, :]
            if k < n_levels - 1:
                lvl_ref[k, start:n, :] = acc
        acc = acc[HIST - SUBLANES * n_levels:, :]
        inv = 1.0 / jnp.minimum(pos + 1, w).astype(F32)
        pooled = (acc * inv - hp[:, cols]).astype(BF16)
        o_ref[:, cols] = x[:, cols] + jnp.dot(pooled, pw_ref[g], preferred_element_type=F32) * ps_ref[:, cols]
    st_ref[...] = ext_ref[tm + HIST - POOL_BUF:tm + HIST, :]
    ext_ref[0:HIST, :] = ext_ref[tm:tm + HIST, :]


def _pool_prompt(x, nw, pw, ps, B, L):
    tm = min(L, 512)
    assert L % tm == 0 and tm >= HIST
    tiles_per_seq = L // tm
    return pl.pallas_call(
        functools.partial(_pool_prompt_kernel, tm=tm, tiles_per_seq=tiles_per_seq),
        grid=(B * tiles_per_seq,),
        in_specs=[pl.BlockSpec((tm, D_MODEL), lambda i: (i, 0)),
                  pl.BlockSpec((1, D_MODEL), lambda i: (0, 0)),
                  pl.BlockSpec(pw.shape, lambda i: (0, 0, 0)),
                  pl.BlockSpec((1, D_MODEL), lambda i: (0, 0))],
        out_specs=[pl.BlockSpec((tm, D_MODEL), lambda i: (i, 0)),
                   pl.BlockSpec((None, POOL_BUF, D_MODEL), lambda i: (i // tiles_per_seq, 0, 0))],
        out_shape=[jax.ShapeDtypeStruct((B * L, D_MODEL), F32), jax.ShapeDtypeStruct((B, POOL_BUF, D_MODEL), F32)],
        scratch_shapes=[pltpu.VMEM((HIST + tm, D_MODEL), F32),
                        pltpu.VMEM((HIST // SUBLANES - 1, HIST + tm, POOL_GROUP), F32)],
        compiler_params=_params(40, 1), name="pool_prompt",
    )(x, nw, pw, ps)


def _pool_sample(x, nw, sp, pw, ps, Bd, Q, pos0):
    G = len(POOL_WINDOWS)
    hist = [pl.BlockSpec((Bd, POOL_GROUP), functools.partial(lambda g, j: (0, j * G + g), j=j)) for j in range(POOL_BUF)]
    col = pl.BlockSpec((Q * Bd, POOL_GROUP), lambda g: (0, g))
    return pl.pallas_call(
        functools.partial(_pool_sample_kernel, Bd=Bd, Q=Q, pos0=pos0),
        grid=(G,),
        in_specs=[pl.BlockSpec((Q * Bd, D_MODEL), lambda g: (0, 0)),
                  pl.BlockSpec((1, D_MODEL), lambda g: (0, 0)),
                  pl.BlockSpec((None, POOL_GROUP, POOL_GROUP), lambda g: (g, 0, 0)),
                  pl.BlockSpec((1, POOL_GROUP), lambda g: (0, g))] + hist,
        out_specs=[col, col],
        out_shape=[jax.ShapeDtypeStruct((Q * Bd, D_MODEL), F32)] * 2,
        scratch_shapes=[pltpu.VMEM((Q * Bd, 1), F32)],
        compiler_params=_params(40, 1), name="pool_sample",
    )(x, nw, pw, ps, *([sp] * POOL_BUF))


def _pool_sample_kernel(x_ref, nw_ref, pw_ref, ps_ref, *rest, Bd, Q, pos0):
    hist_refs = rest[:POOL_BUF]
    o_ref, hs_ref, inv_ref = rest[POOL_BUF:]
    g = pl.program_id(0)

    @pl.when(g == 0)
    def _():
        x = x_ref[...]
        inv_ref[...] = lax.rsqrt(jnp.mean(x * x, axis=-1, keepdims=True) + RMS_EPS)

    for wi, w in enumerate(POOL_WINDOWS):
        @pl.when(g == wi)
        def _(wi=wi, w=w):
            cols = slice(wi * POOL_GROUP, (wi + 1) * POOL_GROUP)
            xs, hs = [], []
            for q in range(Q):
                rows = slice(q * Bd, (q + 1) * Bd)
                xq = x_ref[rows, cols]
                xs.append(xq)
                hs.append(xq * inv_ref[rows, :] * nw_ref[:, cols])
                hs_ref[rows, :] = hs[q]
            ext = lambda j: hist_refs[j][...] if j < POOL_BUF else hs[j - POOL_BUF]
            for q in range(Q):
                acc = hs[q]
                for k in range(1, w):
                    acc = acc + ext(POOL_BUF + q - k)
                cnt = float(min(pos0 + q + 1, w))
                pooled = (acc / cnt - hs[q]).astype(BF16)
                val = jnp.dot(pooled, pw_ref[...], preferred_element_type=F32) * ps_ref[...]
                o_ref[q * Bd:(q + 1) * Bd, :] = xs[q] + val


def kernel(x_prompt, x_sample, cache_k, cache_v, cache_logf, state_s5_re, state_s5_im, state_pool, page_table, norm_mix_w, norm_ffn_w, w_in, b_f, q_norm_w, k_norm_w, s5_lam_re, s5_lam_im, s5_log_dt, s5_b_re, s5_b_im, s5_c_re, s5_c_im, s5_d, w_glu, b_glu, w_out, pool_w, pool_scale, ffn_w1, ffn_w3, ffn_w2):
    B, L, D = x_prompt.shape
    Bd, Q, _ = x_sample.shape
    depth = norm_mix_w.shape[0]
    n_attn, n_phys, page = cache_k.shape[:3]
    n_pages = page_table.shape[1]
    past_len = n_pages * page
    T = 16
    assert D == D_MODEL and L % T == 0

    xp = x_prompt.reshape(B * L, D)
    xs = jnp.swapaxes(x_sample, 0, 1).reshape(Q * Bd, D)
    from_tm = lambda a, *tail: jnp.swapaxes(a.reshape(Q, Bd, *tail), 0, 1)
    pt_flat = page_table.reshape(-1)
    assert (page * N_HEADS) % LANES == 0
    ck_flat = cache_k.reshape(n_attn * n_phys, page, N_HEADS, HEAD_DIM)
    cv_flat = cache_v.reshape(n_attn * n_phys, page, N_HEADS, HEAD_DIM)
    clf_flat = cache_logf.reshape(n_attn * n_phys, page * N_HEADS // LANES, LANES)

    kp, vp, fp, srp, sip, plp = [], [], [], [], [], []
    ksm, vsm, fsm, srs, sis, pls = [], [], [], [], [], []
    for layer in range(depth):
        i = layer // 2
        nw = norm_mix_w[layer].reshape(1, D)
        if layer % 2 == 0:
            w = w_in[i]
            wq = w[:, :D_ATTN].astype(BF16)
            wk = w[:, D_ATTN:2 * D_ATTN].astype(BF16)
            wv = w[:, 2 * D_ATTN:3 * D_ATTN].astype(BF16)
            wf = jnp.pad(w[:, 3 * D_ATTN:3 * D_ATTN + N_HEADS], ((0, 0), (0, LANES - N_HEADS))).astype(BF16)
            wu = w[:, 3 * D_ATTN + N_HEADS:].astype(BF16)
            bf = jnp.pad(b_f[i], (0, LANES - N_HEADS)).reshape(1, LANES)
            qg = q_norm_w[i].reshape(1, HEAD_DIM)
            kg = k_norm_w[i].reshape(1, HEAD_DIM)
            wg = w_glu[i].astype(BF16)
            bg = b_glu[i].reshape(1, D_SSM)
            wo = w_out[i].astype(BF16)
            lam_re = s5_lam_re[i].reshape(1, -1)
            lam_im = s5_lam_im[i].reshape(1, -1)
            log_dt = jnp.broadcast_to(s5_log_dt[i][:, None], (N_SSM_GROUPS, SSM_STATE)).reshape(1, -1)
            bd_re = jnp.swapaxes(s5_b_re[i], 1, 2)
            bd_im = jnp.swapaxes(s5_b_im[i], 1, 2)
            cd_re = jnp.swapaxes(s5_c_re[i], 1, 2)
            cd_im = jnp.swapaxes(s5_c_im[i], 1, 2)
            d_row = s5_d[i].reshape(1, D_SSM)

            qb, k, kb, v, vb, u, lf, c = _inproj(xp, nw, wq, wk, wv, wu, wf, bf, qg, kg, seq_len=L,
                                                 q_scale=HEAD_DIM ** -0.5 * math.log2(math.e))
            c_t = jnp.swapaxes(c[:, :N_HEADS].reshape(B, L, N_HEADS), 1, 2)
            att = _attn_prompt(qb, kb, vb, c, c_t[:, :, None, :], B, L)
            zeros = jnp.zeros((B, 1, N_SLABS * SLAB_STATES), F32)
            y, h_re, h_im = _s5(u, zeros, zeros, lam_re, lam_im, log_dt, bd_re, bd_im, cd_re, cd_im, d_row,
                                n_seq=B, T=T, NC=L // T, chained=True)
            xp = _mixout(xp, att, y, wg, bg, wo)
            kp.append(k.reshape(B, L, N_HEADS, HEAD_DIM))
            vp.append(v.reshape(B, L, N_HEADS, HEAD_DIM))
            fp.append(lf[:, :N_HEADS].reshape(B, L, N_HEADS))
            srp.append(h_re.reshape(B, N_SSM_GROUPS, SSM_STATE))
            sip.append(h_im.reshape(B, N_SSM_GROUPS, SSM_STATE))

            qb, k, kb, v, vb, u, lf = _inproj(xs, nw, wq, wk, wv, wu, wf, bf, qg, kg, seq_len=None,
                                              q_scale=HEAD_DIM ** -0.5)
            lfn = lf[:, :N_HEADS].reshape(Q, Bd, N_HEADS)
            heads = lambda a: a.astype(F32).reshape(Q, Bd, N_HEADS, HEAD_DIM)
            att = _attn_sample(pt_flat, heads(qb), heads(k), heads(v),
                               jnp.swapaxes(lfn, 0, 1).reshape(Bd, Q * N_HEADS, 1),
                               ck_flat, cv_flat, clf_flat, i, n_phys, n_pages, Bd, Q).astype(BF16)
            y, h_re, h_im = _s5(u, state_s5_re[i].reshape(Bd, -1), state_s5_im[i].reshape(Bd, -1),
                                lam_re, lam_im, log_dt, bd_re, bd_im, cd_re, cd_im, d_row,
                                n_seq=1, T=Q, NC=Bd, chained=False)
            xs = _mixout(xs, att.reshape(Q * Bd, D_ATTN), y, wg, bg, wo)
            ksm.append(from_tm(k, N_HEADS, HEAD_DIM))
            vsm.append(from_tm(v, N_HEADS, HEAD_DIM))
            fsm.append(jnp.swapaxes(lfn, 0, 1))
            srs.append(h_re.reshape(Bd, N_SSM_GROUPS, SSM_STATE))
            sis.append(h_im.reshape(Bd, N_SSM_GROUPS, SSM_STATE))
        else:
            pw = pool_w[i].astype(BF16)
            ps = pool_scale[i].reshape(1, D)
            xp, st = _pool_prompt(xp, nw, pw, ps, B, L)
            plp.append(st)
            xs, hs = _pool_sample(xs, nw, state_pool[i].reshape(Bd, POOL_BUF * D), pw, ps, Bd, Q, past_len)
            pls.append(jnp.concatenate([state_pool[i], from_tm(hs, D)], axis=1)[:, -POOL_BUF:])
        nf = norm_ffn_w[layer].reshape(1, D)
        xp = _ffn(xp, nf, ffn_w1, ffn_w3, ffn_w2, layer)
        xs = _ffn(xs, nf, ffn_w1, ffn_w3, ffn_w2, layer)
    return (xp.reshape(B, L, D), from_tm(xs, D),
            jnp.stack(kp), jnp.stack(vp), jnp.stack(fp), jnp.stack(srp), jnp.stack(sip), jnp.stack(plp),
            jnp.stack(ksm), jnp.stack(vsm), jnp.stack(fsm), jnp.stack(srs), jnp.stack(sis), jnp.stack(pls))
```

```python
import functools
import math

import jax
import jax.numpy as jnp
from jax import lax
from jax.experimental import pallas as pl
from jax.experimental.pallas import tpu as pltpu

F32 = jnp.float32
BF16 = jnp.bfloat16

D_MODEL = 2048
D_ATTN = D_MODEL // 2
HEAD_DIM = 128
N_HEADS = D_ATTN // HEAD_DIM
D_SSM = D_MODEL - D_ATTN
SSM_GROUP = 16
N_SSM_GROUPS = D_SSM // SSM_GROUP
SSM_STATE = 64
POOL_WINDOWS = (2, 4, 8, 16)
POOL_GROUP = D_MODEL // len(POOL_WINDOWS)
POOL_BUF = max(POOL_WINDOWS) - 1
RMS_EPS = 1e-6

LANES = 128
MXU_DIM = 256
SLAB = MXU_DIM
SLAB_STATES = (SLAB // SSM_GROUP) * SSM_STATE
N_SLABS = D_SSM // SLAB
VMEM_BYTES = 64 * 1024 * 1024


def _params(vmem_mb, n_axes):
    return pltpu.CompilerParams(dimension_semantics=("arbitrary",) * n_axes,
                                vmem_limit_bytes=min(vmem_mb * 1024 * 1024, VMEM_BYTES - 4 * 1024 * 1024))


def _rms(x, w):
    return x * lax.rsqrt(jnp.mean(x * x, axis=-1, keepdims=True) + RMS_EPS) * w


def _log_sigmoid(x):
    return jnp.minimum(x, 0.0) - jnp.log1p(jnp.exp(-jnp.abs(x)))


def _sigmoid(x):
    return 1.0 / (1.0 + jnp.exp(-x))


def _split3(x):
    hi = x.astype(BF16)
    r = x - hi.astype(F32)
    mid = r.astype(BF16)
    lo = (r - mid.astype(F32)).astype(BF16)
    return hi, mid, lo


def _cumsum_rows(x):
    n = x.shape[0]
    tri = (lax.broadcasted_iota(jnp.int32, (n, n), 1) <= lax.broadcasted_iota(jnp.int32, (n, n), 0)).astype(BF16)
    hi, mid, lo = _split3(x)
    z = jnp.dot(tri, jnp.concatenate([hi, mid, lo], axis=1), preferred_element_type=F32)
    return z[:, :LANES] + z[:, LANES:2 * LANES] + z[:, 2 * LANES:]


def _inproj_kernel(x_ref, nw_ref, wq_ref, wk_ref, wv_ref, wu_ref, wf_ref, bf_ref, qg_ref, kg_ref,
                   q_ref, k_ref, kb_ref, v_ref, vb_ref, u_ref, lf_ref, *rest, tiles_per_seq, scale):
    h = _rms(x_ref[...], nw_ref[...]).astype(BF16)
    for j in range(D_ATTN // MXU_DIM):
        cols = slice(j * MXU_DIM, (j + 1) * MXU_DIM)
        zq = jnp.dot(h, wq_ref[:, cols], preferred_element_type=F32)
        zk = jnp.dot(h, wk_ref[:, cols], preferred_element_type=F32)
        zv = jnp.dot(h, wv_ref[:, cols], preferred_element_type=F32)
        for t in range(MXU_DIM // HEAD_DIM):
            sub = slice(t * HEAD_DIM, (t + 1) * HEAD_DIM)
            dst = slice(j * MXU_DIM + t * HEAD_DIM, j * MXU_DIM + (t + 1) * HEAD_DIM)
            q_ref[:, dst] = (_rms(zq[:, sub], qg_ref[...]) * scale).astype(BF16)
            kn = _rms(zk[:, sub], kg_ref[...])
            k_ref[:, dst] = kn
            kb_ref[:, dst] = kn.astype(BF16)
        v_ref[:, cols] = zv
        vb_ref[:, cols] = zv.astype(BF16)
        u_ref[:, cols] = jnp.dot(h, wu_ref[:, cols], preferred_element_type=F32)
    lf = _log_sigmoid(jnp.dot(h, wf_ref[...], preferred_element_type=F32) + bf_ref[...])
    lf_ref[...] = lf
    if tiles_per_seq is not None:
        c_ref, carry_ref = rest

        @pl.when(pl.program_id(0) % tiles_per_seq == 0)
        def _():
            carry_ref[...] = jnp.zeros_like(carry_ref)

        c = _cumsum_rows(lf) + carry_ref[...]
        c_ref[...] = c
        carry_ref[...] = c[c.shape[0] - 1:, :]


def _inproj(x, nw, wq, wk, wv, wu, wf, bf, qg, kg, seq_len, q_scale):
    R = x.shape[0]
    tm = min(R, 512)
    row = lambda n: pl.BlockSpec((tm, n), lambda i: (i, 0))
    full = lambda a: pl.BlockSpec(a.shape, lambda i: (0,) * a.ndim, pipeline_mode=pl.Buffered(1))
    out_shape = [jax.ShapeDtypeStruct((R, D_ATTN), BF16), jax.ShapeDtypeStruct((R, D_ATTN), F32),
                 jax.ShapeDtypeStruct((R, D_ATTN), BF16), jax.ShapeDtypeStruct((R, D_ATTN), F32),
                 jax.ShapeDtypeStruct((R, D_ATTN), BF16), jax.ShapeDtypeStruct((R, D_SSM), F32),
                 jax.ShapeDtypeStruct((R, LANES), F32)]
    out_specs = [row(D_ATTN)] * 5 + [row(D_SSM), row(LANES)]
    scratch = []
    tiles_per_seq = None
    if seq_len is not None:
        assert seq_len % tm == 0
        tiles_per_seq = seq_len // tm
        out_shape.append(jax.ShapeDtypeStruct((R, LANES), F32))
        out_specs.append(row(LANES))
        scratch.append(pltpu.VMEM((1, LANES), F32))
    return pl.pallas_call(
        functools.partial(_inproj_kernel, tiles_per_seq=tiles_per_seq, scale=q_scale),
        grid=(R // tm,),
        in_specs=[row(D_MODEL), full(nw), full(wq), full(wk), full(wv), full(wu), full(wf), full(bf), full(qg), full(kg)],
        out_specs=out_specs, out_shape=out_shape, scratch_shapes=scratch,
        compiler_params=_params(52 if seq_len is not None else 38, 1), name="inproj",
    )(x, nw, wq, wk, wv, wu, wf, bf, qg, kg)


def _attn_prompt_kernel(q_ref, k_ref, v_ref, c_ref, ck_ref, o_ref, *, L, tq):
    nt = (((1,), (1,)), ((), ()))
    log2e = math.log2(math.e)
    head = lax.broadcasted_iota(jnp.int32, (tq, LANES), 1) == pl.program_id(1)
    for qi in range(L // tq):
        rows = slice(qi * tq, (qi + 1) * tq)
        q = q_ref[rows, :]
        cq = jnp.sum(jnp.where(head, c_ref[rows, :], 0.0), axis=-1, keepdims=True) * log2e
        m = l = acc = None
        for ki in range(qi + 1):
            cols = slice(ki * tq, (ki + 1) * tq)
            t = lax.dot_general(q, k_ref[cols, :], nt, preferred_element_type=F32) - ck_ref[:, cols] * log2e
            if ki == qi:
                keep = lax.broadcasted_iota(jnp.int32, (tq, tq), 1) <= lax.broadcasted_iota(jnp.int32, (tq, tq), 0)
                t = jnp.where(keep, t, -jnp.inf)
            mx = jnp.max(t, axis=-1, keepdims=True) + cq
            if ki == 0:
                m = mx
                p = jnp.exp2(t - (m - cq))
                l = jnp.sum(p, axis=-1, keepdims=True)
                acc = jnp.dot(p.astype(BF16), v_ref[cols, :], preferred_element_type=F32)
            else:
                m_new = jnp.maximum(m, mx)
                alpha = jnp.exp2(m - m_new)
                p = jnp.exp2(t - (m_new - cq))
                l = alpha * l + jnp.sum(p, axis=-1, keepdims=True)
                acc = alpha * acc + jnp.dot(p.astype(BF16), v_ref[cols, :], preferred_element_type=F32)
                m = m_new
        o_ref[rows, :] = (acc / l).astype(BF16)


def _attn_prompt(qb, kb, vb, c, ck, B, L):
    tq = min(L, 512)
    blk = pl.BlockSpec((L, HEAD_DIM), lambda b, h: (b, h))
    return pl.pallas_call(
        functools.partial(_attn_prompt_kernel, L=L, tq=tq),
        grid=(B, N_HEADS),
        in_specs=[blk, blk, blk,
                  pl.BlockSpec((L, LANES), lambda b, h: (b, 0)),
                  pl.BlockSpec((None, None, 1, L), lambda b, h: (b, h, 0, 0))],
        out_specs=blk, out_shape=jax.ShapeDtypeStruct((B * L, D_ATTN), BF16),
        compiler_params=_params(19, 2), name="attn_prompt",
    )(qb, kb, vb, c, ck)


def _attn_sample_kernel(pt_ref, q_ref, kn_ref, vn_ref, lfn_ref, *rest, n_pages, Q):
    k_refs = rest[:n_pages]
    v_refs = rest[n_pages:2 * n_pages]
    lf_refs = rest[2 * n_pages:3 * n_pages]
    o_ref, s_ref, d_ref = rest[3 * n_pages:]
    nt = (((1,), (1,)), ((), ()))
    H = N_HEADS
    M = Q * H
    page = k_refs[0].shape[0]
    PL = page * H
    RPP = PL // LANES

    qm = q_ref[...].reshape(M, HEAD_DIM)
    qmb = qm.astype(BF16)
    same_head = (lax.broadcasted_iota(jnp.int32, (M, LANES), 1) % H
                 == lax.broadcasted_iota(jnp.int32, (M, LANES), 0) % H)

    x = jnp.concatenate([r[...] for r in lf_refs], axis=0)
    lane = lax.broadcasted_iota(jnp.int32, x.shape, 1)
    sh = H
    while sh < LANES:
        x = x + jnp.where(lane >= sh, pltpu.roll(x, sh, axis=1), 0.0)
        sh *= 2
    tot = jnp.where(lane >= LANES - H, x, 0.0)
    inc = _cumsum_rows(tot)
    after = inc[inc.shape[0] - 1:, :] - inc
    sh = H
    while sh < LANES:
        after = after + pltpu.roll(after, sh, axis=1)
        sh *= 2
    row_tot = tot
    sh = H
    while sh < LANES:
        row_tot = row_tot + pltpu.roll(row_tot, sh, axis=1)
        sh *= 2
    d_ref[...] = after + (row_tot - x)

    lfn = lfn_ref[...]
    cum = [lfn[0:H, :]]
    for i in range(1, Q):
        cum.append(cum[-1] + lfn[i * H:(i + 1) * H, :])
    cum_q = jnp.concatenate(cum, axis=0)
    cum_k = jnp.concatenate([jnp.concatenate([c] * Q, axis=0) for c in cum], axis=1)

    qm_r = qmb.astype(F32)
    tile_q = lambda a: jnp.concatenate([a.astype(BF16).astype(F32)] * Q, axis=0)
    s_new = jnp.concatenate(
        [jnp.sum(qm_r * tile_q(kn_ref[i]), axis=-1, keepdims=True) for i in range(Q)], axis=1)
    s_new = s_new + (cum_q - cum_k)
    q_of_row = lax.broadcasted_iota(jnp.int32, (M, Q), 0) // H
    s_new = jnp.where(lax.broadcasted_iota(jnp.int32, (M, Q), 1) <= q_of_row, s_new, -jnp.inf)
    m = jnp.max(s_new, axis=-1, keepdims=True)

    for j in range(n_pages):
        kb = k_refs[j][...].reshape(PL, HEAD_DIM).astype(BF16)
        s = lax.dot_general(qmb, kb, nt, preferred_element_type=F32)
        for r in range(RPP):
            blk = s[:, r * LANES:(r + 1) * LANES] + (cum_q + d_ref[j * RPP + r:j * RPP + r + 1, :])
            blk = jnp.where(same_head, blk, -jnp.inf)
            s_ref[:, (j * RPP + r) * LANES:(j * RPP + r + 1) * LANES] = blk
            m = jnp.maximum(m, jnp.max(blk, axis=-1, keepdims=True))

    p_new = jnp.exp(s_new - m)
    l = jnp.sum(p_new, axis=-1, keepdims=True)
    p_new_r = p_new.astype(BF16).astype(F32)
    acc = p_new_r[:, 0:1] * tile_q(vn_ref[0])
    for i in range(1, Q):
        acc = acc + p_new_r[:, i:i + 1] * tile_q(vn_ref[i])
    for j in range(n_pages):
        p = jnp.exp(s_ref[:, j * PL:(j + 1) * PL] - m)
        l = l + jnp.sum(p, axis=-1, keepdims=True)
        vb = v_refs[j][...].reshape(PL, HEAD_DIM).astype(BF16)
        acc = acc + jnp.dot(p.astype(BF16), vb, preferred_element_type=F32)
    o_ref[...] = (acc / l).reshape(Q, H, HEAD_DIM)


def _attn_sample(page_table_flat, q4, k4, v4, lfn_col, cache_k, cache_v, cache_lf, layer, n_phys, n_pages, Bd, Q):
    page = cache_k.shape[1]
    base = layer * n_phys
    rpp = page * N_HEADS // LANES

    def paged(tail):
        return [pl.BlockSpec((None,) + tail, functools.partial(
            lambda b, pt, j: (pt[b * n_pages + j] + base,) + (0,) * len(tail), j=j)) for j in range(n_pages)]

    tok = pl.BlockSpec((Q, None, N_HEADS, HEAD_DIM), lambda b, pt: (0, b, 0, 0))
    grid_spec = pltpu.PrefetchScalarGridSpec(
        num_scalar_prefetch=1, grid=(Bd,),
        in_specs=[tok, tok, tok, pl.BlockSpec((None, Q * N_HEADS, 1), lambda b, pt: (b, 0, 0))]
        + paged((page, N_HEADS, HEAD_DIM)) + paged((page, N_HEADS, HEAD_DIM)) + paged((rpp, LANES)),
        out_specs=tok,
        scratch_shapes=[pltpu.VMEM((Q * N_HEADS, n_pages * page * N_HEADS), F32),
                        pltpu.VMEM((n_pages * rpp, LANES), F32)])
    return pl.pallas_call(
        functools.partial(_attn_sample_kernel, n_pages=n_pages, Q=Q),
        grid_spec=grid_spec, out_shape=jax.ShapeDtypeStruct((Q, Bd, N_HEADS, HEAD_DIM), F32),
        compiler_params=_params(41, 1), name="attn_sample",
    )(page_table_flat, q4, k4, v4, lfn_col, *([cache_k] * n_pages), *([cache_v] * n_pages), *([cache_lf] * n_pages))


def _s5_kernel(u_ref, h0r_ref, h0i_ref, lr_ref, li_ref, ldt_ref, btr_ref, bti_ref, ctr_ref, cti_ref, d_ref,
               y_ref, fr_ref, fi_ref, xr_ref, xi_ref, bdr_ref, bdi_ref, cdr_ref, cdi_ref, *scratch, T, NC, chained):
    @pl.when((pl.program_id(0) == 0) & (pl.program_id(1) == 0))
    def _():
        for ref in (bdr_ref, bdi_ref, cdr_ref, cdi_ref):
            ref[...] = jnp.zeros_like(ref)

    for g in range(SLAB // SSM_GROUP):
        rows_c, rows_p = slice(g * SSM_GROUP, (g + 1) * SSM_GROUP), slice(g * SSM_STATE, (g + 1) * SSM_STATE)
        bdr_ref[rows_c, rows_p] = btr_ref[g]
        bdi_ref[rows_c, rows_p] = bti_ref[g]
        cdr_ref[rows_p, rows_c] = ctr_ref[g]
        cdi_ref[rows_p, rows_c] = cti_ref[g]

    lr, li = lr_ref[...], li_ref[...]
    dt = jnp.exp(ldt_ref[...])
    mag = jnp.exp(lr * dt)
    ar, ai = mag * jnp.cos(li * dt), mag * jnp.sin(li * dt)
    den = lr * lr + li * li
    zr = ((ar - 1.0) * lr + ai * li) / den
    zi = (ai * lr - (ar - 1.0) * li) / den
    bdr, bdi = bdr_ref[...], bdi_ref[...]
    pr = (zr * bdr - zi * bdi).astype(BF16)
    pi = (zr * bdi + zi * bdr).astype(BF16)
    n_cb = SLAB_STATES // LANES
    per = MXU_DIM // LANES
    if chained:
        stage_ref, perm_ref = scratch[4:]
        for t in range(per):
            stage_ref[t] = u_ref[:, t * LANES:(t + 1) * LANES]
        for s in range(T):
            for t in range(per):
                perm_ref[s * NC:(s + 1) * NC, t * LANES:(t + 1) * LANES] = stage_ref[t, pl.ds(s, NC, stride=T), :]
        u = perm_ref[...]
    else:
        u = u_ref[...]
    ub = u.astype(BF16)
    for c2 in range(n_cb // per):
        cols = slice(c2 * MXU_DIM, (c2 + 1) * MXU_DIM)
        x_r = jnp.dot(ub, pr[:, cols], preferred_element_type=F32)
        x_i = jnp.dot(ub, pi[:, cols], preferred_element_type=F32)
        for t in range(per):
            xr_ref[c2 * per + t] = x_r[:, t * LANES:(t + 1) * LANES]
            xi_ref[c2 * per + t] = x_i[:, t * LANES:(t + 1) * LANES]

    def sweep(cb, hr, hi, store):
        cols = slice(cb * LANES, (cb + 1) * LANES)
        a_r, a_i = ar[:, cols], ai[:, cols]
        for s in range(T):
            rows = slice(s * NC, (s + 1) * NC)
            xr, xi = xr_ref[cb, rows, :], xi_ref[cb, rows, :]
            hr, hi = a_r * hr - a_i * hi + xr, a_r * hi + a_i * hr + xi
            if store:
                xr_ref[cb, rows, :] = hr
                xi_ref[cb, rows, :] = hi
        return hr, hi

    if chained:
        er_ref, ei_ref, gr_ref, gi_ref = scratch[:4]
        zero = jnp.zeros((NC, LANES), F32)
        for cb in range(n_cb):
            cols = slice(cb * LANES, (cb + 1) * LANES)
            er_ref[:, cols], ei_ref[:, cols] = sweep(cb, zero, zero, False)
        tr, ti = ar, ai
        for _ in range(T - 1):
            tr, ti = tr * ar - ti * ai, tr * ai + ti * ar

        def chunk_step(k, carry):
            sr, si = carry
            gr_ref[pl.ds(k, 1), :] = sr
            gi_ref[pl.ds(k, 1), :] = si
            er, ei = er_ref[pl.ds(k, 1), :], ei_ref[pl.ds(k, 1), :]
            return tr * sr - ti * si + er, tr * si + ti * sr + ei

        sr, si = lax.fori_loop(0, NC, chunk_step, (h0r_ref[...], h0i_ref[...]))
        fr_ref[...] = sr
        fi_ref[...] = si
        for cb in range(n_cb):
            cols = slice(cb * LANES, (cb + 1) * LANES)
            sweep(cb, gr_ref[:, cols], gi_ref[:, cols], True)
    else:
        for cb in range(n_cb):
            cols = slice(cb * LANES, (cb + 1) * LANES)
            hr, hi = sweep(cb, h0r_ref[:, cols], h0i_ref[:, cols], True)
            fr_ref[:, cols] = hr
            fi_ref[:, cols] = hi

    y = d_ref[...] * u
    for c2 in range(n_cb // per):
        rows = slice(c2 * MXU_DIM, (c2 + 1) * MXU_DIM)
        h_r = jnp.concatenate([xr_ref[c2 * per + t] for t in range(per)], axis=1).astype(BF16)
        h_i = jnp.concatenate([xi_ref[c2 * per + t] for t in range(per)], axis=1).astype(BF16)
        y = y + (jnp.dot(h_r, cdr_ref[rows, :].astype(BF16), preferred_element_type=F32)
                 - jnp.dot(h_i, cdi_ref[rows, :].astype(BF16), preferred_element_type=F32))
    if chained:
        for s in range(T):
            for t in range(per):
                stage_ref[t, pl.ds(s, NC, stride=T), :] = y[s * NC:(s + 1) * NC, t * LANES:(t + 1) * LANES]
        for t in range(per):
            y_ref[:, t * LANES:(t + 1) * LANES] = stage_ref[t]
    else:
        y_ref[...] = y


def _s5(u, h0r, h0i, lam_re, lam_im, log_dt, bt_re, bt_im, ct_re, ct_im, d_row, n_seq, T, NC, chained):
    rows = NC * T
    gps = SLAB // SSM_GROUP
    slab_row = lambda: pl.BlockSpec((1, SLAB_STATES), lambda b, s: (0, s))
    if chained:
        st_spec = pl.BlockSpec((None, 1, SLAB_STATES), lambda b, s: (b, 0, s))
        st_shape = jax.ShapeDtypeStruct((n_seq, 1, N_SLABS * SLAB_STATES), F32)
        scratch = [pltpu.VMEM((NC, SLAB_STATES), F32)] * 4 + [
            pltpu.VMEM((SLAB // LANES, rows, LANES), F32), pltpu.VMEM((rows, SLAB), F32)]
    else:
        st_spec = pl.BlockSpec((NC, SLAB_STATES), lambda b, s: (0, s))
        st_shape = jax.ShapeDtypeStruct((NC, N_SLABS * SLAB_STATES), F32)
        scratch = []
    return pl.pallas_call(
        functools.partial(_s5_kernel, T=T, NC=NC, chained=chained),
        grid=(n_seq, N_SLABS),
        in_specs=[pl.BlockSpec((rows, SLAB), lambda b, s: (b, s)), st_spec, st_spec,
                  slab_row(), slab_row(), slab_row(),
                  pl.BlockSpec((gps, SSM_GROUP, SSM_STATE), lambda b, s: (s, 0, 0)),
                  pl.BlockSpec((gps, SSM_GROUP, SSM_STATE), lambda b, s: (s, 0, 0)),
                  pl.BlockSpec((gps, SSM_STATE, SSM_GROUP), lambda b, s: (s, 0, 0)),
                  pl.BlockSpec((gps, SSM_STATE, SSM_GROUP), lambda b, s: (s, 0, 0)),
                  pl.BlockSpec((1, SLAB), lambda b, s: (0, s))],
        out_specs=[pl.BlockSpec((rows, SLAB), lambda b, s: (b, s)), st_spec, st_spec],
        out_shape=[jax.ShapeDtypeStruct((n_seq * rows, D_SSM), F32), st_shape, st_shape],
        scratch_shapes=[pltpu.VMEM((SLAB_STATES // LANES, rows, LANES), F32)] * 2
        + [pltpu.VMEM((SLAB, SLAB_STATES), F32)] * 2 + [pltpu.VMEM((SLAB_STATES, SLAB), F32)] * 2 + scratch,
        compiler_params=_params(42 if chained else 21, 2), name="s5",
    )(u, h0r, h0i, lam_re, lam_im, log_dt, bt_re, bt_im, ct_re, ct_im, d_row)


def _mixout_kernel(x_ref, att_ref, y_ref, wg_ref, bg_ref, wo_ref, o_ref):
    y = y_ref[...]
    z = 0.5 * y * (1.0 + jnp.tanh(math.sqrt(2.0 / math.pi) * (y + 0.044715 * (y * y * y))))
    gate = _sigmoid(jnp.dot(z.astype(BF16), wg_ref[...], preferred_element_type=F32) + bg_ref[...])
    ssm = (z * gate).astype(BF16)
    o_ref[...] = (x_ref[...] + jnp.dot(att_ref[...], wo_ref[0:D_ATTN, :], preferred_element_type=F32)
                  + jnp.dot(ssm, wo_ref[D_ATTN:, :], preferred_element_type=F32))


def _mixout(x, att, y, wg, bg, wo):
    R = x.shape[0]
    tm = min(R, 512)
    row = lambda n: pl.BlockSpec((tm, n), lambda i: (i, 0))
    full = lambda a: pl.BlockSpec(a.shape, lambda i: (0,) * a.ndim, pipeline_mode=pl.Buffered(1))
    return pl.pallas_call(
        _mixout_kernel, grid=(R // tm,),
        in_specs=[row(D_MODEL), row(D_ATTN), row(D_SSM), full(wg), full(bg), full(wo)],
        out_specs=row(D_MODEL), out_shape=jax.ShapeDtypeStruct((R, D_MODEL), F32),
        compiler_params=_params(42 if R > tm else 31, 1), name="mixout",
    )(x, att, y, wg, bg, wo)


def _ffn_kernel(x_ref, nw_ref, w1_ref, w3_ref, w2_ref, o_ref, h_ref):
    @pl.when(pl.program_id(1) == 0)
    def _():
        x = x_ref[...]
        h_ref[...] = _rms(x, nw_ref[...]).astype(BF16)
        o_ref[...] = x

    h = h_ref[...]
    for c in range(w1_ref.shape[1] // MXU_DIM):
        cols = slice(c * MXU_DIM, (c + 1) * MXU_DIM)
        a = jnp.dot(h, w1_ref[:, cols].astype(BF16), preferred_element_type=F32)
        b = jnp.dot(h, w3_ref[:, cols].astype(BF16), preferred_element_type=F32)
        g = (a * _sigmoid(a) * b).astype(BF16)
        o_ref[...] += jnp.dot(g, w2_ref[cols, :].astype(BF16), preferred_element_type=F32)


def _ffn(x, nw, w1, w3, w2, layer):
    R = x.shape[0]
    d_ff = w1.shape[2]
    tm = min(R, 1024)
    tf = 2 * MXU_DIM
    assert d_ff % tf == 0 and R % tm == 0
    return pl.pallas_call(
        _ffn_kernel, grid=(R // tm, d_ff // tf),
        in_specs=[pl.BlockSpec((tm, D_MODEL), lambda i, f: (i, 0), pipeline_mode=pl.Buffered(1)),
                  pl.BlockSpec((1, D_MODEL), lambda i, f: (0, 0)),
                  pl.BlockSpec((None, D_MODEL, tf), lambda i, f: (layer, 0, f)),
                  pl.BlockSpec((None, D_MODEL, tf), lambda i, f: (layer, 0, f)),
                  pl.BlockSpec((None, tf, D_MODEL), lambda i, f: (layer, f, 0))],
        out_specs=pl.BlockSpec((tm, D_MODEL), lambda i, f: (i, 0)),
        out_shape=jax.ShapeDtypeStruct((R, D_MODEL), F32),
        scratch_shapes=[pltpu.VMEM((tm, D_MODEL), BF16)],
        compiler_params=_params(60 if R > tm else 42, 2), name="ffn",
    )(x, nw, w1, w3, w2)


SUBLANES = 8
HIST = SUBLANES * (max(POOL_WINDOWS).bit_length() - 1)


def _pool_prompt_kernel(x_ref, nw_ref, pw_ref, ps_ref, o_ref, st_ref, ext_ref, lvl_ref, *, tm, tiles_per_seq):
    i = pl.program_id(0)
    t_in_seq = i % tiles_per_seq
    n = HIST + tm

    @pl.when(t_in_seq == 0)
    def _():
        ext_ref[0:HIST, :] = jnp.zeros((HIST, D_MODEL), F32)

    x = x_ref[...]
    hp = _rms(x, nw_ref[...])
    ext_ref[HIST:n, :] = hp
    pos = t_in_seq * tm + lax.broadcasted_iota(jnp.int32, (tm, 1), 0)
    for g, w in enumerate(POOL_WINDOWS):
        cols = slice(g * POOL_GROUP, (g + 1) * POOL_GROUP)
        n_levels = w.bit_length() - 1
        assert w == 1 << n_levels and SUBLANES * n_levels <= HIST
        acc = None
        for k in range(n_levels):
            start, sh = SUBLANES * (k + 1), 1 << k
            if k == 0:
                acc = ext_ref[start:n, cols] + ext_ref[start - sh:n - sh, cols]
            else:
                acc = lvl_ref[k - 1, start:n, :] + lvl_ref[k - 1, start - sh:n - sh, :]
            if k < n_levels - 1:
                lvl_ref[k, start:n, :] = acc
        acc = acc[HIST - SUBLANES * n_levels:, :]
        inv = 1.0 / jnp.minimum(pos + 1, w).astype(F32)
        pooled = (acc * inv - hp[:, cols]).astype(BF16)
        o_ref[:, cols] = x[:, cols] + jnp.dot(pooled, pw_ref[g], preferred_element_type=F32) * ps_ref[:, cols]
    st_ref[...] = ext_ref[tm + HIST - POOL_BUF:tm + HIST, :]
    ext_ref[0:HIST, :] = ext_ref[tm:tm + HIST, :]


def _pool_prompt(x, nw, pw, ps, B, L):
    tm = min(L, 512)
    assert L % tm == 0 and tm >= HIST
    tiles_per_seq = L // tm
    return pl.pallas_call(
        functools.partial(_pool_prompt_kernel, tm=tm, tiles_per_seq=tiles_per_seq),
        grid=(B * tiles_per_seq,),
        in_specs=[pl.BlockSpec((tm, D_MODEL), lambda i: (i, 0)),
                  pl.BlockSpec((1, D_MODEL), lambda i: (0, 0)),
                  pl.BlockSpec(pw.shape, lambda i: (0, 0, 0)),
                  pl.BlockSpec((1, D_MODEL), lambda i: (0, 0))],
        out_specs=[pl.BlockSpec((tm, D_MODEL), lambda i: (i, 0)),
                   pl.BlockSpec((None, POOL_BUF, D_MODEL), lambda i: (i // tiles_per_seq, 0, 0))],
        out_shape=[jax.ShapeDtypeStruct((B * L, D_MODEL), F32), jax.ShapeDtypeStruct((B, POOL_BUF, D_MODEL), F32)],
        scratch_shapes=[pltpu.VMEM((HIST + tm, D_MODEL), F32),
                        pltpu.VMEM((HIST // SUBLANES - 1, HIST + tm, POOL_GROUP), F32)],
        compiler_params=_params(31, 1), name="pool_prompt",
    )(x, nw, pw, ps)


def _pool_sample(x, nw, sp, pw, ps, Bd, Q, pos0):
    G = len(POOL_WINDOWS)
    hist = [pl.BlockSpec((Bd, POOL_GROUP), functools.partial(lambda g, j: (0, j * G + g), j=j)) for j in range(POOL_BUF)]
    col = pl.BlockSpec((Q * Bd, POOL_GROUP), lambda g: (0, g))
    return pl.pallas_call(
        functools.partial(_pool_sample_kernel, Bd=Bd, Q=Q, pos0=pos0),
        grid=(G,),
        in_specs=[pl.BlockSpec((Q * Bd, D_MODEL), lambda g: (0, 0)),
                  pl.BlockSpec((1, D_MODEL), lambda g: (0, 0)),
                  pl.BlockSpec((None, POOL_GROUP, POOL_GROUP), lambda g: (g, 0, 0)),
                  pl.BlockSpec((1, POOL_GROUP), lambda g: (0, g))] + hist,
        out_specs=[col, col],
        out_shape=[jax.ShapeDtypeStruct((Q * Bd, D_MODEL), F32)] * 2,
        scratch_shapes=[pltpu.VMEM((Q * Bd, 1), F32)],
        compiler_params=_params(23, 1), name="pool_sample",
    )(x, nw, pw, ps, *([sp] * POOL_BUF))


def _pool_sample_kernel(x_ref, nw_ref, pw_ref, ps_ref, *rest, Bd, Q, pos0):
    hist_refs = rest[:POOL_BUF]
    o_ref, hs_ref, inv_ref = rest[POOL_BUF:]
    g = pl.program_id(0)

    @pl.when(g == 0)
    def _():
        x = x_ref[...]
        inv_ref[...] = lax.rsqrt(jnp.mean(x * x, axis=-1, keepdims=True) + RMS_EPS)

    for wi, w in enumerate(POOL_WINDOWS):
        @pl.when(g == wi)
        def _(wi=wi, w=w):
            cols = slice(wi * POOL_GROUP, (wi + 1) * POOL_GROUP)
            xs, hs = [], []
            for q in range(Q):
                rows = slice(q * Bd, (q + 1) * Bd)
                xq = x_ref[rows, cols]
                xs.append(xq)
                hs.append(xq * inv_ref[rows, :] * nw_ref[:, cols])
                hs_ref[rows, :] = hs[q]
            ext = lambda j: hist_refs[j][...] if j < POOL_BUF else hs[j - POOL_BUF]
            for q in range(Q):
                acc = hs[q]
                for k in range(1, w):
                    acc = acc + ext(POOL_BUF + q - k)
                cnt = float(min(pos0 + q + 1, w))
                pooled = (acc / cnt - hs[q]).astype(BF16)
                val = jnp.dot(pooled, pw_ref[...], preferred_element_type=F32) * ps_ref[...]
                o_ref[q * Bd:(q + 1) * Bd, :] = xs[q] + val


def kernel(x_prompt, x_sample, cache_k, cache_v, cache_logf, state_s5_re, state_s5_im, state_pool, page_table, norm_mix_w, norm_ffn_w, w_in, b_f, q_norm_w, k_norm_w, s5_lam_re, s5_lam_im, s5_log_dt, s5_b_re, s5_b_im, s5_c_re, s5_c_im, s5_d, w_glu, b_glu, w_out, pool_w, pool_scale, ffn_w1, ffn_w3, ffn_w2):
    B, L, D = x_prompt.shape
    Bd, Q, _ = x_sample.shape
    depth = norm_mix_w.shape[0]
    n_attn, n_phys, page = cache_k.shape[:3]
    n_pages = page_table.shape[1]
    past_len = n_pages * page
    T = 16
    assert D == D_MODEL and L % T == 0

    xp = x_prompt.reshape(B * L, D)
    xs = jnp.swapaxes(x_sample, 0, 1).reshape(Q * Bd, D)
    from_tm = lambda a, *tail: jnp.swapaxes(a.reshape(Q, Bd, *tail), 0, 1)
    pt_flat = page_table.reshape(-1)
    assert (page * N_HEADS) % LANES == 0
    ck_flat = cache_k.reshape(n_attn * n_phys, page, N_HEADS, HEAD_DIM)
    cv_flat = cache_v.reshape(n_attn * n_phys, page, N_HEADS, HEAD_DIM)
    clf_flat = cache_logf.reshape(n_attn * n_phys, page * N_HEADS // LANES, LANES)

    kp, vp, fp, srp, sip, plp = [], [], [], [], [], []
    ksm, vsm, fsm, srs, sis, pls = [], [], [], [], [], []
    for layer in range(depth):
        i = layer // 2
        nw = norm_mix_w[layer].reshape(1, D)
        if layer % 2 == 0:
            w = w_in[i]
            wq = w[:, :D_ATTN].astype(BF16)
            wk = w[:, D_ATTN:2 * D_ATTN].astype(BF16)
            wv = w[:, 2 * D_ATTN:3 * D_ATTN].astype(BF16)
            wf = jnp.pad(w[:, 3 * D_ATTN:3 * D_ATTN + N_HEADS], ((0, 0), (0, LANES - N_HEADS))).astype(BF16)
            wu = w[:, 3 * D_ATTN + N_HEADS:].astype(BF16)
            bf = jnp.pad(b_f[i], (0, LANES - N_HEADS)).reshape(1, LANES)
            qg = q_norm_w[i].reshape(1, HEAD_DIM)
            kg = k_norm_w[i].reshape(1, HEAD_DIM)
            wg = w_glu[i].astype(BF16)
            bg = b_glu[i].reshape(1, D_SSM)
            wo = w_out[i].astype(BF16)
            lam_re = s5_lam_re[i].reshape(1, -1)
            lam_im = s5_lam_im[i].reshape(1, -1)
            log_dt = jnp.broadcast_to(s5_log_dt[i][:, None], (N_SSM_GROUPS, SSM_STATE)).reshape(1, -1)
            bd_re = jnp.swapaxes(s5_b_re[i], 1, 2)
            bd_im = jnp.swapaxes(s5_b_im[i], 1, 2)
            cd_re = jnp.swapaxes(s5_c_re[i], 1, 2)
            cd_im = jnp.swapaxes(s5_c_im[i], 1, 2)
            d_row = s5_d[i].reshape(1, D_SSM)

            qb, k, kb, v, vb, u, lf, c = _inproj(xp, nw, wq, wk, wv, wu, wf, bf, qg, kg, seq_len=L,
                                                 q_scale=HEAD_DIM ** -0.5 * math.log2(math.e))
            c_t = jnp.swapaxes(c[:, :N_HEADS].reshape(B, L, N_HEADS), 1, 2)
            att = _attn_prompt(qb, kb, vb, c, c_t[:, :, None, :], B, L)
            zeros = jnp.zeros((B, 1, N_SLABS * SLAB_STATES), F32)
            y, h_re, h_im = _s5(u, zeros, zeros, lam_re, lam_im, log_dt, bd_re, bd_im, cd_re, cd_im, d_row,
                                n_seq=B, T=T, NC=L // T, chained=True)
            xp = _mixout(xp, att, y, wg, bg, wo)
            kp.append(k.reshape(B, L, N_HEADS, HEAD_DIM))
            vp.append(v.reshape(B, L, N_HEADS, HEAD_DIM))
            fp.append(lf[:, :N_HEADS].reshape(B, L, N_HEADS))
            srp.append(h_re.reshape(B, N_SSM_GROUPS, SSM_STATE))
            sip.append(h_im.reshape(B, N_SSM_GROUPS, SSM_STATE))

            qb, k, kb, v, vb, u, lf = _inproj(xs, nw, wq, wk, wv, wu, wf, bf, qg, kg, seq_len=None,
                                              q_scale=HEAD_DIM ** -0.5)
            lfn = lf[:, :N_HEADS].reshape(Q, Bd, N_HEADS)
            heads = lambda a: a.astype(F32).reshape(Q, Bd, N_HEADS, HEAD_DIM)
            att = _attn_sample(pt_flat, heads(qb), heads(k), heads(v),
                               jnp.swapaxes(lfn, 0, 1).reshape(Bd, Q * N_HEADS, 1),
                               ck_flat, cv_flat, clf_flat, i, n_phys, n_pages, Bd, Q).astype(BF16)
            y, h_re, h_im = _s5(u, state_s5_re[i].reshape(Bd, -1), state_s5_im[i].reshape(Bd, -1),
                                lam_re, lam_im, log_dt, bd_re, bd_im, cd_re, cd_im, d_row,
                                n_seq=1, T=Q, NC=Bd, chained=False)
            xs = _mixout(xs, att.reshape(Q * Bd, D_ATTN), y, wg, bg, wo)
            ksm.append(from_tm(k, N_HEADS, HEAD_DIM))
            vsm.append(from_tm(v, N_HEADS, HEAD_DIM))
            fsm.append(jnp.swapaxes(lfn, 0, 1))
            srs.append(h_re.reshape(Bd, N_SSM_GROUPS, SSM_STATE))
            sis.append(h_im.reshape(Bd, N_SSM_GROUPS, SSM_STATE))
        else:
            pw = pool_w[i].astype(BF16)
            ps = pool_scale[i].reshape(1, D)
            xp, st = _pool_prompt(xp, nw, pw, ps, B, L)
            plp.append(st)
            xs, hs = _pool_sample(xs, nw, state_pool[i].reshape(Bd, POOL_BUF * D), pw, ps, Bd, Q, past_len)
            pls.append(jnp.concatenate([state_pool[i], from_tm(hs, D)], axis=1)[:, -POOL_BUF:])
        nf = norm_ffn_w[layer].reshape(1, D)
        xp = _ffn(xp, nf, ffn_w1, ffn_w3, ffn_w2, layer)
        xs = _ffn(xs, nf, ffn_w1, ffn_w3, ffn_w2, layer)
    return (xp.reshape(B, L, D), from_tm(xs, D),
            jnp.stack(kp), jnp.stack(vp), jnp.stack(fp), jnp.stack(srp), jnp.stack(sip), jnp.stack(plp),
            jnp.stack(ksm), jnp.stack(vsm), jnp.stack(fsm), jnp.stack(srs), jnp.stack(sis), jnp.stack(pls))
```

```python
import functools
import math

import jax
import jax.numpy as jnp
from jax import lax
from jax.experimental import pallas as pl
from jax.experimental.pallas import tpu as pltpu

F32 = jnp.float32
BF16 = jnp.bfloat16

D_MODEL = 2048
D_ATTN = D_MODEL // 2
HEAD_DIM = 128
N_HEADS = D_ATTN // HEAD_DIM
D_SSM = D_MODEL - D_ATTN
SSM_GROUP = 16
N_SSM_GROUPS = D_SSM // SSM_GROUP
SSM_STATE = 64
POOL_WINDOWS = (2, 4, 8, 16)
POOL_GROUP = D_MODEL // len(POOL_WINDOWS)
POOL_BUF = max(POOL_WINDOWS) - 1
RMS_EPS = 1e-6

LANES = 128
MXU_DIM = 256
SLAB = MXU_DIM
SLAB_STATES = (SLAB // SSM_GROUP) * SSM_STATE
N_SLABS = D_SSM // SLAB
VMEM_BYTES = 64 * 1024 * 1024


def _params(vmem_mb, n_axes):
    return pltpu.CompilerParams(dimension_semantics=("arbitrary",) * n_axes,
                                vmem_limit_bytes=min(vmem_mb * 1024 * 1024, VMEM_BYTES - 4 * 1024 * 1024))


def _rms(x, w):
    return x * lax.rsqrt(jnp.mean(x * x, axis=-1, keepdims=True) + RMS_EPS) * w


def _log_sigmoid(x):
    return jnp.minimum(x, 0.0) - jnp.log1p(jnp.exp(-jnp.abs(x)))


def _sigmoid(x):
    return 1.0 / (1.0 + jnp.exp(-x))


def _split3(x):
    hi = x.astype(BF16)
    r = x - hi.astype(F32)
    mid = r.astype(BF16)
    lo = (r - mid.astype(F32)).astype(BF16)
    return hi, mid, lo


def _cumsum_rows(x):
    n = x.shape[0]
    tri = (lax.broadcasted_iota(jnp.int32, (n, n), 1) <= lax.broadcasted_iota(jnp.int32, (n, n), 0)).astype(BF16)
    hi, mid, lo = _split3(x)
    z = jnp.dot(tri, jnp.concatenate([hi, mid, lo], axis=1), preferred_element_type=F32)
    return z[:, :LANES] + z[:, LANES:2 * LANES] + z[:, 2 * LANES:]


def _inproj_kernel(x_ref, nw_ref, wq_ref, wk_ref, wv_ref, wu_ref, wf_ref, bf_ref, qg_ref, kg_ref,
                   q_ref, k_ref, kb_ref, v_ref, vb_ref, u_ref, lf_ref, *rest, tiles_per_seq, scale):
    h = _rms(x_ref[...], nw_ref[...]).astype(BF16)
    for j in range(D_ATTN // MXU_DIM):
        cols = slice(j * MXU_DIM, (j + 1) * MXU_DIM)
        zq = jnp.dot(h, wq_ref[:, cols], preferred_element_type=F32)
        zk = jnp.dot(h, wk_ref[:, cols], preferred_element_type=F32)
        zv = jnp.dot(h, wv_ref[:, cols], preferred_element_type=F32)
        for t in range(MXU_DIM // HEAD_DIM):
            sub = slice(t * HEAD_DIM, (t + 1) * HEAD_DIM)
            dst = slice(j * MXU_DIM + t * HEAD_DIM, j * MXU_DIM + (t + 1) * HEAD_DIM)
            q_ref[:, dst] = (_rms(zq[:, sub], qg_ref[...]) * scale).astype(BF16)
            kn = _rms(zk[:, sub], kg_ref[...])
            k_ref[:, dst] = kn
            kb_ref[:, dst] = kn.astype(BF16)
        v_ref[:, cols] = zv
        vb_ref[:, cols] = zv.astype(BF16)
        u_ref[:, cols] = jnp.dot(h, wu_ref[:, cols], preferred_element_type=F32)
    lf = _log_sigmoid(jnp.dot(h, wf_ref[...], preferred_element_type=F32) + bf_ref[...])
    lf_ref[...] = lf
    if tiles_per_seq is not None:
        c_ref, carry_ref = rest

        @pl.when(pl.program_id(0) % tiles_per_seq == 0)
        def _():
            carry_ref[...] = jnp.zeros_like(carry_ref)

        c = _cumsum_rows(lf) + carry_ref[...]
        c_ref[...] = c
        carry_ref[...] = c[c.shape[0] - 1:, :]


def _inproj(x, nw, wq, wk, wv, wu, wf, bf, qg, kg, seq_len, q_scale):
    R = x.shape[0]
    tm = min(R, 512)
    row = lambda n: pl.BlockSpec((tm, n), lambda i: (i, 0))
    full = lambda a: pl.BlockSpec(a.shape, lambda i: (0,) * a.ndim, pipeline_mode=pl.Buffered(1))
    out_shape = [jax.ShapeDtypeStruct((R, D_ATTN), BF16), jax.ShapeDtypeStruct((R, D_ATTN), F32),
                 jax.ShapeDtypeStruct((R, D_ATTN), BF16), jax.ShapeDtypeStruct((R, D_ATTN), F32),
                 jax.ShapeDtypeStruct((R, D_ATTN), BF16), jax.ShapeDtypeStruct((R, D_SSM), F32),
                 jax.ShapeDtypeStruct((R, LANES), F32)]
    out_specs = [row(D_ATTN)] * 5 + [row(D_SSM), row(LANES)]
    scratch = []
    tiles_per_seq = None
    if seq_len is not None:
        assert seq_len % tm == 0
        tiles_per_seq = seq_len // tm
        out_shape.append(jax.ShapeDtypeStruct((R, LANES), F32))
        out_specs.append(row(LANES))
        scratch.append(pltpu.VMEM((1, LANES), F32))
    return pl.pallas_call(
        functools.partial(_inproj_kernel, tiles_per_seq=tiles_per_seq, scale=q_scale),
        grid=(R // tm,),
        in_specs=[row(D_MODEL), full(nw), full(wq), full(wk), full(wv), full(wu), full(wf), full(bf), full(qg), full(kg)],
        out_specs=out_specs, out_shape=out_shape, scratch_shapes=scratch,
        compiler_params=_params(52 if seq_len is not None else 38, 1), name="inproj",
    )(x, nw, wq, wk, wv, wu, wf, bf, qg, kg)


def _attn_prompt_kernel(q_ref, k_ref, v_ref, c_ref, ck_ref, o_ref, *, L, tq):
    nt = (((1,), (1,)), ((), ()))
    log2e = math.log2(math.e)
    head = lax.broadcasted_iota(jnp.int32, (tq, LANES), 1) == pl.program_id(1)
    for qi in range(L // tq):
        rows = slice(qi * tq, (qi + 1) * tq)
        q = q_ref[rows, :]
        cq = jnp.sum(jnp.where(head, c_ref[rows, :], 0.0), axis=-1, keepdims=True) * log2e
        m = l = acc = None
        for ki in range(qi + 1):
            cols = slice(ki * tq, (ki + 1) * tq)
            t = lax.dot_general(q, k_ref[cols, :], nt, preferred_element_type=F32) - ck_ref[:, cols] * log2e
            if ki == qi:
                keep = lax.broadcasted_iota(jnp.int32, (tq, tq), 1) <= lax.broadcasted_iota(jnp.int32, (tq, tq), 0)
                t = jnp.where(keep, t, -jnp.inf)
            mx = jnp.max(t, axis=-1, keepdims=True) + cq
            if ki == 0:
                m = mx
                p = jnp.exp2(t - (m - cq))
                l = jnp.sum(p, axis=-1, keepdims=True)
                acc = jnp.dot(p.astype(BF16), v_ref[cols, :], preferred_element_type=F32)
            else:
                m_new = jnp.maximum(m, mx)
                alpha = jnp.exp2(m - m_new)
                p = jnp.exp2(t - (m_new - cq))
                l = alpha * l + jnp.sum(p, axis=-1, keepdims=True)
                acc = alpha * acc + jnp.dot(p.astype(BF16), v_ref[cols, :], preferred_element_type=F32)
                m = m_new
        o_ref[rows, :] = (acc / l).astype(BF16)


def _attn_prompt(qb, kb, vb, c, ck, B, L):
    tq = min(L, 512)
    blk = pl.BlockSpec((L, HEAD_DIM), lambda b, h: (b, h))
    return pl.pallas_call(
        functools.partial(_attn_prompt_kernel, L=L, tq=tq),
        grid=(B, N_HEADS),
        in_specs=[blk, blk, blk,
                  pl.BlockSpec((L, LANES), lambda b, h: (b, 0)),
                  pl.BlockSpec((None, None, 1, L), lambda b, h: (b, h, 0, 0))],
        out_specs=blk, out_shape=jax.ShapeDtypeStruct((B * L, D_ATTN), BF16),
        compiler_params=_params(19, 2), name="attn_prompt",
    )(qb, kb, vb, c, ck)


def _attn_sample_kernel(pt_ref, q_ref, kn_ref, vn_ref, lfn_ref, *rest, n_pages, Q):
    k_refs = rest[:n_pages]
    v_refs = rest[n_pages:2 * n_pages]
    lf_refs = rest[2 * n_pages:3 * n_pages]
    o_ref, s_ref, d_ref = rest[3 * n_pages:]
    nt = (((1,), (1,)), ((), ()))
    H = N_HEADS
    M = Q * H
    page = k_refs[0].shape[0]
    PL = page * H
    RPP = PL // LANES

    qm = q_ref[...].reshape(M, HEAD_DIM)
    qmb = qm.astype(BF16)
    same_head = (lax.broadcasted_iota(jnp.int32, (M, LANES), 1) % H
                 == lax.broadcasted_iota(jnp.int32, (M, LANES), 0) % H)

    x = jnp.concatenate([r[...] for r in lf_refs], axis=0)
    lane = lax.broadcasted_iota(jnp.int32, x.shape, 1)
    sh = H
    while sh < LANES:
        x = x + jnp.where(lane >= sh, pltpu.roll(x, sh, axis=1), 0.0)
        sh *= 2
    tot = jnp.where(lane >= LANES - H, x, 0.0)
    inc = _cumsum_rows(tot)
    after = inc[inc.shape[0] - 1:, :] - inc
    sh = H
    while sh < LANES:
        after = after + pltpu.roll(after, sh, axis=1)
        sh *= 2
    row_tot = tot
    sh = H
    while sh < LANES:
        row_tot = row_tot + pltpu.roll(row_tot, sh, axis=1)
        sh *= 2
    d_ref[...] = after + (row_tot - x)

    lfn = lfn_ref[...]
    cum = [lfn[0:H, :]]
    for i in range(1, Q):
        cum.append(cum[-1] + lfn[i * H:(i + 1) * H, :])
    cum_q = jnp.concatenate(cum, axis=0)
    cum_k = jnp.concatenate([jnp.concatenate([c] * Q, axis=0) for c in cum], axis=1)

    qm_r = qmb.astype(F32)
    tile_q = lambda a: jnp.concatenate([a.astype(BF16).astype(F32)] * Q, axis=0)
    s_new = jnp.concatenate(
        [jnp.sum(qm_r * tile_q(kn_ref[i]), axis=-1, keepdims=True) for i in range(Q)], axis=1)
    s_new = s_new + (cum_q - cum_k)
    q_of_row = lax.broadcasted_iota(jnp.int32, (M, Q), 0) // H
    s_new = jnp.where(lax.broadcasted_iota(jnp.int32, (M, Q), 1) <= q_of_row, s_new, -jnp.inf)
    m = jnp.max(s_new, axis=-1, keepdims=True)

    for j in range(n_pages):
        kb = k_refs[j][...].reshape(PL, HEAD_DIM).astype(BF16)
        s = lax.dot_general(qmb, kb, nt, preferred_element_type=F32)
        for r in range(RPP):
            blk = s[:, r * LANES:(r + 1) * LANES] + (cum_q + d_ref[j * RPP + r:j * RPP + r + 1, :])
            blk = jnp.where(same_head, blk, -jnp.inf)
            s_ref[:, (j * RPP + r) * LANES:(j * RPP + r + 1) * LANES] = blk
            m = jnp.maximum(m, jnp.max(blk, axis=-1, keepdims=True))

    p_new = jnp.exp(s_new - m)
    l = jnp.sum(p_new, axis=-1, keepdims=True)
    p_new_r = p_new.astype(BF16).astype(F32)
    acc = p_new_r[:, 0:1] * tile_q(vn_ref[0])
    for i in range(1, Q):
        acc = acc + p_new_r[:, i:i + 1] * tile_q(vn_ref[i])
    for j in range(n_pages):
        p = jnp.exp(s_ref[:, j * PL:(j + 1) * PL] - m)
        l = l + jnp.sum(p, axis=-1, keepdims=True)
        vb = v_refs[j][...].reshape(PL, HEAD_DIM).astype(BF16)
        acc = acc + jnp.dot(p.astype(BF16), vb, preferred_element_type=F32)
    o_ref[...] = (acc / l).reshape(Q, H, HEAD_DIM)


def _attn_sample(page_table_flat, q4, k4, v4, lfn_col, cache_k, cache_v, cache_lf, layer, n_phys, n_pages, Bd, Q):
    page = cache_k.shape[1]
    base = layer * n_phys
    rpp = page * N_HEADS // LANES

    def paged(tail):
        return [pl.BlockSpec((None,) + tail, functools.partial(
            lambda b, pt, j: (pt[b * n_pages + j] + base,) + (0,) * len(tail), j=j)) for j in range(n_pages)]

    tok = pl.BlockSpec((Q, None, N_HEADS, HEAD_DIM), lambda b, pt: (0, b, 0, 0))
    grid_spec = pltpu.PrefetchScalarGridSpec(
        num_scalar_prefetch=1, grid=(Bd,),
        in_specs=[tok, tok, tok, pl.BlockSpec((None, Q * N_HEADS, 1), lambda b, pt: (b, 0, 0))]
        + paged((page, N_HEADS, HEAD_DIM)) + paged((page, N_HEADS, HEAD_DIM)) + paged((rpp, LANES)),
        out_specs=tok,
        scratch_shapes=[pltpu.VMEM((Q * N_HEADS, n_pages * page * N_HEADS), F32),
                        pltpu.VMEM((n_pages * rpp, LANES), F32)])
    return pl.pallas_call(
        functools.partial(_attn_sample_kernel, n_pages=n_pages, Q=Q),
        grid_spec=grid_spec, out_shape=jax.ShapeDtypeStruct((Q, Bd, N_HEADS, HEAD_DIM), F32),
        compiler_params=_params(41, 1), name="attn_sample",
    )(page_table_flat, q4, k4, v4, lfn_col, *([cache_k] * n_pages), *([cache_v] * n_pages), *([cache_lf] * n_pages))


def _s5_kernel(u_ref, h0r_ref, h0i_ref, lr_ref, li_ref, ldt_ref, btr_ref, bti_ref, ctr_ref, cti_ref, d_ref,
               y_ref, fr_ref, fi_ref, xr_ref, xi_ref, bdr_ref, bdi_ref, cdr_ref, cdi_ref, *scratch, T, NC, chained):
    @pl.when((pl.program_id(0) == 0) & (pl.program_id(1) == 0))
    def _():
        for ref in (bdr_ref, bdi_ref, cdr_ref, cdi_ref):
            ref[...] = jnp.zeros_like(ref)

    for g in range(SLAB // SSM_GROUP):
        rows_c, rows_p = slice(g * SSM_GROUP, (g + 1) * SSM_GROUP), slice(g * SSM_STATE, (g + 1) * SSM_STATE)
        bdr_ref[rows_c, rows_p] = btr_ref[g]
        bdi_ref[rows_c, rows_p] = bti_ref[g]
        cdr_ref[rows_p, rows_c] = ctr_ref[g]
        cdi_ref[rows_p, rows_c] = cti_ref[g]

    lr, li = lr_ref[...], li_ref[...]
    dt = jnp.exp(ldt_ref[...])
    mag = jnp.exp(lr * dt)
    ar, ai = mag * jnp.cos(li * dt), mag * jnp.sin(li * dt)
    den = lr * lr + li * li
    zr = ((ar - 1.0) * lr + ai * li) / den
    zi = (ai * lr - (ar - 1.0) * li) / den
    bdr, bdi = bdr_ref[...], bdi_ref[...]
    pr = (zr * bdr - zi * bdi).astype(BF16)
    pi = (zr * bdi + zi * bdr).astype(BF16)
    n_cb = SLAB_STATES // LANES
    per = MXU_DIM // LANES
    if chained:
        stage_ref, perm_ref = scratch[4:]
        for t in range(per):
            stage_ref[t] = u_ref[:, t * LANES:(t + 1) * LANES]
        for s in range(T):
            for t in range(per):
                perm_ref[s * NC:(s + 1) * NC, t * LANES:(t + 1) * LANES] = stage_ref[t, pl.ds(s, NC, stride=T), :]
        u = perm_ref[...]
    else:
        u = u_ref[...]
    ub = u.astype(BF16)
    for c2 in range(n_cb // per):
        cols = slice(c2 * MXU_DIM, (c2 + 1) * MXU_DIM)
        x_r = jnp.dot(ub, pr[:, cols], preferred_element_type=F32)
        x_i = jnp.dot(ub, pi[:, cols], preferred_element_type=F32)
        for t in range(per):
            xr_ref[c2 * per + t] = x_r[:, t * LANES:(t + 1) * LANES]
            xi_ref[c2 * per + t] = x_i[:, t * LANES:(t + 1) * LANES]

    def sweep(cb, hr, hi, store):
        cols = slice(cb * LANES, (cb + 1) * LANES)
        a_r, a_i = ar[:, cols], ai[:, cols]
        for s in range(T):
            rows = slice(s * NC, (s + 1) * NC)
            xr, xi = xr_ref[cb, rows, :], xi_ref[cb, rows, :]
            hr, hi = a_r * hr - a_i * hi + xr, a_r * hi + a_i * hr + xi
            if store:
                xr_ref[cb, rows, :] = hr
                xi_ref[cb, rows, :] = hi
        return hr, hi

    if chained:
        er_ref, ei_ref, gr_ref, gi_ref = scratch[:4]
        zero = jnp.zeros((NC, LANES), F32)
        for cb in range(n_cb):
            cols = slice(cb * LANES, (cb + 1) * LANES)
            er_ref[:, cols], ei_ref[:, cols] = sweep(cb, zero, zero, False)
        tr, ti = ar, ai
        for _ in range(T - 1):
            tr, ti = tr * ar - ti * ai, tr * ai + ti * ar

        def chunk_step(k, carry):
            sr, si = carry
            gr_ref[pl.ds(k, 1), :] = sr
            gi_ref[pl.ds(k, 1), :] = si
            er, ei = er_ref[pl.ds(k, 1), :], ei_ref[pl.ds(k, 1), :]
            return tr * sr - ti * si + er, tr * si + ti * sr + ei

        sr, si = lax.fori_loop(0, NC, chunk_step, (h0r_ref[...], h0i_ref[...]))
        fr_ref[...] = sr
        fi_ref[...] = si
        for cb in range(n_cb):
            cols = slice(cb * LANES, (cb + 1) * LANES)
            sweep(cb, gr_ref[:, cols], gi_ref[:, cols], True)
    else:
        for cb in range(n_cb):
            cols = slice(cb * LANES, (cb + 1) * LANES)
            hr, hi = sweep(cb, h0r_ref[:, cols], h0i_ref[:, cols], True)
            fr_ref[:, cols] = hr
            fi_ref[:, cols] = hi

    y = d_ref[...] * u
    for c2 in range(n_cb // per):
        rows = slice(c2 * MXU_DIM, (c2 + 1) * MXU_DIM)
        h_r = jnp.concatenate([xr_ref[c2 * per + t] for t in range(per)], axis=1).astype(BF16)
        h_i = jnp.concatenate([xi_ref[c2 * per + t] for t in range(per)], axis=1).astype(BF16)
        y = y + (jnp.dot(h_r, cdr_ref[rows, :].astype(BF16), preferred_element_type=F32)
                 - jnp.dot(h_i, cdi_ref[rows, :].astype(BF16), preferred_element_type=F32))
    if chained:
        for s in range(T):
            for t in range(per):
                stage_ref[t, pl.ds(s, NC, stride=T), :] = y[s * NC:(s + 1) * NC, t * LANES:(t + 1) * LANES]
        for t in range(per):
            y_ref[:, t * LANES:(t + 1) * LANES] = stage_ref[t]
    else:
        y_ref[...] = y


def _s5(u, h0r, h0i, lam_re, lam_im, log_dt, bt_re, bt_im, ct_re, ct_im, d_row, n_seq, T, NC, chained):
    rows = NC * T
    gps = SLAB // SSM_GROUP
    slab_row = lambda: pl.BlockSpec((1, SLAB_STATES), lambda b, s: (0, s))
    if chained:
        st_spec = pl.BlockSpec((None, 1, SLAB_STATES), lambda b, s: (b, 0, s))
        st_shape = jax.ShapeDtypeStruct((n_seq, 1, N_SLABS * SLAB_STATES), F32)
        scratch = [pltpu.VMEM((NC, SLAB_STATES), F32)] * 4 + [
            pltpu.VMEM((SLAB // LANES, rows, LANES), F32), pltpu.VMEM((rows, SLAB), F32)]
    else:
        st_spec = pl.BlockSpec((NC, SLAB_STATES), lambda b, s: (0, s))
        st_shape = jax.ShapeDtypeStruct((NC, N_SLABS * SLAB_STATES), F32)
        scratch = []
    return pl.pallas_call(
        functools.partial(_s5_kernel, T=T, NC=NC, chained=chained),
        grid=(n_seq, N_SLABS),
        in_specs=[pl.BlockSpec((rows, SLAB), lambda b, s: (b, s)), st_spec, st_spec,
                  slab_row(), slab_row(), slab_row(),
                  pl.BlockSpec((gps, SSM_GROUP, SSM_STATE), lambda b, s: (s, 0, 0)),
                  pl.BlockSpec((gps, SSM_GROUP, SSM_STATE), lambda b, s: (s, 0, 0)),
                  pl.BlockSpec((gps, SSM_STATE, SSM_GROUP), lambda b, s: (s, 0, 0)),
                  pl.BlockSpec((gps, SSM_STATE, SSM_GROUP), lambda b, s: (s, 0, 0)),
                  pl.BlockSpec((1, SLAB), lambda b, s: (0, s))],
        out_specs=[pl.BlockSpec((rows, SLAB), lambda b, s: (b, s)), st_spec, st_spec],
        out_shape=[jax.ShapeDtypeStruct((n_seq * rows, D_SSM), F32), st_shape, st_shape],
        scratch_shapes=[pltpu.VMEM((SLAB_STATES // LANES, rows, LANES), F32)] * 2
        + [pltpu.VMEM((SLAB, SLAB_STATES), F32)] * 2 + [pltpu.VMEM((SLAB_STATES, SLAB), F32)] * 2 + scratch,
        compiler_params=_params(42 if chained else 21, 2), name="s5",
    )(u, h0r, h0i, lam_re, lam_im, log_dt, bt_re, bt_im, ct_re, ct_im, d_row)


def _mixout_kernel(x_ref, att_ref, y_ref, wg_ref, bg_ref, wo_ref, o_ref):
    y = y_ref[...]
    z = 0.5 * y * (1.0 + jnp.tanh(math.sqrt(2.0 / math.pi) * (y + 0.044715 * (y * y * y))))
    gate = _sigmoid(jnp.dot(z.astype(BF16), wg_ref[...], preferred_element_type=F32) + bg_ref[...])
    ssm = (z * gate).astype(BF16)
    o_ref[...] = (x_ref[...] + jnp.dot(att_ref[...], wo_ref[0:D_ATTN, :], preferred_element_type=F32)
                  + jnp.dot(ssm, wo_ref[D_ATTN:, :], preferred_element_type=F32))


def _mixout(x, att, y, wg, bg, wo):
    R = x.shape[0]
    tm = min(R, 512)
    row = lambda n: pl.BlockSpec((tm, n), lambda i: (i, 0))
    full = lambda a: pl.BlockSpec(a.shape, lambda i: (0,) * a.ndim, pipeline_mode=pl.Buffered(1))
    return pl.pallas_call(
        _mixout_kernel, grid=(R // tm,),
        in_specs=[row(D_MODEL), row(D_ATTN), row(D_SSM), full(wg), full(bg), full(wo)],
        out_specs=row(D_MODEL), out_shape=jax.ShapeDtypeStruct((R, D_MODEL), F32),
        compiler_params=_params(42 if R > tm else 31, 1), name="mixout",
    )(x, att, y, wg, bg, wo)


def _ffn_kernel(x_ref, nw_ref, w1_ref, w3_ref, w2_ref, o_ref, h_ref):
    @pl.when(pl.program_id(1) == 0)
    def _():
        x = x_ref[...]
        h_ref[...] = _rms(x, nw_ref[...]).astype(BF16)
        o_ref[...] = x

    h = h_ref[...]
    for c in range(w1_ref.shape[1] // MXU_DIM):
        cols = slice(c * MXU_DIM, (c + 1) * MXU_DIM)
        a = jnp.dot(h, w1_ref[:, cols].astype(BF16), preferred_element_type=F32)
        b = jnp.dot(h, w3_ref[:, cols].astype(BF16), preferred_element_type=F32)
        g = (a * _sigmoid(a) * b).astype(BF16)
        o_ref[...] += jnp.dot(g, w2_ref[cols, :].astype(BF16), preferred_element_type=F32)


def _ffn(x, nw, w1, w3, w2, layer):
    R = x.shape[0]
    d_ff = w1.shape[2]
    tm = min(R, 1024)
    tf = MXU_DIM
    assert d_ff % tf == 0 and R % tm == 0
    return pl.pallas_call(
        _ffn_kernel, grid=(R // tm, d_ff // tf),
        in_specs=[pl.BlockSpec((tm, D_MODEL), lambda i, f: (i, 0), pipeline_mode=pl.Buffered(1)),
                  pl.BlockSpec((1, D_MODEL), lambda i, f: (0, 0)),
                  pl.BlockSpec((None, D_MODEL, tf), lambda i, f: (layer, 0, f)),
                  pl.BlockSpec((None, D_MODEL, tf), lambda i, f: (layer, 0, f)),
                  pl.BlockSpec((None, tf, D_MODEL), lambda i, f: (layer, f, 0))],
        out_specs=pl.BlockSpec((tm, D_MODEL), lambda i, f: (i, 0)),
        out_shape=jax.ShapeDtypeStruct((R, D_MODEL), F32),
        scratch_shapes=[pltpu.VMEM((tm, D_MODEL), BF16)],
        compiler_params=_params(46 if R > tm else 20, 2), name="ffn",
    )(x, nw, w1, w3, w2)


SUBLANES = 8
HIST = SUBLANES * (max(POOL_WINDOWS).bit_length() - 1)


def _pool_prompt_kernel(x_ref, nw_ref, pw_ref, ps_ref, o_ref, st_ref, ext_ref, lvl_ref, *, tm, tiles_per_seq):
    i = pl.program_id(0)
    t_in_seq = i % tiles_per_seq
    n = HIST + tm

    @pl.when(t_in_seq == 0)
    def _():
        ext_ref[0:HIST, :] = jnp.zeros((HIST, D_MODEL), F32)

    x = x_ref[...]
    hp = _rms(x, nw_ref[...])
    ext_ref[HIST:n, :] = hp
    pos = t_in_seq * tm + lax.broadcasted_iota(jnp.int32, (tm, 1), 0)
    for g, w in enumerate(POOL_WINDOWS):
        cols = slice(g * POOL_GROUP, (g + 1) * POOL_GROUP)
        n_levels = w.bit_length() - 1
        assert w == 1 << n_levels and SUBLANES * n_levels <= HIST
        acc = None
        for k in range(n_levels):
            start, sh = SUBLANES * (k + 1), 1 << k
            if k == 0:
                acc = ext_ref[start:n, cols] + ext_ref[start - sh:n - sh, cols]
            else:
                acc = lvl_ref[k - 1, start:n, :] + lvl_ref[k - 1, start - sh:n - sh, :]
            if k < n_levels - 1:
                lvl_ref[k, start:n, :] = acc
        acc = acc[HIST - SUBLANES * n_levels:, :]
        inv = 1.0 / jnp.minimum(pos + 1, w).astype(F32)
        pooled = (acc * inv - hp[:, cols]).astype(BF16)
        o_ref[:, cols] = x[:, cols] + jnp.dot(pooled, pw_ref[g], preferred_element_type=F32) * ps_ref[:, cols]
    st_ref[...] = ext_ref[tm + HIST - POOL_BUF:tm + HIST, :]
    ext_ref[0:HIST, :] = ext_ref[tm:tm + HIST, :]


def _pool_prompt(x, nw, pw, ps, B, L):
    tm = min(L, 512)
    assert L % tm == 0 and tm >= HIST
    tiles_per_seq = L // tm
    return pl.pallas_call(
        functools.partial(_pool_prompt_kernel, tm=tm, tiles_per_seq=tiles_per_seq),
        grid=(B * tiles_per_seq,),
        in_specs=[pl.BlockSpec((tm, D_MODEL), lambda i: (i, 0)),
                  pl.BlockSpec((1, D_MODEL), lambda i: (0, 0)),
                  pl.BlockSpec(pw.shape, lambda i: (0, 0, 0)),
                  pl.BlockSpec((1, D_MODEL), lambda i: (0, 0))],
        out_specs=[pl.BlockSpec((tm, D_MODEL), lambda i: (i, 0)),
                   pl.BlockSpec((None, POOL_BUF, D_MODEL), lambda i: (i // tiles_per_seq, 0, 0))],
        out_shape=[jax.ShapeDtypeStruct((B * L, D_MODEL), F32), jax.ShapeDtypeStruct((B, POOL_BUF, D_MODEL), F32)],
        scratch_shapes=[pltpu.VMEM((HIST + tm, D_MODEL), F32),
                        pltpu.VMEM((HIST // SUBLANES - 1, HIST + tm, POOL_GROUP), F32)],
        compiler_params=_params(31, 1), name="pool_prompt",
    )(x, nw, pw, ps)


def _pool_sample(x, nw, sp, pw, ps, Bd, Q, pos0):
    G = len(POOL_WINDOWS)
    hist = [pl.BlockSpec((Bd, POOL_GROUP), functools.partial(lambda g, j: (0, j * G + g), j=j)) for j in range(POOL_BUF)]
    col = pl.BlockSpec((Q * Bd, POOL_GROUP), lambda g: (0, g))
    return pl.pallas_call(
        functools.partial(_pool_sample_kernel, Bd=Bd, Q=Q, pos0=pos0),
        grid=(G,),
        in_specs=[pl.BlockSpec((Q * Bd, D_MODEL), lambda g: (0, 0)),
                  pl.BlockSpec((1, D_MODEL), lambda g: (0, 0)),
                  pl.BlockSpec((None, POOL_GROUP, POOL_GROUP), lambda g: (g, 0, 0)),
                  pl.BlockSpec((1, POOL_GROUP), lambda g: (0, g))] + hist,
        out_specs=[col, col],
        out_shape=[jax.ShapeDtypeStruct((Q * Bd, D_MODEL), F32)] * 2,
        scratch_shapes=[pltpu.VMEM((Q * Bd, 1), F32)],
        compiler_params=_params(23, 1), name="pool_sample",
    )(x, nw, pw, ps, *([sp] * POOL_BUF))


def _pool_sample_kernel(x_ref, nw_ref, pw_ref, ps_ref, *rest, Bd, Q, pos0):
    hist_refs = rest[:POOL_BUF]
    o_ref, hs_ref, inv_ref = rest[POOL_BUF:]
    g = pl.program_id(0)

    @pl.when(g == 0)
    def _():
        x = x_ref[...]
        inv_ref[...] = lax.rsqrt(jnp.mean(x * x, axis=-1, keepdims=True) + RMS_EPS)

    for wi, w in enumerate(POOL_WINDOWS):
        @pl.when(g == wi)
        def _(wi=wi, w=w):
            cols = slice(wi * POOL_GROUP, (wi + 1) * POOL_GROUP)
            xs, hs = [], []
            for q in range(Q):
                rows = slice(q * Bd, (q + 1) * Bd)
                xq = x_ref[rows, cols]
                xs.append(xq)
                hs.append(xq * inv_ref[rows, :] * nw_ref[:, cols])
                hs_ref[rows, :] = hs[q]
            ext = lambda j: hist_refs[j][...] if j < POOL_BUF else hs[j - POOL_BUF]
            for q in range(Q):
                acc = hs[q]
                for k in range(1, w):
                    acc = acc + ext(POOL_BUF + q - k)
                cnt = float(min(pos0 + q + 1, w))
                pooled = (acc / cnt - hs[q]).astype(BF16)
                val = jnp.dot(pooled, pw_ref[...], preferred_element_type=F32) * ps_ref[...]
                o_ref[q * Bd:(q + 1) * Bd, :] = xs[q] + val


def kernel(x_prompt, x_sample, cache_k, cache_v, cache_logf, state_s5_re, state_s5_im, state_pool, page_table, norm_mix_w, norm_ffn_w, w_in, b_f, q_norm_w, k_norm_w, s5_lam_re, s5_lam_im, s5_log_dt, s5_b_re, s5_b_im, s5_c_re, s5_c_im, s5_d, w_glu, b_glu, w_out, pool_w, pool_scale, ffn_w1, ffn_w3, ffn_w2):
    B, L, D = x_prompt.shape
    Bd, Q, _ = x_sample.shape
    depth = norm_mix_w.shape[0]
    n_attn, n_phys, page = cache_k.shape[:3]
    n_pages = page_table.shape[1]
    past_len = n_pages * page
    T = 16
    assert D == D_MODEL and L % T == 0

    xp = x_prompt.reshape(B * L, D)
    xs = jnp.swapaxes(x_sample, 0, 1).reshape(Q * Bd, D)
    from_tm = lambda a, *tail: jnp.swapaxes(a.reshape(Q, Bd, *tail), 0, 1)
    pt_flat = page_table.reshape(-1)
    assert (page * N_HEADS) % LANES == 0
    ck_flat = cache_k.reshape(n_attn * n_phys, page, N_HEADS, HEAD_DIM)
    cv_flat = cache_v.reshape(n_attn * n_phys, page, N_HEADS, HEAD_DIM)
    clf_flat = cache_logf.reshape(n_attn * n_phys, page * N_HEADS // LANES, LANES)

    kp, vp, fp, srp, sip, plp = [], [], [], [], [], []
    ksm, vsm, fsm, srs, sis, pls = [], [], [], [], [], []
    for layer in range(depth):
        i = layer // 2
        nw = norm_mix_w[layer].reshape(1, D)
        if layer % 2 == 0:
            w = w_in[i]
            wq = w[:, :D_ATTN].astype(BF16)
            wk = w[:, D_ATTN:2 * D_ATTN].astype(BF16)
            wv = w[:, 2 * D_ATTN:3 * D_ATTN].astype(BF16)
            wf = jnp.pad(w[:, 3 * D_ATTN:3 * D_ATTN + N_HEADS], ((0, 0), (0, LANES - N_HEADS))).astype(BF16)
            wu = w[:, 3 * D_ATTN + N_HEADS:].astype(BF16)
            bf = jnp.pad(b_f[i], (0, LANES - N_HEADS)).reshape(1, LANES)
            qg = q_norm_w[i].reshape(1, HEAD_DIM)
            kg = k_norm_w[i].reshape(1, HEAD_DIM)
            wg = w_glu[i].astype(BF16)
            bg = b_glu[i].reshape(1, D_SSM)
            wo = w_out[i].astype(BF16)
            lam_re = s5_lam_re[i].reshape(1, -1)
            lam_im = s5_lam_im[i].reshape(1, -1)
            log_dt = jnp.broadcast_to(s5_log_dt[i][:, None], (N_SSM_GROUPS, SSM_STATE)).reshape(1, -1)
            bd_re = jnp.swapaxes(s5_b_re[i], 1, 2)
            bd_im = jnp.swapaxes(s5_b_im[i], 1, 2)
            cd_re = jnp.swapaxes(s5_c_re[i], 1, 2)
            cd_im = jnp.swapaxes(s5_c_im[i], 1, 2)
            d_row = s5_d[i].reshape(1, D_SSM)

            qb, k, kb, v, vb, u, lf, c = _inproj(xp, nw, wq, wk, wv, wu, wf, bf, qg, kg, seq_len=L,
                                                 q_scale=HEAD_DIM ** -0.5 * math.log2(math.e))
            c_t = jnp.swapaxes(c[:, :N_HEADS].reshape(B, L, N_HEADS), 1, 2)
            att = _attn_prompt(qb, kb, vb, c, c_t[:, :, None, :], B, L)
            zeros = jnp.zeros((B, 1, N_SLABS * SLAB_STATES), F32)
            y, h_re, h_im = _s5(u, zeros, zeros, lam_re, lam_im, log_dt, bd_re, bd_im, cd_re, cd_im, d_row,
                                n_seq=B, T=T, NC=L // T, chained=True)
            xp = _mixout(xp, att, y, wg, bg, wo)
            kp.append(k.reshape(B, L, N_HEADS, HEAD_DIM))
            vp.append(v.reshape(B, L, N_HEADS, HEAD_DIM))
            fp.append(lf[:, :N_HEADS].reshape(B, L, N_HEADS))
            srp.append(h_re.reshape(B, N_SSM_GROUPS, SSM_STATE))
            sip.append(h_im.reshape(B, N_SSM_GROUPS, SSM_STATE))

            qb, k, kb, v, vb, u, lf = _inproj(xs, nw, wq, wk, wv, wu, wf, bf, qg, kg, seq_len=None,
                                              q_scale=HEAD_DIM ** -0.5)
            lfn = lf[:, :N_HEADS].reshape(Q, Bd, N_HEADS)
            heads = lambda a: a.astype(F32).reshape(Q, Bd, N_HEADS, HEAD_DIM)
            att = _attn_sample(pt_flat, heads(qb), heads(k), heads(v),
                               jnp.swapaxes(lfn, 0, 1).reshape(Bd, Q * N_HEADS, 1),
                               ck_flat, cv_flat, clf_flat, i, n_phys, n_pages, Bd, Q).astype(BF16)
            y, h_re, h_im = _s5(u, state_s5_re[i].reshape(Bd, -1), state_s5_im[i].reshape(Bd, -1),
                                lam_re, lam_im, log_dt, bd_re, bd_im, cd_re, cd_im, d_row,
                                n_seq=1, T=Q, NC=Bd, chained=False)
            xs = _mixout(xs, att.reshape(Q * Bd, D_ATTN), y, wg, bg, wo)
            ksm.append(from_tm(k, N_HEADS, HEAD_DIM))
            vsm.append(from_tm(v, N_HEADS, HEAD_DIM))
            fsm.append(jnp.swapaxes(lfn, 0, 1))
            srs.append(h_re.reshape(Bd, N_SSM_GROUPS, SSM_STATE))
            sis.append(h_im.reshape(Bd, N_SSM_GROUPS, SSM_STATE))
        else:
            pw = pool_w[i].astype(BF16)
            ps = pool_scale[i].reshape(1, D)
            xp, st = _pool_prompt(xp, nw, pw, ps, B, L)
            plp.append(st)
            xs, hs = _pool_sample(xs, nw, state_pool[i].reshape(Bd, POOL_BUF * D), pw, ps, Bd, Q, past_len)
            pls.append(jnp.concatenate([state_pool[i], from_tm(hs, D)], axis=1)[:, -POOL_BUF:])
        nf = norm_ffn_w[layer].reshape(1, D)
        xp = _ffn(xp, nf, ffn_w1, ffn_w3, ffn_w2, layer)
        xs = _ffn(xs, nf, ffn_w1, ffn_w3, ffn_w2, layer)
    return (xp.reshape(B, L, D), from_tm(xs, D),
            jnp.stack(kp), jnp.stack(vp), jnp.stack(fp), jnp.stack(srp), jnp.stack(sip), jnp.stack(plp),
            jnp.stack(ksm), jnp.stack(vsm), jnp.stack(fsm), jnp.stack(srs), jnp.stack(sis), jnp.stack(pls))
```

```python
import functools
import math

import jax
import jax.numpy as jnp
from jax import lax
from jax.experimental import pallas as pl
from jax.experimental.pallas import tpu as pltpu

F32 = jnp.float32
BF16 = jnp.bfloat16

D_MODEL = 2048
D_ATTN = D_MODEL // 2
HEAD_DIM = 128
N_HEADS = D_ATTN // HEAD_DIM
D_SSM = D_MODEL - D_ATTN
SSM_GROUP = 16
N_SSM_GROUPS = D_SSM // SSM_GROUP
SSM_STATE = 64
POOL_WINDOWS = (2, 4, 8, 16)
POOL_GROUP = D_MODEL // len(POOL_WINDOWS)
POOL_BUF = max(POOL_WINDOWS) - 1
RMS_EPS = 1e-6

LANES = 128
MXU_DIM = 256
SLAB = MXU_DIM
SLAB_STATES = (SLAB // SSM_GROUP) * SSM_STATE
N_SLABS = D_SSM // SLAB
VMEM_BYTES = 64 * 1024 * 1024


def _params(vmem_mb, n_axes):
    return pltpu.CompilerParams(dimension_semantics=("arbitrary",) * n_axes,
                                vmem_limit_bytes=min(vmem_mb * 1024 * 1024, VMEM_BYTES - 4 * 1024 * 1024))


def _rms(x, w):
    return x * lax.rsqrt(jnp.mean(x * x, axis=-1, keepdims=True) + RMS_EPS) * w


def _log_sigmoid(x):
    return jnp.minimum(x, 0.0) - jnp.log1p(jnp.exp(-jnp.abs(x)))


def _sigmoid(x):
    return 1.0 / (1.0 + jnp.exp(-x))


def _split3(x):
    hi = x.astype(BF16)
    r = x - hi.astype(F32)
    mid = r.astype(BF16)
    lo = (r - mid.astype(F32)).astype(BF16)
    return hi, mid, lo


def _cumsum_rows(x):
    n = x.shape[0]
    tri = (lax.broadcasted_iota(jnp.int32, (n, n), 1) <= lax.broadcasted_iota(jnp.int32, (n, n), 0)).astype(BF16)
    hi, mid, lo = _split3(x)
    z = jnp.dot(tri, jnp.concatenate([hi, mid, lo], axis=1), preferred_element_type=F32)
    return z[:, :LANES] + z[:, LANES:2 * LANES] + z[:, 2 * LANES:]


def _inproj_kernel(x_ref, nw_ref, wq_ref, wk_ref, wv_ref, wu_ref, wf_ref, bf_ref, qg_ref, kg_ref,
                   q_ref, k_ref, kb_ref, v_ref, vb_ref, u_ref, lf_ref, *rest, tiles_per_seq, scale):
    h = _rms(x_ref[...], nw_ref[...]).astype(BF16)
    proj = lambda w_ref, cols: lax.dot_general(h, w_ref[cols, :], (((1,), (1,)), ((), ())), preferred_element_type=F32)
    for j in range(D_ATTN // MXU_DIM):
        cols = slice(j * MXU_DIM, (j + 1) * MXU_DIM)
        zq = proj(wq_ref, cols)
        zk = proj(wk_ref, cols)
        zv = proj(wv_ref, cols)
        for t in range(MXU_DIM // HEAD_DIM):
            sub = slice(t * HEAD_DIM, (t + 1) * HEAD_DIM)
            dst = slice(j * MXU_DIM + t * HEAD_DIM, j * MXU_DIM + (t + 1) * HEAD_DIM)
            q_ref[:, dst] = (_rms(zq[:, sub], qg_ref[...]) * scale).astype(BF16)
            kn = _rms(zk[:, sub], kg_ref[...])
            k_ref[:, dst] = kn
            kb_ref[:, dst] = kn.astype(BF16)
        v_ref[:, cols] = zv
        vb_ref[:, cols] = zv.astype(BF16)
        u_ref[:, cols] = proj(wu_ref, cols)
    lf = _log_sigmoid(proj(wf_ref, slice(None)) + bf_ref[...])
    lf_ref[...] = lf
    if tiles_per_seq is not None:
        c_ref, carry_ref = rest

        @pl.when(pl.program_id(0) % tiles_per_seq == 0)
        def _():
            carry_ref[...] = jnp.zeros_like(carry_ref)

        c = _cumsum_rows(lf) + carry_ref[...]
        c_ref[...] = c
        carry_ref[...] = c[c.shape[0] - 1:, :]


def _inproj(x, nw, wq, wk, wv, wu, wf, bf, qg, kg, seq_len, q_scale):
    R = x.shape[0]
    tm = min(R, 512)
    row = lambda n: pl.BlockSpec((tm, n), lambda i: (i, 0))
    full = lambda a: pl.BlockSpec(a.shape, lambda i: (0,) * a.ndim, pipeline_mode=pl.Buffered(1))
    out_shape = [jax.ShapeDtypeStruct((R, D_ATTN), BF16), jax.ShapeDtypeStruct((R, D_ATTN), F32),
                 jax.ShapeDtypeStruct((R, D_ATTN), BF16), jax.ShapeDtypeStruct((R, D_ATTN), F32),
                 jax.ShapeDtypeStruct((R, D_ATTN), BF16), jax.ShapeDtypeStruct((R, D_SSM), F32),
                 jax.ShapeDtypeStruct((R, LANES), F32)]
    out_specs = [row(D_ATTN)] * 5 + [row(D_SSM), row(LANES)]
    scratch = []
    tiles_per_seq = None
    if seq_len is not None:
        assert seq_len % tm == 0
        tiles_per_seq = seq_len // tm
        out_shape.append(jax.ShapeDtypeStruct((R, LANES), F32))
        out_specs.append(row(LANES))
        scratch.append(pltpu.VMEM((1, LANES), F32))
    return pl.pallas_call(
        functools.partial(_inproj_kernel, tiles_per_seq=tiles_per_seq, scale=q_scale),
        grid=(R // tm,),
        in_specs=[row(D_MODEL), full(nw), full(wq), full(wk), full(wv), full(wu), full(wf), full(bf), full(qg), full(kg)],
        out_specs=out_specs, out_shape=out_shape, scratch_shapes=scratch,
        compiler_params=_params(52 if seq_len is not None else 38, 1), name="inproj",
    )(x, nw, wq, wk, wv, wu, wf, bf, qg, kg)


def _attn_prompt_kernel(q_ref, k_ref, v_ref, c_ref, ck_ref, o_ref, *, L, tq):
    nt = (((1,), (1,)), ((), ()))
    log2e = math.log2(math.e)
    head = lax.broadcasted_iota(jnp.int32, (tq, LANES), 1) == pl.program_id(1)
    for qi in range(L // tq):
        rows = slice(qi * tq, (qi + 1) * tq)
        q = q_ref[rows, :]
        cq = jnp.sum(jnp.where(head, c_ref[rows, :], 0.0), axis=-1, keepdims=True) * log2e
        m = l = acc = None
        for ki in range(qi + 1):
            cols = slice(ki * tq, (ki + 1) * tq)
            t = lax.dot_general(q, k_ref[cols, :], nt, preferred_element_type=F32) - ck_ref[:, cols] * log2e
            if ki == qi:
                keep = lax.broadcasted_iota(jnp.int32, (tq, tq), 1) <= lax.broadcasted_iota(jnp.int32, (tq, tq), 0)
                t = jnp.where(keep, t, -jnp.inf)
            mx = jnp.max(t, axis=-1, keepdims=True) + cq
            if ki == 0:
                m = mx
                p = jnp.exp2(t - (m - cq))
                l = jnp.sum(p, axis=-1, keepdims=True)
                acc = jnp.dot(p.astype(BF16), v_ref[cols, :], preferred_element_type=F32)
            else:
                m_new = jnp.maximum(m, mx)
                alpha = jnp.exp2(m - m_new)
                p = jnp.exp2(t - (m_new - cq))
                l = alpha * l + jnp.sum(p, axis=-1, keepdims=True)
                acc = alpha * acc + jnp.dot(p.astype(BF16), v_ref[cols, :], preferred_element_type=F32)
                m = m_new
        o_ref[rows, :] = (acc / l).astype(BF16)


def _attn_prompt(qb, kb, vb, c, ck, B, L):
    tq = min(L, 512)
    blk = pl.BlockSpec((L, HEAD_DIM), lambda b, h: (b, h))
    return pl.pallas_call(
        functools.partial(_attn_prompt_kernel, L=L, tq=tq),
        grid=(B, N_HEADS),
        in_specs=[blk, blk, blk,
                  pl.BlockSpec((L, LANES), lambda b, h: (b, 0)),
                  pl.BlockSpec((None, None, 1, L), lambda b, h: (b, h, 0, 0))],
        out_specs=blk, out_shape=jax.ShapeDtypeStruct((B * L, D_ATTN), BF16),
        compiler_params=_params(19, 2), name="attn_prompt",
    )(qb, kb, vb, c, ck)


def _attn_sample_kernel(pt_ref, q_ref, kn_ref, vn_ref, lfn_ref, *rest, n_pages, Q):
    k_refs = rest[:n_pages]
    v_refs = rest[n_pages:2 * n_pages]
    lf_refs = rest[2 * n_pages:3 * n_pages]
    o_ref, s_ref, d_ref = rest[3 * n_pages:]
    nt = (((1,), (1,)), ((), ()))
    H = N_HEADS
    M = Q * H
    page = k_refs[0].shape[0]
    PL = page * H
    RPP = PL // LANES

    qm = q_ref[...].reshape(M, HEAD_DIM)
    qmb = qm.astype(BF16)
    same_head = (lax.broadcasted_iota(jnp.int32, (M, LANES), 1) % H
                 == lax.broadcasted_iota(jnp.int32, (M, LANES), 0) % H)

    x = jnp.concatenate([r[...] for r in lf_refs], axis=0)
    lane = lax.broadcasted_iota(jnp.int32, x.shape, 1)
    sh = H
    while sh < LANES:
        x = x + jnp.where(lane >= sh, pltpu.roll(x, sh, axis=1), 0.0)
        sh *= 2
    tot = jnp.where(lane >= LANES - H, x, 0.0)
    inc = _cumsum_rows(tot)
    after = inc[inc.shape[0] - 1:, :] - inc
    sh = H
    while sh < LANES:
        after = after + pltpu.roll(after, sh, axis=1)
        sh *= 2
    row_tot = tot
    sh = H
    while sh < LANES:
        row_tot = row_tot + pltpu.roll(row_tot, sh, axis=1)
        sh *= 2
    d_ref[...] = after + (row_tot - x)

    lfn = lfn_ref[...]
    cum = [lfn[0:H, :]]
    for i in range(1, Q):
        cum.append(cum[-1] + lfn[i * H:(i + 1) * H, :])
    cum_q = jnp.concatenate(cum, axis=0)
    cum_k = jnp.concatenate([jnp.concatenate([c] * Q, axis=0) for c in cum], axis=1)

    qm_r = qmb.astype(F32)
    tile_q = lambda a: jnp.concatenate([a.astype(BF16).astype(F32)] * Q, axis=0)
    s_new = jnp.concatenate(
        [jnp.sum(qm_r * tile_q(kn_ref[i]), axis=-1, keepdims=True) for i in range(Q)], axis=1)
    s_new = s_new + (cum_q - cum_k)
    q_of_row = lax.broadcasted_iota(jnp.int32, (M, Q), 0) // H
    s_new = jnp.where(lax.broadcasted_iota(jnp.int32, (M, Q), 1) <= q_of_row, s_new, -jnp.inf)
    m = jnp.max(s_new, axis=-1, keepdims=True)

    for j in range(n_pages):
        kb = k_refs[j][...].reshape(PL, HEAD_DIM).astype(BF16)
        s = lax.dot_general(qmb, kb, nt, preferred_element_type=F32)
        for r in range(RPP):
            blk = s[:, r * LANES:(r + 1) * LANES] + (cum_q + d_ref[j * RPP + r:j * RPP + r + 1, :])
            blk = jnp.where(same_head, blk, -jnp.inf)
            s_ref[:, (j * RPP + r) * LANES:(j * RPP + r + 1) * LANES] = blk
            m = jnp.maximum(m, jnp.max(blk, axis=-1, keepdims=True))

    p_new = jnp.exp(s_new - m)
    l = jnp.sum(p_new, axis=-1, keepdims=True)
    p_new_r = p_new.astype(BF16).astype(F32)
    acc = p_new_r[:, 0:1] * tile_q(vn_ref[0])
    for i in range(1, Q):
        acc = acc + p_new_r[:, i:i + 1] * tile_q(vn_ref[i])
    for j in range(n_pages):
        p = jnp.exp(s_ref[:, j * PL:(j + 1) * PL] - m)
        l = l + jnp.sum(p, axis=-1, keepdims=True)
        vb = v_refs[j][...].reshape(PL, HEAD_DIM).astype(BF16)
        acc = acc + jnp.dot(p.astype(BF16), vb, preferred_element_type=F32)
    o_ref[...] = (acc / l).reshape(Q, H, HEAD_DIM)


def _attn_sample(page_table_flat, q4, k4, v4, lfn_col, cache_k, cache_v, cache_lf, layer, n_phys, n_pages, Bd, Q):
    page = cache_k.shape[1]
    base = layer * n_phys
    rpp = page * N_HEADS // LANES

    def paged(tail):
        return [pl.BlockSpec((None,) + tail, functools.partial(
            lambda b, pt, j: (pt[b * n_pages + j] + base,) + (0,) * len(tail), j=j)) for j in range(n_pages)]

    tok = pl.BlockSpec((Q, None, N_HEADS, HEAD_DIM), lambda b, pt: (0, b, 0, 0))
    grid_spec = pltpu.PrefetchScalarGridSpec(
        num_scalar_prefetch=1, grid=(Bd,),
        in_specs=[tok, tok, tok, pl.BlockSpec((None, Q * N_HEADS, 1), lambda b, pt: (b, 0, 0))]
        + paged((page, N_HEADS, HEAD_DIM)) + paged((page, N_HEADS, HEAD_DIM)) + paged((rpp, LANES)),
        out_specs=tok,
        scratch_shapes=[pltpu.VMEM((Q * N_HEADS, n_pages * page * N_HEADS), F32),
                        pltpu.VMEM((n_pages * rpp, LANES), F32)])
    return pl.pallas_call(
        functools.partial(_attn_sample_kernel, n_pages=n_pages, Q=Q),
        grid_spec=grid_spec, out_shape=jax.ShapeDtypeStruct((Q, Bd, N_HEADS, HEAD_DIM), F32),
        compiler_params=_params(41, 1), name="attn_sample",
    )(page_table_flat, q4, k4, v4, lfn_col, *([cache_k] * n_pages), *([cache_v] * n_pages), *([cache_lf] * n_pages))


def _s5_kernel(u_ref, h0r_ref, h0i_ref, lr_ref, li_ref, ldt_ref, btr_ref, bti_ref, ctr_ref, cti_ref, d_ref,
               y_ref, fr_ref, fi_ref, xr_ref, xi_ref, bdr_ref, bdi_ref, cdr_ref, cdi_ref, *scratch, T, NC, chained):
    @pl.when((pl.program_id(0) == 0) & (pl.program_id(1) == 0))
    def _():
        for ref in (bdr_ref, bdi_ref, cdr_ref, cdi_ref):
            ref[...] = jnp.zeros_like(ref)

    for g in range(SLAB // SSM_GROUP):
        rows_c, rows_p = slice(g * SSM_GROUP, (g + 1) * SSM_GROUP), slice(g * SSM_STATE, (g + 1) * SSM_STATE)
        bdr_ref[rows_c, rows_p] = btr_ref[g]
        bdi_ref[rows_c, rows_p] = bti_ref[g]
        cdr_ref[rows_p, rows_c] = ctr_ref[g]
        cdi_ref[rows_p, rows_c] = cti_ref[g]

    lr, li = lr_ref[...], li_ref[...]
    dt = jnp.exp(ldt_ref[...])
    mag = jnp.exp(lr * dt)
    ar, ai = mag * jnp.cos(li * dt), mag * jnp.sin(li * dt)
    den = lr * lr + li * li
    zr = ((ar - 1.0) * lr + ai * li) / den
    zi = (ai * lr - (ar - 1.0) * li) / den
    bdr, bdi = bdr_ref[...], bdi_ref[...]
    pr = (zr * bdr - zi * bdi).astype(BF16)
    pi = (zr * bdi + zi * bdr).astype(BF16)
    n_cb = SLAB_STATES // LANES
    per = MXU_DIM // LANES
    if chained:
        stage_ref, perm_ref = scratch[4:]
        for t in range(per):
            stage_ref[t] = u_ref[:, t * LANES:(t + 1) * LANES]
        for s in range(T):
            for t in range(per):
                perm_ref[s * NC:(s + 1) * NC, t * LANES:(t + 1) * LANES] = stage_ref[t, pl.ds(s, NC, stride=T), :]
        u = perm_ref[...]
    else:
        u = u_ref[...]
    ub = u.astype(BF16)
    for c2 in range(n_cb // per):
        cols = slice(c2 * MXU_DIM, (c2 + 1) * MXU_DIM)
        x_r = jnp.dot(ub, pr[:, cols], preferred_element_type=F32)
        x_i = jnp.dot(ub, pi[:, cols], preferred_element_type=F32)
        for t in range(per):
            xr_ref[c2 * per + t] = x_r[:, t * LANES:(t + 1) * LANES]
            xi_ref[c2 * per + t] = x_i[:, t * LANES:(t + 1) * LANES]

    def sweep(cb, hr, hi, store):
        cols = slice(cb * LANES, (cb + 1) * LANES)
        a_r, a_i = ar[:, cols], ai[:, cols]
        for s in range(T):
            rows = slice(s * NC, (s + 1) * NC)
            xr, xi = xr_ref[cb, rows, :], xi_ref[cb, rows, :]
            hr, hi = a_r * hr - a_i * hi + xr, a_r * hi + a_i * hr + xi
            if store:
                xr_ref[cb, rows, :] = hr
                xi_ref[cb, rows, :] = hi
        return hr, hi

    if chained:
        er_ref, ei_ref, gr_ref, gi_ref = scratch[:4]
        zero = jnp.zeros((NC, LANES), F32)
        for cb in range(n_cb):
            cols = slice(cb * LANES, (cb + 1) * LANES)
            er_ref[:, cols], ei_ref[:, cols] = sweep(cb, zero, zero, False)
        tr, ti = ar, ai
        for _ in range(T - 1):
            tr, ti = tr * ar - ti * ai, tr * ai + ti * ar

        def chunk_step(k, carry):
            sr, si = carry
            gr_ref[pl.ds(k, 1), :] = sr
            gi_ref[pl.ds(k, 1), :] = si
            er, ei = er_ref[pl.ds(k, 1), :], ei_ref[pl.ds(k, 1), :]
            return tr * sr - ti * si + er, tr * si + ti * sr + ei

        sr, si = lax.fori_loop(0, NC, chunk_step, (h0r_ref[...], h0i_ref[...]))
        fr_ref[...] = sr
        fi_ref[...] = si
        for cb in range(n_cb):
            cols = slice(cb * LANES, (cb + 1) * LANES)
            sweep(cb, gr_ref[:, cols], gi_ref[:, cols], True)
    else:
        for cb in range(n_cb):
            cols = slice(cb * LANES, (cb + 1) * LANES)
            hr, hi = sweep(cb, h0r_ref[:, cols], h0i_ref[:, cols], True)
            fr_ref[:, cols] = hr
            fi_ref[:, cols] = hi

    y = d_ref[...] * u
    for c2 in range(n_cb // per):
        rows = slice(c2 * MXU_DIM, (c2 + 1) * MXU_DIM)
        h_r = jnp.concatenate([xr_ref[c2 * per + t] for t in range(per)], axis=1).astype(BF16)
        h_i = jnp.concatenate([xi_ref[c2 * per + t] for t in range(per)], axis=1).astype(BF16)
        y = y + (jnp.dot(h_r, cdr_ref[rows, :].astype(BF16), preferred_element_type=F32)
                 - jnp.dot(h_i, cdi_ref[rows, :].astype(BF16), preferred_element_type=F32))
    if chained:
        for s in range(T):
            for t in range(per):
                stage_ref[t, pl.ds(s, NC, stride=T), :] = y[s * NC:(s + 1) * NC, t * LANES:(t + 1) * LANES]
        for t in range(per):
            y_ref[:, t * LANES:(t + 1) * LANES] = stage_ref[t]
    else:
        y_ref[...] = y


def _s5(u, h0r, h0i, lam_re, lam_im, log_dt, bt_re, bt_im, ct_re, ct_im, d_row, n_seq, T, NC, chained):
    rows = NC * T
    gps = SLAB // SSM_GROUP
    slab_row = lambda: pl.BlockSpec((1, SLAB_STATES), lambda b, s: (0, s))
    if chained:
        st_spec = pl.BlockSpec((None, 1, SLAB_STATES), lambda b, s: (b, 0, s))
        st_shape = jax.ShapeDtypeStruct((n_seq, 1, N_SLABS * SLAB_STATES), F32)
        scratch = [pltpu.VMEM((NC, SLAB_STATES), F32)] * 4 + [
            pltpu.VMEM((SLAB // LANES, rows, LANES), F32), pltpu.VMEM((rows, SLAB), F32)]
    else:
        st_spec = pl.BlockSpec((NC, SLAB_STATES), lambda b, s: (0, s))
        st_shape = jax.ShapeDtypeStruct((NC, N_SLABS * SLAB_STATES), F32)
        scratch = []
    return pl.pallas_call(
        functools.partial(_s5_kernel, T=T, NC=NC, chained=chained),
        grid=(n_seq, N_SLABS),
        in_specs=[pl.BlockSpec((rows, SLAB), lambda b, s: (b, s)), st_spec, st_spec,
                  slab_row(), slab_row(), slab_row(),
                  pl.BlockSpec((gps, SSM_GROUP, SSM_STATE), lambda b, s: (s, 0, 0)),
                  pl.BlockSpec((gps, SSM_GROUP, SSM_STATE), lambda b, s: (s, 0, 0)),
                  pl.BlockSpec((gps, SSM_STATE, SSM_GROUP), lambda b, s: (s, 0, 0)),
                  pl.BlockSpec((gps, SSM_STATE, SSM_GROUP), lambda b, s: (s, 0, 0)),
                  pl.BlockSpec((1, SLAB), lambda b, s: (0, s))],
        out_specs=[pl.BlockSpec((rows, SLAB), lambda b, s: (b, s)), st_spec, st_spec],
        out_shape=[jax.ShapeDtypeStruct((n_seq * rows, D_SSM), F32), st_shape, st_shape],
        scratch_shapes=[pltpu.VMEM((SLAB_STATES // LANES, rows, LANES), F32)] * 2
        + [pltpu.VMEM((SLAB, SLAB_STATES), F32)] * 2 + [pltpu.VMEM((SLAB_STATES, SLAB), F32)] * 2 + scratch,
        compiler_params=_params(42 if chained else 21, 2), name="s5",
    )(u, h0r, h0i, lam_re, lam_im, log_dt, bt_re, bt_im, ct_re, ct_im, d_row)


def _mixout_kernel(x_ref, att_ref, y_ref, wg_ref, bg_ref, wo_ref, o_ref):
    y = y_ref[...]
    z = 0.5 * y * (1.0 + jnp.tanh(math.sqrt(2.0 / math.pi) * (y + 0.044715 * (y * y * y))))
    gate = _sigmoid(jnp.dot(z.astype(BF16), wg_ref[...], preferred_element_type=F32) + bg_ref[...])
    ssm = (z * gate).astype(BF16)
    o_ref[...] = (x_ref[...] + jnp.dot(att_ref[...], wo_ref[0:D_ATTN, :], preferred_element_type=F32)
                  + jnp.dot(ssm, wo_ref[D_ATTN:, :], preferred_element_type=F32))


def _mixout(x, att, y, wg, bg, wo):
    R = x.shape[0]
    tm = min(R, 512)
    row = lambda n: pl.BlockSpec((tm, n), lambda i: (i, 0))
    full = lambda a: pl.BlockSpec(a.shape, lambda i: (0,) * a.ndim, pipeline_mode=pl.Buffered(1))
    return pl.pallas_call(
        _mixout_kernel, grid=(R // tm,),
        in_specs=[row(D_MODEL), row(D_ATTN), row(D_SSM), full(wg), full(bg), full(wo)],
        out_specs=row(D_MODEL), out_shape=jax.ShapeDtypeStruct((R, D_MODEL), F32),
        compiler_params=_params(42 if R > tm else 31, 1), name="mixout",
    )(x, att, y, wg, bg, wo)


def _ffn_kernel(x_ref, nw_ref, w1_ref, w3_ref, w2_ref, o_ref, h_ref):
    @pl.when(pl.program_id(1) == 0)
    def _():
        x = x_ref[...]
        h_ref[...] = _rms(x, nw_ref[...]).astype(BF16)
        o_ref[...] = x

    h = h_ref[...]
    for c in range(w1_ref.shape[1] // MXU_DIM):
        cols = slice(c * MXU_DIM, (c + 1) * MXU_DIM)
        a = jnp.dot(h, w1_ref[:, cols].astype(BF16), preferred_element_type=F32)
        b = jnp.dot(h, w3_ref[:, cols].astype(BF16), preferred_element_type=F32)
        g = (a * _sigmoid(a) * b).astype(BF16)
        o_ref[...] += jnp.dot(g, w2_ref[cols, :].astype(BF16), preferred_element_type=F32)


def _ffn(x, nw, w1, w3, w2, layer):
    R = x.shape[0]
    d_ff = w1.shape[2]
    tm = min(R, 1024)
    tf = 2 * MXU_DIM
    assert d_ff % tf == 0 and R % tm == 0
    return pl.pallas_call(
        _ffn_kernel, grid=(R // tm, d_ff // tf),
        in_specs=[pl.BlockSpec((tm, D_MODEL), lambda i, f: (i, 0), pipeline_mode=pl.Buffered(1)),
                  pl.BlockSpec((1, D_MODEL), lambda i, f: (0, 0)),
                  pl.BlockSpec((None, D_MODEL, tf), lambda i, f: (layer, 0, f)),
                  pl.BlockSpec((None, D_MODEL, tf), lambda i, f: (layer, 0, f)),
                  pl.BlockSpec((None, tf, D_MODEL), lambda i, f: (layer, f, 0))],
        out_specs=pl.BlockSpec((tm, D_MODEL), lambda i, f: (i, 0)),
        out_shape=jax.ShapeDtypeStruct((R, D_MODEL), F32),
        scratch_shapes=[pltpu.VMEM((tm, D_MODEL), BF16)],
        compiler_params=_params(60 if R > tm else 42, 2), name="ffn",
    )(x, nw, w1, w3, w2)


SUBLANES = 8
HIST = SUBLANES * (max(POOL_WINDOWS).bit_length() - 1)


def _pool_prompt_kernel(x_ref, nw_ref, pw_ref, ps_ref, o_ref, st_ref, ext_ref, lvl_ref, *, tm, tiles_per_seq):
    i = pl.program_id(0)
    t_in_seq = i % tiles_per_seq
    n = HIST + tm

    @pl.when(t_in_seq == 0)
    def _():
        ext_ref[0:HIST, :] = jnp.zeros((HIST, D_MODEL), F32)

    x = x_ref[...]
    hp = _rms(x, nw_ref[...])
    ext_ref[HIST:n, :] = hp
    pos = t_in_seq * tm + lax.broadcasted_iota(jnp.int32, (tm, 1), 0)
    for g, w in enumerate(POOL_WINDOWS):
        cols = slice(g * POOL_GROUP, (g + 1) * POOL_GROUP)
        n_levels = w.bit_length() - 1
        assert w == 1 << n_levels and SUBLANES * n_levels <= HIST
        acc = None
        for k in range(n_levels):
            start, sh = SUBLANES * (k + 1), 1 << k
            if k == 0:
                acc = ext_ref[start:n, cols] + ext_ref[start - sh:n - sh, cols]
            else:
                acc = lvl_ref[k - 1, start:n, :] + lvl_ref[k - 1, start - sh:n - sh, :]
            if k < n_levels - 1:
                lvl_ref[k, start:n, :] = acc
        acc = acc[HIST - SUBLANES * n_levels:, :]
        inv = 1.0 / jnp.minimum(pos + 1, w).astype(F32)
        pooled = (acc * inv - hp[:, cols]).astype(BF16)
        o_ref[:, cols] = x[:, cols] + jnp.dot(pooled, pw_ref[g], preferred_element_type=F32) * ps_ref[:, cols]
    st_ref[...] = ext_ref[tm + HIST - POOL_BUF:tm + HIST, :]
    ext_ref[0:HIST, :] = ext_ref[tm:tm + HIST, :]


def _pool_prompt(x, nw, pw, ps, B, L):
    tm = min(L, 512)
    assert L % tm == 0 and tm >= HIST
    tiles_per_seq = L // tm
    return pl.pallas_call(
        functools.partial(_pool_prompt_kernel, tm=tm, tiles_per_seq=tiles_per_seq),
        grid=(B * tiles_per_seq,),
        in_specs=[pl.BlockSpec((tm, D_MODEL), lambda i: (i, 0)),
                  pl.BlockSpec((1, D_MODEL), lambda i: (0, 0)),
                  pl.BlockSpec(pw.shape, lambda i: (0, 0, 0)),
                  pl.BlockSpec((1, D_MODEL), lambda i: (0, 0))],
        out_specs=[pl.BlockSpec((tm, D_MODEL), lambda i: (i, 0)),
                   pl.BlockSpec((None, POOL_BUF, D_MODEL), lambda i: (i // tiles_per_seq, 0, 0))],
        out_shape=[jax.ShapeDtypeStruct((B * L, D_MODEL), F32), jax.ShapeDtypeStruct((B, POOL_BUF, D_MODEL), F32)],
        scratch_shapes=[pltpu.VMEM((HIST + tm, D_MODEL), F32),
                        pltpu.VMEM((HIST // SUBLANES - 1, HIST + tm, POOL_GROUP), F32)],
        compiler_params=_params(31, 1), name="pool_prompt",
    )(x, nw, pw, ps)


def _pool_sample(x, nw, sp, pw, ps, Bd, Q, pos0):
    G = len(POOL_WINDOWS)
    hist = [pl.BlockSpec((Bd, POOL_GROUP), functools.partial(lambda g, j: (0, j * G + g), j=j)) for j in range(POOL_BUF)]
    col = pl.BlockSpec((Q * Bd, POOL_GROUP), lambda g: (0, g))
    return pl.pallas_call(
        functools.partial(_pool_sample_kernel, Bd=Bd, Q=Q, pos0=pos0),
        grid=(G,),
        in_specs=[pl.BlockSpec((Q * Bd, D_MODEL), lambda g: (0, 0)),
                  pl.BlockSpec((1, D_MODEL), lambda g: (0, 0)),
                  pl.BlockSpec((None, POOL_GROUP, POOL_GROUP), lambda g: (g, 0, 0)),
                  pl.BlockSpec((1, POOL_GROUP), lambda g: (0, g))] + hist,
        out_specs=[col, col],
        out_shape=[jax.ShapeDtypeStruct((Q * Bd, D_MODEL), F32)] * 2,
        scratch_shapes=[pltpu.VMEM((Q * Bd, 1), F32)],
        compiler_params=_params(23, 1), name="pool_sample",
    )(x, nw, pw, ps, *([sp] * POOL_BUF))


def _pool_sample_kernel(x_ref, nw_ref, pw_ref, ps_ref, *rest, Bd, Q, pos0):
    hist_refs = rest[:POOL_BUF]
    o_ref, hs_ref, inv_ref = rest[POOL_BUF:]
    g = pl.program_id(0)

    @pl.when(g == 0)
    def _():
        x = x_ref[...]
        inv_ref[...] = lax.rsqrt(jnp.mean(x * x, axis=-1, keepdims=True) + RMS_EPS)

    for wi, w in enumerate(POOL_WINDOWS):
        @pl.when(g == wi)
        def _(wi=wi, w=w):
            cols = slice(wi * POOL_GROUP, (wi + 1) * POOL_GROUP)
            xs, hs = [], []
            for q in range(Q):
                rows = slice(q * Bd, (q + 1) * Bd)
                xq = x_ref[rows, cols]
                xs.append(xq)
                hs.append(xq * inv_ref[rows, :] * nw_ref[:, cols])
                hs_ref[rows, :] = hs[q]
            ext = lambda j: hist_refs[j][...] if j < POOL_BUF else hs[j - POOL_BUF]
            for q in range(Q):
                acc = hs[q]
                for k in range(1, w):
                    acc = acc + ext(POOL_BUF + q - k)
                cnt = float(min(pos0 + q + 1, w))
                pooled = (acc / cnt - hs[q]).astype(BF16)
                val = jnp.dot(pooled, pw_ref[...], preferred_element_type=F32) * ps_ref[...]
                o_ref[q * Bd:(q + 1) * Bd, :] = xs[q] + val


def kernel(x_prompt, x_sample, cache_k, cache_v, cache_logf, state_s5_re, state_s5_im, state_pool, page_table, norm_mix_w, norm_ffn_w, w_in, b_f, q_norm_w, k_norm_w, s5_lam_re, s5_lam_im, s5_log_dt, s5_b_re, s5_b_im, s5_c_re, s5_c_im, s5_d, w_glu, b_glu, w_out, pool_w, pool_scale, ffn_w1, ffn_w3, ffn_w2):
    B, L, D = x_prompt.shape
    Bd, Q, _ = x_sample.shape
    depth = norm_mix_w.shape[0]
    n_attn, n_phys, page = cache_k.shape[:3]
    n_pages = page_table.shape[1]
    past_len = n_pages * page
    T = 16
    assert D == D_MODEL and L % T == 0

    xp = x_prompt.reshape(B * L, D)
    xs = jnp.swapaxes(x_sample, 0, 1).reshape(Q * Bd, D)
    from_tm = lambda a, *tail: jnp.swapaxes(a.reshape(Q, Bd, *tail), 0, 1)
    pt_flat = page_table.reshape(-1)
    assert (page * N_HEADS) % LANES == 0
    ck_flat = cache_k.reshape(n_attn * n_phys, page, N_HEADS, HEAD_DIM)
    cv_flat = cache_v.reshape(n_attn * n_phys, page, N_HEADS, HEAD_DIM)
    clf_flat = cache_logf.reshape(n_attn * n_phys, page * N_HEADS // LANES, LANES)

    kp, vp, fp, srp, sip, plp = [], [], [], [], [], []
    ksm, vsm, fsm, srs, sis, pls = [], [], [], [], [], []
    for layer in range(depth):
        i = layer // 2
        nw = norm_mix_w[layer].reshape(1, D)
        if layer % 2 == 0:
            w = jnp.swapaxes(w_in[i], 0, 1)
            wq = w[:D_ATTN].astype(BF16)
            wk = w[D_ATTN:2 * D_ATTN].astype(BF16)
            wv = w[2 * D_ATTN:3 * D_ATTN].astype(BF16)
            wf = jnp.pad(w[3 * D_ATTN:3 * D_ATTN + N_HEADS], ((0, LANES - N_HEADS), (0, 0))).astype(BF16)
            wu = w[3 * D_ATTN + N_HEADS:].astype(BF16)
            bf = jnp.pad(b_f[i], (0, LANES - N_HEADS)).reshape(1, LANES)
            qg = q_norm_w[i].reshape(1, HEAD_DIM)
            kg = k_norm_w[i].reshape(1, HEAD_DIM)
            wg = w_glu[i].astype(BF16)
            bg = b_glu[i].reshape(1, D_SSM)
            wo = w_out[i].astype(BF16)
            lam_re = s5_lam_re[i].reshape(1, -1)
            lam_im = s5_lam_im[i].reshape(1, -1)
            log_dt = jnp.broadcast_to(s5_log_dt[i][:, None], (N_SSM_GROUPS, SSM_STATE)).reshape(1, -1)
            bd_re = jnp.swapaxes(s5_b_re[i], 1, 2)
            bd_im = jnp.swapaxes(s5_b_im[i], 1, 2)
            cd_re = jnp.swapaxes(s5_c_re[i], 1, 2)
            cd_im = jnp.swapaxes(s5_c_im[i], 1, 2)
            d_row = s5_d[i].reshape(1, D_SSM)

            qb, k, kb, v, vb, u, lf, c = _inproj(xp, nw, wq, wk, wv, wu, wf, bf, qg, kg, seq_len=L,
                                                 q_scale=HEAD_DIM ** -0.5 * math.log2(math.e))
            c_t = jnp.swapaxes(c[:, :N_HEADS].reshape(B, L, N_HEADS), 1, 2)
            att = _attn_prompt(qb, kb, vb, c, c_t[:, :, None, :], B, L)
            zeros = jnp.zeros((B, 1, N_SLABS * SLAB_STATES), F32)
            y, h_re, h_im = _s5(u, zeros, zeros, lam_re, lam_im, log_dt, bd_re, bd_im, cd_re, cd_im, d_row,
                                n_seq=B, T=T, NC=L // T, chained=True)
            xp = _mixout(xp, att, y, wg, bg, wo)
            kp.append(k.reshape(B, L, N_HEADS, HEAD_DIM))
            vp.append(v.reshape(B, L, N_HEADS, HEAD_DIM))
            fp.append(lf[:, :N_HEADS].reshape(B, L, N_HEADS))
            srp.append(h_re.reshape(B, N_SSM_GROUPS, SSM_STATE))
            sip.append(h_im.reshape(B, N_SSM_GROUPS, SSM_STATE))

            qb, k, kb, v, vb, u, lf = _inproj(xs, nw, wq, wk, wv, wu, wf, bf, qg, kg, seq_len=None,
                                              q_scale=HEAD_DIM ** -0.5)
            lfn = lf[:, :N_HEADS].reshape(Q, Bd, N_HEADS)
            heads = lambda a: a.astype(F32).reshape(Q, Bd, N_HEADS, HEAD_DIM)
            att = _attn_sample(pt_flat, heads(qb), heads(k), heads(v),
                               jnp.swapaxes(lfn, 0, 1).reshape(Bd, Q * N_HEADS, 1),
                               ck_flat, cv_flat, clf_flat, i, n_phys, n_pages, Bd, Q).astype(BF16)
            y, h_re, h_im = _s5(u, state_s5_re[i].reshape(Bd, -1), state_s5_im[i].reshape(Bd, -1),
                                lam_re, lam_im, log_dt, bd_re, bd_im, cd_re, cd_im, d_row,
                                n_seq=1, T=Q, NC=Bd, chained=False)
            xs = _mixout(xs, att.reshape(Q * Bd, D_ATTN), y, wg, bg, wo)
            ksm.append(from_tm(k, N_HEADS, HEAD_DIM))
            vsm.append(from_tm(v, N_HEADS, HEAD_DIM))
            fsm.append(jnp.swapaxes(lfn, 0, 1))
            srs.append(h_re.reshape(Bd, N_SSM_GROUPS, SSM_STATE))
            sis.append(h_im.reshape(Bd, N_SSM_GROUPS, SSM_STATE))
        else:
            pw = pool_w[i].astype(BF16)
            ps = pool_scale[i].reshape(1, D)
            xp, st = _pool_prompt(xp, nw, pw, ps, B, L)
            plp.append(st)
            xs, hs = _pool_sample(xs, nw, state_pool[i].reshape(Bd, POOL_BUF * D), pw, ps, Bd, Q, past_len)
            pls.append(jnp.concatenate([state_pool[i], from_tm(hs, D)], axis=1)[:, -POOL_BUF:])
        nf = norm_ffn_w[layer].reshape(1, D)
        xp = _ffn(xp, nf, ffn_w1, ffn_w3, ffn_w2, layer)
        xs = _ffn(xs, nf, ffn_w1, ffn_w3, ffn_w2, layer)
    return (xp.reshape(B, L, D), from_tm(xs, D),
            jnp.stack(kp), jnp.stack(vp), jnp.stack(fp), jnp.stack(srp), jnp.stack(sip), jnp.stack(plp),
            jnp.stack(ksm), jnp.stack(vsm), jnp.stack(fsm), jnp.stack(srs), jnp.stack(sis), jnp.stack(pls))
```
